```python
import math
import jax, jax.numpy as jnp
from jax import lax
import numpy as np

D_MODEL = 1024
BATCH = 2
SEQ = 8192
DEPTH = 1
DEC_BATCH = 128
DEC_SEQ = 1
PAST_LEN = 2048
PAGE_SIZE = 128

MIX_WIDTH = D_MODEL
R_WIDTH = MIX_WIDTH // 2
A_WIDTH = MIX_WIDTH - R_WIDTH
R_EXPAND = 128
R_HEADS = R_WIDTH // R_EXPAND
R_DK = R_EXPAND
R_DV = R_WIDTH // R_HEADS
A_HEADS = 4
A_DV = A_WIDTH // A_HEADS
A_DK = A_DV // 2
D_FF = -(-8 * D_MODEL // (3 * 256)) * 256
CHUNK = 64
Q_BLOCK = 128
EPS = 1e-6
PROJ_SIZES = (R_HEADS * R_DK, R_HEADS * R_DK, R_WIDTH, R_WIDTH,
              2 * A_HEADS * A_DK, 2 * A_HEADS * A_DK, A_WIDTH)
D_IN = sum(PROJ_SIZES)

kernel_name = "hymba_hgrn2_diffattn_decode_step"


def rmsnorm(x, g):
    xf = x.astype(jnp.float32)
    y = xf * lax.rsqrt(jnp.mean(xf * xf, axis=-1, keepdims=True) + EPS)
    return (y * g.astype(jnp.float32)).astype(x.dtype)


def project(h, w_in_l, lb_l):
    B, T, _ = h.shape
    p = jnp.einsum('btd,de->bte', h, w_in_l)
    offs = np.cumsum(PROJ_SIZES)[:-1].tolist()
    rq, rf, ri, rg, aq, ak, av = jnp.split(p, offs, axis=-1)
    f = lb_l + (1.0 - lb_l) * jax.nn.sigmoid(rf.astype(jnp.float32))
    logf = jnp.log(f).reshape(B, T, R_HEADS, R_DK)
    rk = (1.0 - f).reshape(B, T, R_HEADS, R_DK)
    rq = jax.nn.silu(rq.astype(jnp.float32)).reshape(B, T, R_HEADS, R_DK)
    rv = ri.astype(jnp.float32).reshape(B, T, R_HEADS, R_DV)
    aq = aq.reshape(B, T, A_HEADS, 2, A_DK)
    ak = ak.reshape(B, T, A_HEADS, 2, A_DK)
    av = av.reshape(B, T, A_HEADS, A_DV)
    return rq, rk, rv, logf, rg, aq, ak, av


def hgrn_chunked(q, k, v, logf, s0):
    B, T, H, DK = q.shape
    DV = v.shape[-1]
    n = T // CHUNK

    def to_chunks(a):
        return a.reshape(B, n, CHUNK, H, a.shape[-1]).transpose(1, 0, 3, 2, 4)

    causal = jnp.tril(jnp.ones((CHUNK, CHUNK), dtype=bool))[:, :, None]

    def step(S, blk):
        qb, kb, vb, gb = blk
        b = jnp.cumsum(gb, axis=2)
        o_inter = jnp.einsum('bhtk,bhkv->bhtv', qb * jnp.exp(b), S)
        rel = b[:, :, :, None, :] - b[:, :, None, :, :]
        decay = jnp.exp(jnp.where(causal, rel, -jnp.inf))
        A = jnp.einsum('bhtk,bhsk,bhtsk->bhts', qb, kb, decay)
        o = o_inter + jnp.einsum('bhts,bhsv->bhtv', A, vb)
        b_last = b[:, :, -1:, :]
        S_new = jnp.exp(b_last[:, :, 0, :])[..., None] * S + jnp.einsum(
            'bhsk,bhsv->bhkv', kb * jnp.exp(b_last - b), vb)
        return S_new, o

    S_fin, o = lax.scan(step, s0, (to_chunks(q), to_chunks(k), to_chunks(v), to_chunks(logf)))
    o = o.transpose(1, 0, 3, 2, 4).reshape(B, T, H, DV)
    return o, S_fin


def hgrn_recurrent(q, k, v, logf, s0):
    def step(S, xs):
        qt, kt, vt, gt = xs
        S = jnp.exp(gt)[..., None] * S + kt[..., None] * vt[..., None, :]
        return S, jnp.einsum('bhk,bhkv->bhv', qt, S)

    S_fin, o = lax.scan(step, s0, (q.swapaxes(0, 1), k.swapaxes(0, 1),
                                   v.swapaxes(0, 1), logf.swapaxes(0, 1)))
    return o.swapaxes(0, 1), S_fin


def diff_attend(q, k, v, q_pos, lam, lam_init, subln_l):
    s = jnp.einsum('bqhmd,bkhmd->bhmqk', q.astype(jnp.float32), k.astype(jnp.float32)) * (A_DK ** -0.5)
    mask = jnp.arange(k.shape[1])[None, :] <= q_pos[:, None]
    p = jax.nn.softmax(jnp.where(mask, s, -jnp.inf), axis=-1)
    attn = p[:, :, 0] - lam * p[:, :, 1]
    o = jnp.einsum('bhqk,bkhv->bqhv', attn, v.astype(jnp.float32))
    return rmsnorm(o, subln_l) * (1.0 - lam_init)


def diff_prompt(aq, ak, av, lam, lam_init, subln_l):
    B, T = aq.shape[:2]
    nb = T // Q_BLOCK
    qb = aq.reshape(B, nb, Q_BLOCK, A_HEADS, 2, A_DK).transpose(1, 0, 2, 3, 4, 5)
    pos = jnp.arange(T).reshape(nb, Q_BLOCK)
    o = lax.map(lambda a: diff_attend(a[0], ak, av, a[1], lam, lam_init, subln_l), (qb, pos))
    return o.transpose(1, 0, 2, 3, 4).reshape(B, T, A_HEADS, A_DV)


def diff_sample(aq, ak, av, cache_k_l, cache_v_l, page_table, lam, lam_init, subln_l):
    DB, T = aq.shape[:2]
    past = page_table.shape[1] * PAGE_SIZE
    k_past = cache_k_l[page_table].reshape(DB, past, A_HEADS, 2, A_DK)
    v_past = cache_v_l[page_table].reshape(DB, past, A_HEADS, A_DV)
    k = jnp.concatenate([k_past.astype(ak.dtype), ak], axis=1)
    v = jnp.concatenate([v_past.astype(av.dtype), av], axis=1)
    q_pos = past + jnp.arange(T)
    return diff_attend(aq, k, v, q_pos, lam, lam_init, subln_l)


def merge(o_r, g, o_a, r_gnorm_l, w_out_l, dtype):
    B, T = o_r.shape[:2]
    o_r = rmsnorm(o_r, r_gnorm_l) * jax.nn.silu(g.astype(jnp.float32)).reshape(B, T, R_HEADS, R_DV)
    o = jnp.concatenate([o_r.reshape(B, T, R_WIDTH), o_a.reshape(B, T, A_WIDTH)], axis=-1).astype(dtype)
    return jnp.einsum('bte,ed->btd', o, w_out_l)


def swiglu(h, wg, wu, wd):
    return jnp.einsum('btf,fd->btd', jax.nn.silu(jnp.einsum('btd,df->btf', h, wg)) * jnp.einsum('btd,df->btf', h, wu), wd)


def setup_inputs(seed: int = 0) -> dict:
    key = jax.random.key(seed)
    ks = jax.random.split(key, 24)
    n_pages = PAST_LEN // PAGE_SIZE
    n_phys = (DEC_BATCH * n_pages * 5) // 4
    f32 = jnp.float32
    nrm = lambda k, shape, s: jax.random.normal(k, shape, f32) * s
    x_prompt = nrm(ks[0], (BATCH, SEQ, D_MODEL), 1.0)
    x_sample = nrm(ks[1], (DEC_BATCH, DEC_SEQ, D_MODEL), 1.0)
    cache_k = nrm(ks[2], (DEPTH, n_phys, PAGE_SIZE, A_HEADS, 2, A_DK), 1.0)
    cache_v = nrm(ks[3], (DEPTH, n_phys, PAGE_SIZE, A_HEADS, A_DV), 1.0)
    state_hgrn = nrm(ks[4], (DEPTH, DEC_BATCH, R_HEADS, R_DK, R_DV), 0.5)
    page_table = jax.random.permutation(ks[5], n_phys)[:DEC_BATCH * n_pages].reshape(DEC_BATCH, n_pages).astype(jnp.int32)
    return {
        "x_prompt": x_prompt,
        "x_sample": x_sample,
        "cache_k": cache_k,
        "cache_v": cache_v,
        "state_hgrn": state_hgrn,
        "page_table": page_table,
        "w_in": nrm(ks[6], (DEPTH, D_MODEL, D_IN), D_MODEL ** -0.5),
        "w_out": nrm(ks[7], (DEPTH, MIX_WIDTH, D_MODEL), MIX_WIDTH ** -0.5),
        "lb_param": nrm(ks[8], (DEPTH + 1, R_HEADS * R_DK), 0.1),
        "r_gnorm": 1.0 + nrm(ks[9], (DEPTH, R_DV), 0.02),
        "lam_q1": nrm(ks[10], (DEPTH, A_DK), 0.1),
        "lam_k1": nrm(ks[11], (DEPTH, A_DK), 0.1),
        "lam_q2": nrm(ks[12], (DEPTH, A_DK), 0.1),
        "lam_k2": nrm(ks[13], (DEPTH, A_DK), 0.1),
        "a_subln": 1.0 + nrm(ks[14], (DEPTH, A_DV), 0.02),
        "norm_mix": 1.0 + nrm(ks[15], (DEPTH, D_MODEL), 0.02),
        "norm_ffn": 1.0 + nrm(ks[16], (DEPTH, D_MODEL), 0.02),
        "w_gate": nrm(ks[17], (DEPTH, D_MODEL, D_FF), D_MODEL ** -0.5),
        "w_up": nrm(ks[18], (DEPTH, D_MODEL, D_FF), D_MODEL ** -0.5),
        "w_down": nrm(ks[19], (DEPTH, D_FF, D_MODEL), D_FF ** -0.5),
        "norm_final": 1.0 + nrm(ks[20], (D_MODEL,), 0.02),
    }


def reference(x_prompt, x_sample, cache_k, cache_v, state_hgrn, page_table, w_in, w_out, lb_param,
              r_gnorm, lam_q1, lam_k1, lam_q2, lam_k2, a_subln, norm_mix, norm_ffn, w_gate, w_up,
              w_down, norm_final):
    hp, hs = x_prompt, x_sample
    B = x_prompt.shape[0]
    lb_all = jnp.cumsum(jax.nn.softmax(lb_param.astype(jnp.float32), axis=0), axis=0)
    kp, vp, sp, kss, vss, sss = [], [], [], [], [], []
    for l in range(DEPTH):
        lam_init = 0.8 - 0.6 * math.exp(-0.3 * l)
        lam = (jnp.exp(jnp.sum(lam_q1[l].astype(jnp.float32) * lam_k1[l].astype(jnp.float32)))
               - jnp.exp(jnp.sum(lam_q2[l].astype(jnp.float32) * lam_k2[l].astype(jnp.float32))) + lam_init)
        rq, rk, rv, logf, rg, aq, ak, av = project(rmsnorm(hp, norm_mix[l]), w_in[l], lb_all[l])
        s0 = jnp.zeros((B, R_HEADS, R_DK, R_DV), jnp.float32)
        o_r, s_new = hgrn_chunked(rq, rk, rv, logf, s0)
        o_a = diff_prompt(aq, ak, av, lam, lam_init, a_subln[l])
        hp = hp + merge(o_r, rg, o_a, r_gnorm[l], w_out[l], hp.dtype)
        hp = hp + swiglu(rmsnorm(hp, norm_ffn[l]), w_gate[l], w_up[l], w_down[l])
        kp.append(ak); vp.append(av); sp.append(s_new)
        rq, rk, rv, logf, rg, aq, ak, av = project(rmsnorm(hs, norm_mix[l]), w_in[l], lb_all[l])
        o_r, s_new = hgrn_recurrent(rq, rk, rv, logf, state_hgrn[l].astype(jnp.float32))
        o_a = diff_sample(aq, ak, av, cache_k[l], cache_v[l], page_table, lam, lam_init, a_subln[l])
        hs = hs + merge(o_r, rg, o_a, r_gnorm[l], w_out[l], hs.dtype)
        hs = hs + swiglu(rmsnorm(hs, norm_ffn[l]), w_gate[l], w_up[l], w_down[l])
        kss.append(ak); vss.append(av); sss.append(s_new)
    y_prompt = rmsnorm(hp, norm_final)
    y_sample = rmsnorm(hs, norm_final)
    k_prompt = jnp.stack(kp)
    v_prompt = jnp.stack(vp)
    s_prompt = jnp.stack(sp)
    k_sample = jnp.stack(kss)
    v_sample = jnp.stack(vss)
    s_sample = jnp.stack(sss)
    return (y_prompt, y_sample, k_prompt, v_prompt, s_prompt, k_sample, v_sample, s_sample)
```

```python
import functools
import math

import jax
import jax.numpy as jnp
from jax import lax
from jax.experimental import pallas as pl
from jax.experimental.pallas import tpu as pltpu

F32 = jnp.float32
BF16 = jnp.bfloat16

EPS = 1e-6
HEAD = 128
N_HEADS = 4
A_DK = 64
GROUP = N_HEADS * HEAD
N_PIECES = 7
HGRN_CHUNK = 64
HGRN_BLOCK = 256
ATT_BLOCK = 512
ROW_BLOCK = 256
FF_CHUNK = 256
VMEM_LIMIT = 48 * 1024 * 1024


def _sigmoid(x):
    return 1.0 / (1.0 + jnp.exp(-x))


def _rms(x, w):
    return x * lax.rsqrt(jnp.mean(x * x, axis=-1, keepdims=True) + EPS) * w


def _const_spec(shape):
    return pl.BlockSpec(shape, lambda *_: (0,) * len(shape))


def _inproj_kernel(layer, x_ref, nw_ref, w_ref, lb_ref,
                   qr_ref, logf_ref, kr_ref, vr_ref, gate_ref,
                   aq_ref, ak_ref, av_ref, akb_ref, avb_ref):
    h = _rms(x_ref[...], nw_ref[...]).astype(BF16)

    def piece(j):
        return jnp.dot(h, w_ref[:, j * GROUP:(j + 1) * GROUP], preferred_element_type=F32)

    lbp = lb_ref[...]
    e = jnp.exp(lbp - jnp.max(lbp, axis=0, keepdims=True))
    lb = jnp.sum(e[:layer + 1], axis=0, keepdims=True) / jnp.sum(e, axis=0, keepdims=True)

    rq = piece(0)
    qr_ref[...] = rq * _sigmoid(rq)
    f = lb + (1.0 - lb) * _sigmoid(piece(1))
    logf_ref[...] = jnp.log(f)
    kr_ref[...] = 1.0 - f
    vr_ref[...] = piece(2)
    rg = piece(3)
    gate_ref[...] = rg * _sigmoid(rg)
    aq_ref[...] = (piece(4) * (A_DK ** -0.5)).astype(BF16)
    ak = piece(5)
    ak_ref[...] = ak
    akb_ref[...] = ak.astype(BF16)
    av = piece(6)
    av_ref[...] = av
    avb_ref[...] = av.astype(BF16)


def _inproj(x, norm_w, w_bf, lb_param, layer, tm):
    rows, d = x.shape
    d_in = w_bf.shape[1]
    f32_out = jax.ShapeDtypeStruct((rows, GROUP), F32)
    bf_out = jax.ShapeDtypeStruct((rows, GROUP), BF16)
    row_spec = pl.BlockSpec((tm, GROUP), lambda i: (i, 0))
    return pl.pallas_call(
        functools.partial(_inproj_kernel, layer),
        grid=(rows // tm,),
        in_specs=[pl.BlockSpec((tm, d), lambda i: (i, 0)),
                  _const_spec((1, d)),
                  _const_spec((d, d_in)),
                  _const_spec(lb_param.shape)],
        out_specs=[row_spec] * 10,
        out_shape=[f32_out] * 5 + [bf_out, f32_out, f32_out, bf_out, bf_out],
        compiler_params=pltpu.CompilerParams(
            dimension_semantics=("parallel",), vmem_limit_bytes=VMEM_LIMIT),
        name="inproj",
    )(x, norm_w, w_bf, lb_param)


def _hgrn_prompt_kernel(q_ref, g_ref, k_ref, v_ref, o_ref, s_ref, st_scr):
    i = pl.program_id(1)

    @pl.when(i == 0)
    def _():
        st_scr[...] = jnp.zeros_like(st_scr)

    c = HGRN_CHUNK
    row = lax.broadcasted_iota(jnp.int32, (c, c), 0)
    col = lax.broadcasted_iota(jnp.int32, (c, c), 1)
    tri = col <= row
    tri_bf = jnp.where(tri, 1.0, 0.0).astype(BF16)
    nt = (((1,), (1,)), ((), ()))
    tn = (((0,), (0,)), ((), ()))

    for ci in range(HGRN_BLOCK // c):
        rs = slice(ci * c, (ci + 1) * c)
        for h in range(N_HEADS):
            cs = slice(h * HEAD, (h + 1) * HEAD)
            g = g_ref[rs, cs]
            g1 = g.astype(BF16)
            r1 = g - g1.astype(F32)
            g2 = r1.astype(BF16)
            g3 = (r1 - g2.astype(F32)).astype(BF16)
            b = (jnp.dot(tri_bf, g1, preferred_element_type=F32)
                 + jnp.dot(tri_bf, g2, preferred_element_type=F32)
                 + jnp.dot(tri_bf, g3, preferred_element_type=F32))
            b_last = b[c - 1:c, :]
            qt = (q_ref[rs, cs] * jnp.exp(b)).astype(BF16)
            kt = k_ref[rs, cs] * jnp.exp(-b)
            k2 = (kt * jnp.exp(b_last)).astype(BF16)
            v = v_ref[rs, cs].astype(BF16)
            a = lax.dot_general(qt, kt.astype(BF16), nt, preferred_element_type=F32)
            a = jnp.where(tri, a, 0.0).astype(BF16)
            st = st_scr[h]
            o = (lax.dot_general(qt, st.astype(BF16), nt, preferred_element_type=F32)
                 + jnp.dot(a, v, preferred_element_type=F32))
            o_ref[rs, cs] = o
            st_scr[h] = st * jnp.exp(b_last) + lax.dot_general(v, k2, tn, preferred_element_type=F32)

    @pl.when(i == pl.num_programs(1) - 1)
    def _():
        for h in range(N_HEADS):
            s_ref[0, h] = st_scr[h].T


def _hgrn_prompt(qr, logf, kr, vr, batch, seq):
    nb = seq // HGRN_BLOCK
    spec = pl.BlockSpec((HGRN_BLOCK, GROUP), lambda b, i: (b * nb + i, 0))
    return pl.pallas_call(
        _hgrn_prompt_kernel,
        grid=(batch, nb),
        in_specs=[spec] * 4,
        out_specs=[spec, pl.BlockSpec((1, N_HEADS, HEAD, HEAD), lambda b, i: (b, 0, 0, 0))],
        out_shape=[jax.ShapeDtypeStruct((batch * seq, GROUP), F32),
                   jax.ShapeDtypeStruct((batch, N_HEADS, HEAD, HEAD), F32)],
        scratch_shapes=[pltpu.VMEM((N_HEADS, HEAD, HEAD), F32)],
        compiler_params=pltpu.CompilerParams(
            dimension_semantics=("parallel", "arbitrary"), vmem_limit_bytes=VMEM_LIMIT),
        name="hgrn_prompt",
    )(qr, logf, kr, vr)


def _hgrn_sample_kernel(q_ref, g_ref, k_ref, v_ref, s_in_ref, o_ref, s_out_ref):
    nb = q_ref.shape[0]
    for h in range(N_HEADS):
        cs = slice(h * HEAD, (h + 1) * HEAD)
        qt = q_ref[:, cs].T
        ft = jnp.exp(g_ref[:, cs]).T
        kt = k_ref[:, cs].T
        for j in range(nb):
            s_new = ft[:, j:j + 1] * s_in_ref[j, h] + kt[:, j:j + 1] * v_ref[j:j + 1, cs]
            s_out_ref[j, h] = s_new
            o_ref[j:j + 1, cs] = jnp.sum(qt[:, j:j + 1] * s_new, axis=0, keepdims=True)


def _hgrn_sample(qr, logf, kr, vr, state):
    n = qr.shape[0]
    nb = 8
    row_spec = pl.BlockSpec((nb, GROUP), lambda i: (i, 0))
    st_spec = pl.BlockSpec((nb, N_HEADS, HEAD, HEAD), lambda i: (i, 0, 0, 0))
    return pl.pallas_call(
        _hgrn_sample_kernel,
        grid=(n // nb,),
        in_specs=[row_spec] * 4 + [st_spec],
        out_specs=[row_spec, st_spec],
        out_shape=[jax.ShapeDtypeStruct((n, GROUP), F32),
                   jax.ShapeDtypeStruct(state.shape, F32)],
        compiler_params=pltpu.CompilerParams(
            dimension_semantics=("parallel",), vmem_limit_bytes=VMEM_LIMIT),
        name="hgrn_sample",
    )(qr, logf, kr, vr, state)


def _lambda(lq1_ref, lk1_ref, lq2_ref, lk2_ref, lam_init):
    a = jnp.sum(lq1_ref[...] * lk1_ref[...], axis=-1, keepdims=True)
    b = jnp.sum(lq2_ref[...] * lk2_ref[...], axis=-1, keepdims=True)
    return jnp.exp(a) - jnp.exp(b) + lam_init


def _attn_prompt_kernel(lam_init, qi_tab, kj_tab,
                        q_ref, k_ref, v_ref, lq1_ref, lk1_ref, lq2_ref, lk2_ref, sub_ref,
                        o_ref, m_scr, l_scr, acc_scr):
    p = pl.program_id(2)
    qi = qi_tab[p]
    kj = kj_tab[p]

    @pl.when(kj == 0)
    def _():
        m_scr[...] = jnp.full_like(m_scr, -jnp.inf)
        l_scr[...] = jnp.zeros_like(l_scr)
        acc_scr[...] = jnp.zeros_like(acc_scr)

    q = q_ref[...]
    k = k_ref[...]
    v = v_ref[...]
    tq, tk = q.shape[0], k.shape[0]
    lane = lax.broadcasted_iota(jnp.int32, q.shape, 1)
    row = lax.broadcasted_iota(jnp.int32, (tq, tk), 0)
    col = lax.broadcasted_iota(jnp.int32, (tq, tk), 1)
    visible = jnp.logical_or(col <= row, kj < qi)
    nt = (((1,), (1,)), ((), ()))
    zero = jnp.zeros_like(q)
    for mi, qm in enumerate((jnp.where(lane < A_DK, q, zero), jnp.where(lane >= A_DK, q, zero))):
        s = lax.dot_general(qm, k, nt, preferred_element_type=F32)
        s = jnp.where(visible, s, -jnp.inf)
        m_prev = m_scr[mi]
        m_new = jnp.maximum(m_prev, jnp.max(s, axis=-1, keepdims=True))
        alpha = jnp.exp(m_prev - m_new)
        pr = jnp.exp(s - m_new)
        l_scr[mi] = alpha * l_scr[mi] + jnp.sum(pr, axis=-1, keepdims=True)
        acc_scr[mi] = alpha * acc_scr[mi] + jnp.dot(pr.astype(BF16), v, preferred_element_type=F32)
        m_scr[mi] = m_new

    @pl.when(kj == qi)
    def _():
        lam = _lambda(lq1_ref, lk1_ref, lq2_ref, lk2_ref, lam_init)
        o = acc_scr[0] / l_scr[0] - lam * (acc_scr[1] / l_scr[1])
        o_ref[...] = (_rms(o, sub_ref[...]) * (1.0 - lam_init)).astype(o_ref.dtype)


def _attn_prompt(aq_bf, ak_bf, av_bf, lq1, lk1, lq2, lk2, subln, lam_init, batch, seq):
    t = ATT_BLOCK
    nq = seq // t
    pairs = [(i, j) for i in range(nq) for j in range(i + 1)]
    qi_tab = jnp.asarray([p[0] for p in pairs], jnp.int32)
    kj_tab = jnp.asarray([p[1] for p in pairs], jnp.int32)
    q_spec = pl.BlockSpec((t, HEAD), lambda b, h, p, qi, kj: (b * nq + qi[p], h))
    kv_spec = pl.BlockSpec((t, HEAD), lambda b, h, p, qi, kj: (b * nq + kj[p], h))
    lam_spec = pl.BlockSpec((1, A_DK), lambda b, h, p, qi, kj: (0, 0))
    sub_spec = pl.BlockSpec((1, HEAD), lambda b, h, p, qi, kj: (0, 0))
    return pl.pallas_call(
        functools.partial(_attn_prompt_kernel, lam_init),
        grid_spec=pltpu.PrefetchScalarGridSpec(
            num_scalar_prefetch=2,
            grid=(batch, N_HEADS, len(pairs)),
            in_specs=[q_spec, kv_spec, kv_spec, lam_spec, lam_spec, lam_spec, lam_spec, sub_spec],
            out_specs=q_spec,
            scratch_shapes=[pltpu.VMEM((2, t, 1), F32), pltpu.VMEM((2, t, 1), F32),
                            pltpu.VMEM((2, t, HEAD), F32)]),
        out_shape=jax.ShapeDtypeStruct((batch * seq, GROUP), BF16),
        compiler_params=pltpu.CompilerParams(
            dimension_semantics=("parallel", "parallel", "arbitrary"), vmem_limit_bytes=VMEM_LIMIT),
        name="attn_prompt",
    )(qi_tab, kj_tab, aq_bf, ak_bf, av_bf, lq1, lk1, lq2, lk2, subln)


def _attn_sample_kernel(lam_init, n_pages, pt_ref,
                        q_ref, kn_ref, vn_ref, lq1_ref, lk1_ref, lq2_ref, lk2_ref, sub_ref,
                        ck_hbm, cv_hbm, o_ref, kbuf, vbuf, sem):
    b = pl.program_id(0)
    nb = pl.num_programs(0)

    page_rows = ck_hbm.shape[1]
    page_size = ck_hbm.shape[2]

    def page_copies(bb, slot):
        out = []
        for j in range(n_pages):
            page = pt_ref[bb * n_pages + j]
            out.append(pltpu.make_async_copy(ck_hbm.at[page], kbuf.at[slot, j], sem.at[slot, 0]))
            out.append(pltpu.make_async_copy(cv_hbm.at[page], vbuf.at[slot, pl.ds(j * page_rows, page_rows)],
                                             sem.at[slot, 1]))
        return out

    @pl.when(b == 0)
    def _():
        for cp in page_copies(0, 0):
            cp.start()

    slot = b % 2

    @pl.when(b + 1 < nb)
    def _():
        for cp in page_copies(b + 1, 1 - slot):
            cp.start()

    for cp in page_copies(b, slot):
        cp.wait()

    past = n_pages * page_size
    q = q_ref[0]
    rows = 2 * N_HEADS
    ri = lax.broadcasted_iota(jnp.int32, (rows, GROUP), 0)
    li = lax.broadcasted_iota(jnp.int32, (rows, GROUP), 1)
    mine = (li // A_DK) == (ri % N_HEADS) * 2 + ri // N_HEADS
    qbd = jnp.where(mine, jnp.broadcast_to(q.astype(F32), (rows, GROUP)), 0.0)
    qbd_bf = qbd.astype(BF16)
    s = jnp.concatenate(
        [jnp.dot(qbd_bf, kbuf[slot, j].astype(BF16), preferred_element_type=F32) for j in range(n_pages)],
        axis=-1)
    s_new = jnp.sum(qbd * kn_ref[0], axis=-1, keepdims=True)
    m = jnp.maximum(jnp.max(s, axis=-1, keepdims=True), s_new)
    p = jnp.exp(s - m)
    p_new = jnp.exp(s_new - m)
    inv_l = 1.0 / (jnp.sum(p, axis=-1, keepdims=True) + p_new)
    lam = _lambda(lq1_ref, lk1_ref, lq2_ref, lk2_ref, lam_init)
    pn = p * inv_l
    pn_new = p_new * inv_l
    attn = pn - lam * pltpu.roll(pn, N_HEADS, 0)
    attn_new = pn_new - lam * pltpu.roll(pn_new, N_HEADS, 0)
    attn_bf = attn.astype(BF16)
    o = jnp.concatenate(
        [jnp.dot(attn_bf, vbuf[slot, pl.ds(h, past, stride=N_HEADS), :].astype(BF16), preferred_element_type=F32)
         for h in range(N_HEADS)], axis=-1) + attn_new * vn_ref[0]
    keep = jnp.logical_and(ri < N_HEADS, li // HEAD == ri)
    o = jnp.where(keep, o, 0.0)
    ms = jnp.sum(o * o, axis=-1, keepdims=True) * (1.0 / HEAD)
    o = o * lax.rsqrt(ms + EPS) * sub_ref[...] * (1.0 - lam_init)
    o_ref[0] = jnp.sum(o, axis=0, keepdims=True)


def _attn_sample(aq_bf, ak, av, cache_k, cache_v, page_table, lq1, lk1, lq2, lk2, subln4, lam_init):
    n, n_pages = page_table.shape
    n_phys, page_size = cache_k.shape[0], cache_k.shape[1]
    ck = jnp.transpose(cache_k, (0, 2, 3, 4, 1)).reshape(n_phys, GROUP, page_size)
    cv = cache_v.reshape(n_phys, page_size * N_HEADS, HEAD)
    row_spec = pl.BlockSpec((1, 1, GROUP), lambda b, pt: (b, 0, 0))
    lam_spec = pl.BlockSpec((1, A_DK), lambda b, pt: (0, 0))
    out = pl.pallas_call(
        functools.partial(_attn_sample_kernel, lam_init, n_pages),
        grid_spec=pltpu.PrefetchScalarGridSpec(
            num_scalar_prefetch=1,
            grid=(n,),
            in_specs=[row_spec, row_spec, row_spec, lam_spec, lam_spec, lam_spec, lam_spec,
                      pl.BlockSpec((1, GROUP), lambda b, pt: (0, 0)),
                      pl.BlockSpec(memory_space=pl.ANY), pl.BlockSpec(memory_space=pl.ANY)],
            out_specs=row_spec,
            scratch_shapes=[pltpu.VMEM((2, n_pages, GROUP, page_size), F32),
                            pltpu.VMEM((2, n_pages * page_size * N_HEADS, HEAD), F32),
                            pltpu.SemaphoreType.DMA((2, 2))]),
        out_shape=jax.ShapeDtypeStruct((n, 1, GROUP), F32),
        compiler_params=pltpu.CompilerParams(
            dimension_semantics=("arbitrary",), vmem_limit_bytes=VMEM_LIMIT),
        name="attn_sample",
    )(page_table.reshape(-1), aq_bf.reshape(n, 1, GROUP), ak.reshape(n, 1, GROUP), av.reshape(n, 1, GROUP),
      lq1, lk1, lq2, lk2, subln4, ck, cv)
    return out.reshape(n, GROUP)


def _tail_kernel(x_ref, or_ref, gate_ref, oa_ref, gn_ref, wo_ref, nf_ref, wg_ref, wu_ref, wd_ref, nfin_ref,
                 y_ref):
    parts = []
    for h in range(N_HEADS):
        cs = slice(h * HEAD, (h + 1) * HEAD)
        parts.append((_rms(or_ref[:, cs], gn_ref[...]) * gate_ref[:, cs]).astype(BF16))
    parts.append(oa_ref[...].astype(BF16))
    mix = jnp.concatenate(parts, axis=-1)
    x1 = x_ref[...] + jnp.dot(mix, wo_ref[...], preferred_element_type=F32)
    h2 = _rms(x1, nf_ref[...]).astype(BF16)
    d_ff = wg_ref.shape[1]
    acc = x1
    for j in range(d_ff // FF_CHUNK):
        fs = slice(j * FF_CHUNK, (j + 1) * FF_CHUNK)
        gt = jnp.dot(h2, wg_ref[:, fs], preferred_element_type=F32)
        up = jnp.dot(h2, wu_ref[:, fs], preferred_element_type=F32)
        act = (gt * _sigmoid(gt) * up).astype(BF16)
        acc = acc + jnp.dot(act, wd_ref[fs, :], preferred_element_type=F32)
    y_ref[...] = _rms(acc, nfin_ref[...])


def _tail(x, o_r, gate, o_a, r_gnorm, wo_bf, norm_ffn, wg_bf, wu_bf, wd_bf, norm_final, tm):
    rows, d = x.shape
    d_ff = wg_bf.shape[1]

    def resident(shape):
        return pl.BlockSpec(shape, lambda i: (0, 0), pipeline_mode=pl.Buffered(1))

    return pl.pallas_call(
        _tail_kernel,
        grid=(rows // tm,),
        in_specs=[pl.BlockSpec((tm, d), lambda i: (i, 0)),
                  pl.BlockSpec((tm, GROUP), lambda i: (i, 0)),
                  pl.BlockSpec((tm, GROUP), lambda i: (i, 0)),
                  pl.BlockSpec((tm, GROUP), lambda i: (i, 0)),
                  _const_spec((1, HEAD)),
                  resident((d, d)),
                  _const_spec((1, d)),
                  resident((d, d_ff)), resident((d, d_ff)), resident((d_ff, d)),
                  _const_spec((1, d))],
        out_specs=pl.BlockSpec((tm, d), lambda i: (i, 0)),
        out_shape=jax.ShapeDtypeStruct((rows, d), F32),
        compiler_params=pltpu.CompilerParams(
            dimension_semantics=("parallel",), vmem_limit_bytes=VMEM_LIMIT),
        name="tail",
    )(x, o_r, gate, o_a, r_gnorm, wo_bf, norm_ffn, wg_bf, wu_bf, wd_bf, norm_final)


def kernel(x_prompt, x_sample, cache_k, cache_v, state_hgrn, page_table, w_in, w_out, lb_param,
           r_gnorm, lam_q1, lam_k1, lam_q2, lam_k2, a_subln, norm_mix, norm_ffn, w_gate, w_up,
           w_down, norm_final):
    batch, seq, d = x_prompt.shape
    n_dec = x_sample.shape[0]
    depth = w_in.shape[0]
    assert depth == 1 and x_sample.shape[1] == 1
    hp = x_prompt.reshape(batch * seq, d)
    hs = x_sample.reshape(n_dec, d)
    lb_param = lb_param.astype(F32)
    nfin = norm_final.reshape(1, d)
    kp, vp, sp, kss, vss, sss = [], [], [], [], [], []
    for l in range(depth):
        lam_init = 0.8 - 0.6 * math.exp(-0.3 * l)
        w_in_bf = w_in[l].astype(BF16)
        wo_bf = w_out[l].astype(BF16)
        wg_bf = w_gate[l].astype(BF16)
        wu_bf = w_up[l].astype(BF16)
        wd_bf = w_down[l].astype(BF16)
        nmix = norm_mix[l].reshape(1, d)
        nffn = norm_ffn[l].reshape(1, d)
        gn = r_gnorm[l].reshape(1, HEAD)
        sub = a_subln[l].reshape(1, HEAD)
        lams = [a[l].reshape(1, A_DK).astype(F32) for a in (lam_q1, lam_k1, lam_q2, lam_k2)]

        qr, logf, kr, vr, gate, aq_bf, ak, av, ak_bf, av_bf = _inproj(hp, nmix, w_in_bf, lb_param, l, ROW_BLOCK)
        o_r, s_new = _hgrn_prompt(qr, logf, kr, vr, batch, seq)
        o_a = _attn_prompt(aq_bf, ak_bf, av_bf, *lams, sub, lam_init, batch, seq)
        hp = _tail(hp, o_r, gate, o_a, gn, wo_bf, nffn, wg_bf, wu_bf, wd_bf, nfin, ROW_BLOCK)
        kp.append(ak.reshape(batch, seq, N_HEADS, 2, A_DK))
        vp.append(av.reshape(batch, seq, N_HEADS, HEAD))
        sp.append(s_new)

        qr, logf, kr, vr, gate, aq_bf, ak, av, _, _ = _inproj(hs, nmix, w_in_bf, lb_param, l, n_dec)
        o_r, s_new = _hgrn_sample(qr, logf, kr, vr, state_hgrn[l].astype(F32))
        o_a = _attn_sample(aq_bf, ak, av, cache_k[l], cache_v[l], page_table, *lams,
                           jnp.tile(sub, (1, N_HEADS)), lam_init)
        hs = _tail(hs, o_r, gate, o_a, gn, wo_bf, nffn, wg_bf, wu_bf, wd_bf, nfin, n_dec)
        kss.append(ak.reshape(n_dec, 1, N_HEADS, 2, A_DK))
        vss.append(av.reshape(n_dec, 1, N_HEADS, HEAD))
        sss.append(s_new)

    y_prompt = hp.reshape(batch, seq, d)
    y_sample = hs.reshape(n_dec, 1, d)
    return (y_prompt, y_sample, jnp.stack(kp), jnp.stack(vp), jnp.stack(sp),
            jnp.stack(kss), jnp.stack(vss), jnp.stack(sss))
```

```python
import functools
import math

import jax
import jax.numpy as jnp
from jax import lax
from jax.experimental import pallas as pl
from jax.experimental.pallas import tpu as pltpu

F32 = jnp.float32
BF16 = jnp.bfloat16

EPS = 1e-6
LOG2E = math.log2(math.e)
HEAD = 128
N_HEADS = 4
A_DK = 64
GROUP = N_HEADS * HEAD
N_PIECES = 7
HGRN_CHUNK = 64
HGRN_BLOCK = 256
ATT_BLOCK = 1024
ATT_ROWS = 128
ROW_BLOCK = 256
FF_CHUNK = 256
VMEM_LIMIT = 48 * 1024 * 1024


def _sigmoid(x):
    return 1.0 / (1.0 + jnp.exp(-x))


def _rms(x, w):
    return x * lax.rsqrt(jnp.mean(x * x, axis=-1, keepdims=True) + EPS) * w


def _const_spec(shape):
    return pl.BlockSpec(shape, lambda *_: (0,) * len(shape))


def _inproj_kernel(layer, x_ref, nw_ref, w_ref, lb_ref,
                   qr_ref, logf_ref, kr_ref, vr_ref, gate_ref,
                   aq_ref, ak_ref, av_ref, akb_ref, avb_ref):
    h = _rms(x_ref[...], nw_ref[...]).astype(BF16)

    def piece(j):
        return jnp.dot(h, w_ref[:, j * GROUP:(j + 1) * GROUP], preferred_element_type=F32)

    lbp = lb_ref[...]
    e = jnp.exp(lbp - jnp.max(lbp, axis=0, keepdims=True))
    lb = jnp.sum(e[:layer + 1], axis=0, keepdims=True) / jnp.sum(e, axis=0, keepdims=True)

    rq = piece(0)
    qr_ref[...] = rq * _sigmoid(rq)
    f = lb + (1.0 - lb) * _sigmoid(piece(1))
    logf_ref[...] = jnp.log(f)
    kr_ref[...] = 1.0 - f
    vr_ref[...] = piece(2)
    rg = piece(3)
    gate_ref[...] = rg * _sigmoid(rg)
    aq_ref[...] = (piece(4) * (A_DK ** -0.5 * LOG2E)).astype(BF16)
    ak = piece(5)
    ak_ref[...] = ak
    akb_ref[...] = ak.astype(BF16)
    av = piece(6)
    av_ref[...] = av
    avb_ref[...] = av.astype(BF16)


def _inproj(x, norm_w, w_bf, lb_param, layer, tm):
    rows, d = x.shape
    d_in = w_bf.shape[1]
    f32_out = jax.ShapeDtypeStruct((rows, GROUP), F32)
    bf_out = jax.ShapeDtypeStruct((rows, GROUP), BF16)
    row_spec = pl.BlockSpec((tm, GROUP), lambda i: (i, 0))
    return pl.pallas_call(
        functools.partial(_inproj_kernel, layer),
        grid=(rows // tm,),
        in_specs=[pl.BlockSpec((tm, d), lambda i: (i, 0)),
                  _const_spec((1, d)),
                  _const_spec((d, d_in)),
                  _const_spec(lb_param.shape)],
        out_specs=[row_spec] * 10,
        out_shape=[f32_out] * 5 + [bf_out, f32_out, f32_out, bf_out, bf_out],
        compiler_params=pltpu.CompilerParams(
            dimension_semantics=("parallel",), vmem_limit_bytes=VMEM_LIMIT),
        name="inproj",
    )(x, norm_w, w_bf, lb_param)


def _hgrn_prompt_kernel(q_ref, g_ref, k_ref, v_ref, o_ref, s_ref, st_scr):
    i = pl.program_id(1)

    @pl.when(i == 0)
    def _():
        st_scr[...] = jnp.zeros_like(st_scr)

    c = HGRN_CHUNK
    row = lax.broadcasted_iota(jnp.int32, (c, c), 0)
    col = lax.broadcasted_iota(jnp.int32, (c, c), 1)
    tri = col <= row
    tri_bf = jnp.where(tri, 1.0, 0.0).astype(BF16)
    nt = (((1,), (1,)), ((), ()))
    tn = (((0,), (0,)), ((), ()))

    for ci in range(HGRN_BLOCK // c):
        rs = slice(ci * c, (ci + 1) * c)
        for h in range(N_HEADS):
            cs = slice(h * HEAD, (h + 1) * HEAD)
            g = g_ref[rs, cs]
            g1 = g.astype(BF16)
            r1 = g - g1.astype(F32)
            g2 = r1.astype(BF16)
            g3 = (r1 - g2.astype(F32)).astype(BF16)
            b = (jnp.dot(tri_bf, g1, preferred_element_type=F32)
                 + jnp.dot(tri_bf, g2, preferred_element_type=F32)
                 + jnp.dot(tri_bf, g3, preferred_element_type=F32))
            b_last = b[c - 1:c, :]
            qt = (q_ref[rs, cs] * jnp.exp(b)).astype(BF16)
            kt = k_ref[rs, cs] * jnp.exp(-b)
            k2 = (kt * jnp.exp(b_last)).astype(BF16)
            v = v_ref[rs, cs].astype(BF16)
            a = lax.dot_general(qt, kt.astype(BF16), nt, preferred_element_type=F32)
            a = jnp.where(tri, a, 0.0).astype(BF16)
            st = st_scr[h]
            o = (lax.dot_general(qt, st.astype(BF16), nt, preferred_element_type=F32)
                 + jnp.dot(a, v, preferred_element_type=F32))
            o_ref[rs, cs] = o
            st_scr[h] = st * jnp.exp(b_last) + lax.dot_general(v, k2, tn, preferred_element_type=F32)

    @pl.when(i == pl.num_programs(1) - 1)
    def _():
        for h in range(N_HEADS):
            s_ref[0, h] = st_scr[h].T


def _hgrn_prompt(qr, logf, kr, vr, batch, seq):
    nb = seq // HGRN_BLOCK
    spec = pl.BlockSpec((HGRN_BLOCK, GROUP), lambda b, i: (b * nb + i, 0))
    return pl.pallas_call(
        _hgrn_prompt_kernel,
        grid=(batch, nb),
        in_specs=[spec] * 4,
        out_specs=[spec, pl.BlockSpec((1, N_HEADS, HEAD, HEAD), lambda b, i: (b, 0, 0, 0))],
        out_shape=[jax.ShapeDtypeStruct((batch * seq, GROUP), F32),
                   jax.ShapeDtypeStruct((batch, N_HEADS, HEAD, HEAD), F32)],
        scratch_shapes=[pltpu.VMEM((N_HEADS, HEAD, HEAD), F32)],
        compiler_params=pltpu.CompilerParams(
            dimension_semantics=("parallel", "arbitrary"), vmem_limit_bytes=VMEM_LIMIT),
        name="hgrn_prompt",
    )(qr, logf, kr, vr)


def _hgrn_sample_kernel(q_ref, g_ref, k_ref, v_ref, s_in_ref, o_ref, s_out_ref):
    nb = q_ref.shape[0]
    for h in range(N_HEADS):
        cs = slice(h * HEAD, (h + 1) * HEAD)
        qt = q_ref[:, cs].T
        ft = jnp.exp(g_ref[:, cs]).T
        kt = k_ref[:, cs].T
        for j in range(nb):
            s_new = ft[:, j:j + 1] * s_in_ref[j, h] + kt[:, j:j + 1] * v_ref[j:j + 1, cs]
            s_out_ref[j, h] = s_new
            o_ref[j:j + 1, cs] = jnp.sum(qt[:, j:j + 1] * s_new, axis=0, keepdims=True)


def _hgrn_sample(qr, logf, kr, vr, state):
    n = qr.shape[0]
    nb = 8
    row_spec = pl.BlockSpec((nb, GROUP), lambda i: (i, 0))
    st_spec = pl.BlockSpec((nb, N_HEADS, HEAD, HEAD), lambda i: (i, 0, 0, 0))
    return pl.pallas_call(
        _hgrn_sample_kernel,
        grid=(n // nb,),
        in_specs=[row_spec] * 4 + [st_spec],
        out_specs=[row_spec, st_spec],
        out_shape=[jax.ShapeDtypeStruct((n, GROUP), F32),
                   jax.ShapeDtypeStruct(state.shape, F32)],
        compiler_params=pltpu.CompilerParams(
            dimension_semantics=("parallel",), vmem_limit_bytes=VMEM_LIMIT),
        name="hgrn_sample",
    )(qr, logf, kr, vr, state)


def _lambda(lq1_ref, lk1_ref, lq2_ref, lk2_ref, lam_init):
    a = jnp.sum(lq1_ref[...] * lk1_ref[...], axis=-1, keepdims=True)
    b = jnp.sum(lq2_ref[...] * lk2_ref[...], axis=-1, keepdims=True)
    return jnp.exp(a) - jnp.exp(b) + lam_init


def _attn_prompt_kernel(lam_init, qi_tab, kj_tab,
                        q_ref, k_ref, v_ref, lq1_ref, lk1_ref, lq2_ref, lk2_ref, sub_ref,
                        o_ref, m_scr, acc_scr):
    p = pl.program_id(2)
    qi = qi_tab[p]
    kj = kj_tab[p]

    @pl.when(kj == 0)
    def _():
        m_scr[...] = jnp.full_like(m_scr, -jnp.inf)
        acc_scr[...] = jnp.zeros_like(acc_scr)

    tq, tk = q_ref.shape[0], k_ref.shape[0]
    nt = (((1,), (1,)), ((), ()))

    def sweep(masked):
        v_ext = jnp.concatenate([v_ref[...], jnp.ones((tk, HEAD), BF16)], axis=-1)
        lane = lax.broadcasted_iota(jnp.int32, (ATT_ROWS, HEAD), 1)
        below_diag = lane <= lax.broadcasted_iota(jnp.int32, (ATT_ROWS, HEAD), 0)
        zero = jnp.zeros((ATT_ROWS, HEAD), BF16)
        for r in range(tq // ATT_ROWS):
            rows = slice(r * ATT_ROWS, (r + 1) * ATT_ROWS)
            q = q_ref[rows, :]
            n_chunks = r + 1 if masked else tk // HEAD
            k = k_ref[0:n_chunks * HEAD, :]
            for mi, qm in enumerate((jnp.where(lane < A_DK, q, zero), jnp.where(lane >= A_DK, q, zero))):
                s = lax.dot_general(qm, k, nt, preferred_element_type=F32)
                chunks = [s[:, c * HEAD:(c + 1) * HEAD] for c in range(n_chunks)]
                if masked:
                    chunks[-1] = jnp.where(below_diag, chunks[-1], -jnp.inf)
                mc = functools.reduce(jnp.maximum, chunks)
                m_prev = m_scr[mi, rows]
                m_new = jnp.maximum(m_prev, jnp.max(mc, axis=-1, keepdims=True))
                alpha = jnp.exp2(m_prev - m_new)
                pr = jnp.concatenate([jnp.exp2(sc - m_new).astype(BF16) for sc in chunks], axis=-1)
                acc_scr[mi, rows] = (jnp.concatenate([alpha, alpha], axis=-1) * acc_scr[mi, rows]
                                     + jnp.dot(pr, v_ext[0:n_chunks * HEAD], preferred_element_type=F32))
                m_scr[mi, rows] = m_new

    @pl.when(kj < qi)
    def _():
        sweep(False)

    @pl.when(kj == qi)
    def _():
        sweep(True)
        lam = _lambda(lq1_ref, lk1_ref, lq2_ref, lk2_ref, lam_init)
        a0 = acc_scr[0]
        a1 = acc_scr[1]
        o = a0[:, :HEAD] / a0[:, HEAD:] - lam * (a1[:, :HEAD] / a1[:, HEAD:])
        o_ref[...] = (_rms(o, sub_ref[...]) * (1.0 - lam_init)).astype(o_ref.dtype)


def _attn_prompt(aq_bf, ak_bf, av_bf, lq1, lk1, lq2, lk2, subln, lam_init, batch, seq):
    t = ATT_BLOCK
    nq = seq // t
    pairs = [(i, j) for i in range(nq) for j in range(i + 1)]
    qi_tab = jnp.asarray([p[0] for p in pairs], jnp.int32)
    kj_tab = jnp.asarray([p[1] for p in pairs], jnp.int32)
    q_spec = pl.BlockSpec((t, HEAD), lambda b, h, p, qi, kj: (b * nq + qi[p], h))
    kv_spec = pl.BlockSpec((t, HEAD), lambda b, h, p, qi, kj: (b * nq + kj[p], h))
    lam_spec = pl.BlockSpec((1, A_DK), lambda b, h, p, qi, kj: (0, 0))
    sub_spec = pl.BlockSpec((1, HEAD), lambda b, h, p, qi, kj: (0, 0))
    return pl.pallas_call(
        functools.partial(_attn_prompt_kernel, lam_init),
        grid_spec=pltpu.PrefetchScalarGridSpec(
            num_scalar_prefetch=2,
            grid=(batch, N_HEADS, len(pairs)),
            in_specs=[q_spec, kv_spec, kv_spec, lam_spec, lam_spec, lam_spec, lam_spec, sub_spec],
            out_specs=q_spec,
            scratch_shapes=[pltpu.VMEM((2, t, HEAD), F32), pltpu.VMEM((2, t, 2 * HEAD), F32)]),
        out_shape=jax.ShapeDtypeStruct((batch * seq, GROUP), BF16),
        compiler_params=pltpu.CompilerParams(
            dimension_semantics=("parallel", "parallel", "arbitrary"), vmem_limit_bytes=VMEM_LIMIT),
        name="attn_prompt",
    )(qi_tab, kj_tab, aq_bf, ak_bf, av_bf, lq1, lk1, lq2, lk2, subln)


def _attn_sample_kernel(lam_init, n_pages, pt_ref,
                        q_ref, kn_ref, vn_ref, lq1_ref, lk1_ref, lq2_ref, lk2_ref, sub_ref,
                        ck_hbm, cv_hbm, o_ref, kbuf, vbuf, sem):
    b = pl.program_id(0)
    nb = pl.num_programs(0)

    page_rows = ck_hbm.shape[1]
    page_size = ck_hbm.shape[2]

    def page_copies(bb, slot):
        out = []
        for j in range(n_pages):
            page = pt_ref[bb * n_pages + j]
            out.append(pltpu.make_async_copy(ck_hbm.at[page], kbuf.at[slot, j], sem.at[slot, 0]))
            out.append(pltpu.make_async_copy(cv_hbm.at[page], vbuf.at[slot, pl.ds(j * page_rows, page_rows)],
                                             sem.at[slot, 1]))
        return out

    @pl.when(b == 0)
    def _():
        for cp in page_copies(0, 0):
            cp.start()

    slot = b % 2

    @pl.when(b + 1 < nb)
    def _():
        for cp in page_copies(b + 1, 1 - slot):
            cp.start()

    for cp in page_copies(b, slot):
        cp.wait()

    past = n_pages * page_size
    q = q_ref[0]
    rows = 2 * N_HEADS
    ri = lax.broadcasted_iota(jnp.int32, (rows, GROUP), 0)
    li = lax.broadcasted_iota(jnp.int32, (rows, GROUP), 1)
    mine = (li // A_DK) == (ri % N_HEADS) * 2 + ri // N_HEADS
    qbd = jnp.where(mine, jnp.broadcast_to(q.astype(F32), (rows, GROUP)), 0.0)
    qbd_bf = qbd.astype(BF16)
    s = jnp.concatenate(
        [jnp.dot(qbd_bf, kbuf[slot, j].astype(BF16), preferred_element_type=F32) for j in range(n_pages)],
        axis=-1)
    s_new = jnp.sum(qbd * kn_ref[0], axis=-1, keepdims=True)
    m = jnp.maximum(jnp.max(s, axis=-1, keepdims=True), s_new)
    p = jnp.exp2(s - m)
    p_new = jnp.exp2(s_new - m)
    inv_l = 1.0 / (jnp.sum(p, axis=-1, keepdims=True) + p_new)
    lam = _lambda(lq1_ref, lk1_ref, lq2_ref, lk2_ref, lam_init)
    pn = p * inv_l
    pn_new = p_new * inv_l
    attn = pn - lam * pltpu.roll(pn, N_HEADS, 0)
    attn_new = pn_new - lam * pltpu.roll(pn_new, N_HEADS, 0)
    attn_bf = attn.astype(BF16)
    o = jnp.concatenate(
        [jnp.dot(attn_bf, vbuf[slot, pl.ds(h, past, stride=N_HEADS), :].astype(BF16), preferred_element_type=F32)
         for h in range(N_HEADS)], axis=-1) + attn_new * vn_ref[0]
    keep = jnp.logical_and(ri < N_HEADS, li // HEAD == ri)
    o = jnp.where(keep, o, 0.0)
    ms = jnp.sum(o * o, axis=-1, keepdims=True) * (1.0 / HEAD)
    o = o * lax.rsqrt(ms + EPS) * sub_ref[...] * (1.0 - lam_init)
    o_ref[0] = jnp.sum(o, axis=0, keepdims=True)


def _attn_sample(aq_bf, ak, av, cache_k, cache_v, page_table, lq1, lk1, lq2, lk2, subln4, lam_init):
    n, n_pages = page_table.shape
    n_phys, page_size = cache_k.shape[0], cache_k.shape[1]
    ck = jnp.transpose(cache_k, (0, 2, 3, 4, 1)).reshape(n_phys, GROUP, page_size)
    cv = cache_v.reshape(n_phys, page_size * N_HEADS, HEAD)
    row_spec = pl.BlockSpec((1, 1, GROUP), lambda b, pt: (b, 0, 0))
    lam_spec = pl.BlockSpec((1, A_DK), lambda b, pt: (0, 0))
    out = pl.pallas_call(
        functools.partial(_attn_sample_kernel, lam_init, n_pages),
        grid_spec=pltpu.PrefetchScalarGridSpec(
            num_scalar_prefetch=1,
            grid=(n,),
            in_specs=[row_spec, row_spec, row_spec, lam_spec, lam_spec, lam_spec, lam_spec,
                      pl.BlockSpec((1, GROUP), lambda b, pt: (0, 0)),
                      pl.BlockSpec(memory_space=pl.ANY), pl.BlockSpec(memory_space=pl.ANY)],
            out_specs=row_spec,
            scratch_shapes=[pltpu.VMEM((2, n_pages, GROUP, page_size), F32),
                            pltpu.VMEM((2, n_pages * page_size * N_HEADS, HEAD), F32),
                            pltpu.SemaphoreType.DMA((2, 2))]),
        out_shape=jax.ShapeDtypeStruct((n, 1, GROUP), F32),
        compiler_params=pltpu.CompilerParams(
            dimension_semantics=("arbitrary",), vmem_limit_bytes=VMEM_LIMIT),
        name="attn_sample",
    )(page_table.reshape(-1), aq_bf.reshape(n, 1, GROUP), ak.reshape(n, 1, GROUP), av.reshape(n, 1, GROUP),
      lq1, lk1, lq2, lk2, subln4, ck, cv)
    return out.reshape(n, GROUP)


def _tail_kernel(x_ref, or_ref, gate_ref, oa_ref, gn_ref, wo_ref, nf_ref, wg_ref, wu_ref, wd_ref, nfin_ref,
                 y_ref):
    parts = []
    for h in range(N_HEADS):
        cs = slice(h * HEAD, (h + 1) * HEAD)
        parts.append((_rms(or_ref[:, cs], gn_ref[...]) * gate_ref[:, cs]).astype(BF16))
    parts.append(oa_ref[...].astype(BF16))
    mix = jnp.concatenate(parts, axis=-1)
    x1 = x_ref[...] + jnp.dot(mix, wo_ref[...], preferred_element_type=F32)
    h2 = _rms(x1, nf_ref[...]).astype(BF16)
    d_ff = wg_ref.shape[1]
    acc = x1
    for j in range(d_ff // FF_CHUNK):
        fs = slice(j * FF_CHUNK, (j + 1) * FF_CHUNK)
        gt = jnp.dot(h2, wg_ref[:, fs], preferred_element_type=F32)
        up = jnp.dot(h2, wu_ref[:, fs], preferred_element_type=F32)
        act = (gt * _sigmoid(gt) * up).astype(BF16)
        acc = acc + jnp.dot(act, wd_ref[fs, :], preferred_element_type=F32)
    y_ref[...] = _rms(acc, nfin_ref[...])


def _tail(x, o_r, gate, o_a, r_gnorm, wo_bf, norm_ffn, wg_bf, wu_bf, wd_bf, norm_final, tm):
    rows, d = x.shape
    d_ff = wg_bf.shape[1]

    def resident(shape):
        return pl.BlockSpec(shape, lambda i: (0, 0), pipeline_mode=pl.Buffered(1))

    return pl.pallas_call(
        _tail_kernel,
        grid=(rows // tm,),
        in_specs=[pl.BlockSpec((tm, d), lambda i: (i, 0)),
                  pl.BlockSpec((tm, GROUP), lambda i: (i, 0)),
                  pl.BlockSpec((tm, GROUP), lambda i: (i, 0)),
                  pl.BlockSpec((tm, GROUP), lambda i: (i, 0)),
                  _const_spec((1, HEAD)),
                  resident((d, d)),
                  _const_spec((1, d)),
                  resident((d, d_ff)), resident((d, d_ff)), resident((d_ff, d)),
                  _const_spec((1, d))],
        out_specs=pl.BlockSpec((tm, d), lambda i: (i, 0)),
        out_shape=jax.ShapeDtypeStruct((rows, d), F32),
        compiler_params=pltpu.CompilerParams(
            dimension_semantics=("parallel",), vmem_limit_bytes=VMEM_LIMIT),
        name="tail",
    )(x, o_r, gate, o_a, r_gnorm, wo_bf, norm_ffn, wg_bf, wu_bf, wd_bf, norm_final)


def kernel(x_prompt, x_sample, cache_k, cache_v, state_hgrn, page_table, w_in, w_out, lb_param,
           r_gnorm, lam_q1, lam_k1, lam_q2, lam_k2, a_subln, norm_mix, norm_ffn, w_gate, w_up,
           w_down, norm_final):
    batch, seq, d = x_prompt.shape
    n_dec = x_sample.shape[0]
    depth = w_in.shape[0]
    assert depth == 1 and x_sample.shape[1] == 1
    hp = x_prompt.reshape(batch * seq, d)
    hs = x_sample.reshape(n_dec, d)
    lb_param = lb_param.astype(F32)
    nfin = norm_final.reshape(1, d)
    kp, vp, sp, kss, vss, sss = [], [], [], [], [], []
    for l in range(depth):
        lam_init = 0.8 - 0.6 * math.exp(-0.3 * l)
        w_in_bf = w_in[l].astype(BF16)
        wo_bf = w_out[l].astype(BF16)
        wg_bf = w_gate[l].astype(BF16)
        wu_bf = w_up[l].astype(BF16)
        wd_bf = w_down[l].astype(BF16)
        nmix = norm_mix[l].reshape(1, d)
        nffn = norm_ffn[l].reshape(1, d)
        gn = r_gnorm[l].reshape(1, HEAD)
        sub = a_subln[l].reshape(1, HEAD)
        lams = [a[l].reshape(1, A_DK).astype(F32) for a in (lam_q1, lam_k1, lam_q2, lam_k2)]

        qr, logf, kr, vr, gate, aq_bf, ak, av, ak_bf, av_bf = _inproj(hp, nmix, w_in_bf, lb_param, l, ROW_BLOCK)
        o_r, s_new = _hgrn_prompt(qr, logf, kr, vr, batch, seq)
        o_a = _attn_prompt(aq_bf, ak_bf, av_bf, *lams, sub, lam_init, batch, seq)
        hp = _tail(hp, o_r, gate, o_a, gn, wo_bf, nffn, wg_bf, wu_bf, wd_bf, nfin, ROW_BLOCK)
        kp.append(ak.reshape(batch, seq, N_HEADS, 2, A_DK))
        vp.append(av.reshape(batch, seq, N_HEADS, HEAD))
        sp.append(s_new)

        qr, logf, kr, vr, gate, aq_bf, ak, av, _, _ = _inproj(hs, nmix, w_in_bf, lb_param, l, n_dec)
        o_r, s_new = _hgrn_sample(qr, logf, kr, vr, state_hgrn[l].astype(F32))
        o_a = _attn_sample(aq_bf, ak, av, cache_k[l], cache_v[l], page_table, *lams,
                           jnp.tile(sub, (1, N_HEADS)), lam_init)
        hs = _tail(hs, o_r, gate, o_a, gn, wo_bf, nffn, wg_bf, wu_bf, wd_bf, nfin, n_dec)
        kss.append(ak.reshape(n_dec, 1, N_HEADS, 2, A_DK))
        vss.append(av.reshape(n_dec, 1, N_HEADS, HEAD))
        sss.append(s_new)

    y_prompt = hp.reshape(batch, seq, d)
    y_sample = hs.reshape(n_dec, 1, d)
    return (y_prompt, y_sample, jnp.stack(kp), jnp.stack(vp), jnp.stack(sp),
            jnp.stack(kss), jnp.stack(vss), jnp.stack(sss))
```

```python
import functools
import math

import jax
import jax.numpy as jnp
from jax import lax
from jax.experimental import pallas as pl
from jax.experimental.pallas import tpu as pltpu

F32 = jnp.float32
BF16 = jnp.bfloat16

EPS = 1e-6
LOG2E = math.log2(math.e)
HEAD = 128
N_HEADS = 4
A_DK = 64
GROUP = N_HEADS * HEAD
N_PIECES = 7
HGRN_CHUNK = 64
HGRN_BLOCK = 256
ATT_BLOCK = 1024
ATT_ROWS = 128
ROW_BLOCK = 256
TAIL_BLOCK = 512
TAIL_ROWS = 256
FF_CHUNK = 256
VMEM_LIMIT = 48 * 1024 * 1024


def _sigmoid(x):
    return 1.0 / (1.0 + jnp.exp(-x))


def _rms(x, w):
    return x * lax.rsqrt(jnp.mean(x * x, axis=-1, keepdims=True) + EPS) * w


def _const_spec(shape):
    return pl.BlockSpec(shape, lambda *_: (0,) * len(shape))


def _inproj_kernel(layer, x_ref, nw_ref, w_ref, lb_ref,
                   qr_ref, logf_ref, kr_ref, vr_ref, gate_ref,
                   aq_ref, akt_ref, av_ref, akb_ref, avb_ref):
    h = _rms(x_ref[...], nw_ref[...]).astype(BF16)

    def piece(j):
        return jnp.dot(h, w_ref[:, j * GROUP:(j + 1) * GROUP], preferred_element_type=F32)

    lbp = lb_ref[...]
    e = jnp.exp(lbp - jnp.max(lbp, axis=0, keepdims=True))
    lb = jnp.sum(e[:layer + 1], axis=0, keepdims=True) / jnp.sum(e, axis=0, keepdims=True)

    rq = piece(0)
    qr_ref[...] = rq * _sigmoid(rq)
    f = lb + (1.0 - lb) * _sigmoid(piece(1))
    logf_ref[...] = jnp.log(f)
    kr_ref[...] = 1.0 - f
    vr_ref[...] = piece(2)
    rg = piece(3)
    gate_ref[...] = rg * _sigmoid(rg)
    aq_ref[...] = (piece(4) * (A_DK ** -0.5 * LOG2E)).astype(BF16)
    ak = piece(5)
    akt_ref[0] = ak.T
    akb_ref[...] = ak.astype(BF16)
    av = piece(6)
    av_ref[...] = av
    avb_ref[...] = av.astype(BF16)


def _inproj(x, norm_w, w_bf, lb_param, layer, tm, seq):
    rows, d = x.shape
    d_in = w_bf.shape[1]
    per_seq = seq // tm
    f32_out = jax.ShapeDtypeStruct((rows, GROUP), F32)
    bf_out = jax.ShapeDtypeStruct((rows, GROUP), BF16)
    row_spec = pl.BlockSpec((tm, GROUP), lambda i: (i, 0))
    kt_out = jax.ShapeDtypeStruct((rows // seq, GROUP, seq), F32)
    kt_spec = pl.BlockSpec((1, GROUP, tm), lambda i: (i // per_seq, 0, i % per_seq))
    return pl.pallas_call(
        functools.partial(_inproj_kernel, layer),
        grid=(rows // tm,),
        in_specs=[pl.BlockSpec((tm, d), lambda i: (i, 0)),
                  _const_spec((1, d)),
                  _const_spec((d, d_in)),
                  _const_spec(lb_param.shape)],
        out_specs=[row_spec] * 6 + [kt_spec] + [row_spec] * 3,
        out_shape=[f32_out] * 5 + [bf_out, kt_out, f32_out, bf_out, bf_out],
        compiler_params=pltpu.CompilerParams(
            dimension_semantics=("parallel",), vmem_limit_bytes=VMEM_LIMIT),
        name="inproj",
    )(x, norm_w, w_bf, lb_param)


def _hgrn_prompt_kernel(q_ref, g_ref, k_ref, v_ref, o_ref, s_ref, st_scr):
    i = pl.program_id(1)

    @pl.when(i == 0)
    def _():
        st_scr[...] = jnp.zeros_like(st_scr)

    c = HGRN_CHUNK
    row = lax.broadcasted_iota(jnp.int32, (c, c), 0)
    col = lax.broadcasted_iota(jnp.int32, (c, c), 1)
    tri = col <= row
    tri_bf = jnp.where(tri, 1.0, 0.0).astype(BF16)
    nt = (((1,), (1,)), ((), ()))
    tn = (((0,), (0,)), ((), ()))

    for ci in range(HGRN_BLOCK // c):
        rs = slice(ci * c, (ci + 1) * c)
        for h in range(N_HEADS):
            cs = slice(h * HEAD, (h + 1) * HEAD)
            g = g_ref[rs, cs]
            g1 = g.astype(BF16)
            r1 = g - g1.astype(F32)
            g2 = r1.astype(BF16)
            g3 = (r1 - g2.astype(F32)).astype(BF16)
            b = (jnp.dot(tri_bf, g1, preferred_element_type=F32)
                 + jnp.dot(tri_bf, g2, preferred_element_type=F32)
                 + jnp.dot(tri_bf, g3, preferred_element_type=F32))
            b_last = b[c - 1:c, :]
            qt = (q_ref[rs, cs] * jnp.exp(b)).astype(BF16)
            kt = k_ref[rs, cs] * jnp.exp(-b)
            k2 = (kt * jnp.exp(b_last)).astype(BF16)
            v = v_ref[rs, cs].astype(BF16)
            a = lax.dot_general(qt, kt.astype(BF16), nt, preferred_element_type=F32)
            a = jnp.where(tri, a, 0.0).astype(BF16)
            st = st_scr[h]
            o = (lax.dot_general(qt, st.astype(BF16), nt, preferred_element_type=F32)
                 + jnp.dot(a, v, preferred_element_type=F32))
            o_ref[rs, cs] = o
            st_scr[h] = st * jnp.exp(b_last) + lax.dot_general(v, k2, tn, preferred_element_type=F32)

    @pl.when(i == pl.num_programs(1) - 1)
    def _():
        for h in range(N_HEADS):
            s_ref[0, h] = st_scr[h].T


def _hgrn_prompt(qr, logf, kr, vr, batch, seq):
    nb = seq // HGRN_BLOCK
    spec = pl.BlockSpec((HGRN_BLOCK, GROUP), lambda b, i: (b * nb + i, 0))
    return pl.pallas_call(
        _hgrn_prompt_kernel,
        grid=(batch, nb),
        in_specs=[spec] * 4,
        out_specs=[spec, pl.BlockSpec((1, N_HEADS, HEAD, HEAD), lambda b, i: (b, 0, 0, 0))],
        out_shape=[jax.ShapeDtypeStruct((batch * seq, GROUP), F32),
                   jax.ShapeDtypeStruct((batch, N_HEADS, HEAD, HEAD), F32)],
        scratch_shapes=[pltpu.VMEM((N_HEADS, HEAD, HEAD), F32)],
        compiler_params=pltpu.CompilerParams(
            dimension_semantics=("parallel", "arbitrary"), vmem_limit_bytes=VMEM_LIMIT),
        name="hgrn_prompt",
    )(qr, logf, kr, vr)


def _hgrn_sample_kernel(q_ref, g_ref, k_ref, v_ref, s_in_ref, o_ref, s_out_ref):
    nb = q_ref.shape[0]
    for h in range(N_HEADS):
        cs = slice(h * HEAD, (h + 1) * HEAD)
        qt = q_ref[:, cs].T
        ft = jnp.exp(g_ref[:, cs]).T
        kt = k_ref[:, cs].T
        for j in range(nb):
            s_new = ft[:, j:j + 1] * s_in_ref[j, h] + kt[:, j:j + 1] * v_ref[j:j + 1, cs]
            s_out_ref[j, h] = s_new
            o_ref[j:j + 1, cs] = jnp.sum(qt[:, j:j + 1] * s_new, axis=0, keepdims=True)


def _hgrn_sample(qr, logf, kr, vr, state):
    n = qr.shape[0]
    nb = 8
    row_spec = pl.BlockSpec((nb, GROUP), lambda i: (i, 0))
    st_spec = pl.BlockSpec((nb, N_HEADS, HEAD, HEAD), lambda i: (i, 0, 0, 0))
    return pl.pallas_call(
        _hgrn_sample_kernel,
        grid=(n // nb,),
        in_specs=[row_spec] * 4 + [st_spec],
        out_specs=[row_spec, st_spec],
        out_shape=[jax.ShapeDtypeStruct((n, GROUP), F32),
                   jax.ShapeDtypeStruct(state.shape, F32)],
        compiler_params=pltpu.CompilerParams(
            dimension_semantics=("parallel",), vmem_limit_bytes=VMEM_LIMIT),
        name="hgrn_sample",
    )(qr, logf, kr, vr, state)


def _lambda(lq1_ref, lk1_ref, lq2_ref, lk2_ref, lam_init):
    a = jnp.sum(lq1_ref[...] * lk1_ref[...], axis=-1, keepdims=True)
    b = jnp.sum(lq2_ref[...] * lk2_ref[...], axis=-1, keepdims=True)
    return jnp.exp(a) - jnp.exp(b) + lam_init


def _attn_prompt_kernel(lam_init, qi_tab, kj_tab,
                        q_ref, k_ref, v_ref, lq1_ref, lk1_ref, lq2_ref, lk2_ref, sub_ref,
                        o_ref, m_scr, acc_scr):
    p = pl.program_id(2)
    qi = qi_tab[p]
    kj = kj_tab[p]

    @pl.when(kj == 0)
    def _():
        m_scr[...] = jnp.full_like(m_scr, -jnp.inf)
        acc_scr[...] = jnp.zeros_like(acc_scr)

    tq, tk = q_ref.shape[0], k_ref.shape[0]
    nt = (((1,), (1,)), ((), ()))

    def sweep(masked):
        v_ext = jnp.concatenate([v_ref[...], jnp.ones((tk, HEAD), BF16)], axis=-1)
        lane = lax.broadcasted_iota(jnp.int32, (ATT_ROWS, HEAD), 1)
        below_diag = lane <= lax.broadcasted_iota(jnp.int32, (ATT_ROWS, HEAD), 0)
        zero = jnp.zeros((ATT_ROWS, HEAD), BF16)
        for r in range(tq // ATT_ROWS):
            rows = slice(r * ATT_ROWS, (r + 1) * ATT_ROWS)
            q = q_ref[rows, :]
            n_chunks = r + 1 if masked else tk // HEAD
            k = k_ref[0:n_chunks * HEAD, :]
            for mi, qm in enumerate((jnp.where(lane < A_DK, q, zero), jnp.where(lane >= A_DK, q, zero))):
                s = lax.dot_general(qm, k, nt, preferred_element_type=F32)
                chunks = [s[:, c * HEAD:(c + 1) * HEAD] for c in range(n_chunks)]
                if masked:
                    chunks[-1] = jnp.where(below_diag, chunks[-1], -jnp.inf)
                mc = functools.reduce(jnp.maximum, chunks)
                m_prev = m_scr[mi, rows]
                m_new = jnp.maximum(m_prev, jnp.max(mc, axis=-1, keepdims=True))
                alpha = jnp.exp2(m_prev - m_new)
                pr = jnp.concatenate([jnp.exp2(sc - m_new).astype(BF16) for sc in chunks], axis=-1)
                acc_scr[mi, rows] = (jnp.concatenate([alpha, alpha], axis=-1) * acc_scr[mi, rows]
                                     + jnp.dot(pr, v_ext[0:n_chunks * HEAD], preferred_element_type=F32))
                m_scr[mi, rows] = m_new

    @pl.when(kj < qi)
    def _():
        sweep(False)

    @pl.when(kj == qi)
    def _():
        sweep(True)
        lam = _lambda(lq1_ref, lk1_ref, lq2_ref, lk2_ref, lam_init)
        a0 = acc_scr[0]
        a1 = acc_scr[1]
        o = a0[:, :HEAD] / a0[:, HEAD:] - lam * (a1[:, :HEAD] / a1[:, HEAD:])
        o_ref[...] = (_rms(o, sub_ref[...]) * (1.0 - lam_init)).astype(o_ref.dtype)


def _attn_prompt(aq_bf, ak_bf, av_bf, lq1, lk1, lq2, lk2, subln, lam_init, batch, seq):
    t = ATT_BLOCK
    nq = seq // t
    pairs = [(i, j) for i in range(nq) for j in range(i + 1)]
    qi_tab = jnp.asarray([p[0] for p in pairs], jnp.int32)
    kj_tab = jnp.asarray([p[1] for p in pairs], jnp.int32)
    q_spec = pl.BlockSpec((t, HEAD), lambda b, h, p, qi, kj: (b * nq + qi[p], h))
    kv_spec = pl.BlockSpec((t, HEAD), lambda b, h, p, qi, kj: (b * nq + kj[p], h))
    lam_spec = pl.BlockSpec((1, A_DK), lambda b, h, p, qi, kj: (0, 0))
    sub_spec = pl.BlockSpec((1, HEAD), lambda b, h, p, qi, kj: (0, 0))
    return pl.pallas_call(
        functools.partial(_attn_prompt_kernel, lam_init),
        grid_spec=pltpu.PrefetchScalarGridSpec(
            num_scalar_prefetch=2,
            grid=(batch, N_HEADS, len(pairs)),
            in_specs=[q_spec, kv_spec, kv_spec, lam_spec, lam_spec, lam_spec, lam_spec, sub_spec],
            out_specs=q_spec,
            scratch_shapes=[pltpu.VMEM((2, t, HEAD), F32), pltpu.VMEM((2, t, 2 * HEAD), F32)]),
        out_shape=jax.ShapeDtypeStruct((batch * seq, GROUP), BF16),
        compiler_params=pltpu.CompilerParams(
            dimension_semantics=("parallel", "parallel", "arbitrary"), vmem_limit_bytes=VMEM_LIMIT),
        name="attn_prompt",
    )(qi_tab, kj_tab, aq_bf, ak_bf, av_bf, lq1, lk1, lq2, lk2, subln)


def _attn_sample_kernel(lam_init, n_pages, pt_ref,
                        q_ref, kn_ref, vn_ref, lq1_ref, lk1_ref, lq2_ref, lk2_ref, sub_ref,
                        ck_hbm, cv_hbm, o_ref, kbuf, vbuf, sem):
    b = pl.program_id(0)
    nb = pl.num_programs(0)

    page_rows = ck_hbm.shape[1]
    page_size = ck_hbm.shape[2]

    def page_copies(bb, slot):
        out = []
        for j in range(n_pages):
            page = pt_ref[bb * n_pages + j]
            out.append(pltpu.make_async_copy(ck_hbm.at[page], kbuf.at[slot, j], sem.at[slot, 0]))
            out.append(pltpu.make_async_copy(cv_hbm.at[page], vbuf.at[slot, pl.ds(j * page_rows, page_rows)],
                                             sem.at[slot, 1]))
        return out

    @pl.when(b == 0)
    def _():
        for cp in page_copies(0, 0):
            cp.start()

    slot = b % 2

    @pl.when(b + 1 < nb)
    def _():
        for cp in page_copies(b + 1, 1 - slot):
            cp.start()

    for cp in page_copies(b, slot):
        cp.wait()

    past = n_pages * page_size
    q = q_ref[0]
    rows = 2 * N_HEADS
    ri = lax.broadcasted_iota(jnp.int32, (rows, GROUP), 0)
    li = lax.broadcasted_iota(jnp.int32, (rows, GROUP), 1)
    mine = (li // A_DK) == (ri % N_HEADS) * 2 + ri // N_HEADS
    qbd = jnp.where(mine, jnp.broadcast_to(q.astype(F32), (rows, GROUP)), 0.0)
    qbd_bf = qbd.astype(BF16)
    s = jnp.concatenate(
        [jnp.dot(qbd_bf, kbuf[slot, j].astype(BF16), preferred_element_type=F32) for j in range(n_pages)],
        axis=-1)
    s_new = jnp.sum(qbd * kn_ref[0].astype(F32), axis=-1, keepdims=True)
    m = jnp.maximum(jnp.max(s, axis=-1, keepdims=True), s_new)
    p = jnp.exp2(s - m)
    p_new = jnp.exp2(s_new - m)
    inv_l = 1.0 / (jnp.sum(p, axis=-1, keepdims=True) + p_new)
    lam = _lambda(lq1_ref, lk1_ref, lq2_ref, lk2_ref, lam_init)
    pn = p * inv_l
    pn_new = p_new * inv_l
    attn = pn - lam * pltpu.roll(pn, N_HEADS, 0)
    attn_new = pn_new - lam * pltpu.roll(pn_new, N_HEADS, 0)
    attn_bf = attn.astype(BF16)
    o = jnp.concatenate(
        [jnp.dot(attn_bf, vbuf[slot, pl.ds(h, past, stride=N_HEADS), :].astype(BF16), preferred_element_type=F32)
         for h in range(N_HEADS)], axis=-1) + attn_new * vn_ref[0]
    keep = jnp.logical_and(ri < N_HEADS, li // HEAD == ri)
    o = jnp.where(keep, o, 0.0)
    ms = jnp.sum(o * o, axis=-1, keepdims=True) * (1.0 / HEAD)
    o = o * lax.rsqrt(ms + EPS) * sub_ref[...] * (1.0 - lam_init)
    o_ref[0] = jnp.sum(o, axis=0, keepdims=True)


def _attn_sample(aq_bf, ak, av, cache_k, cache_v, page_table, lq1, lk1, lq2, lk2, subln4, lam_init):
    n, n_pages = page_table.shape
    n_phys, page_size = cache_k.shape[0], cache_k.shape[1]
    ck = jnp.transpose(cache_k, (0, 2, 3, 4, 1)).reshape(n_phys, GROUP, page_size)
    cv = cache_v.reshape(n_phys, page_size * N_HEADS, HEAD)
    row_spec = pl.BlockSpec((1, 1, GROUP), lambda b, pt: (b, 0, 0))
    lam_spec = pl.BlockSpec((1, A_DK), lambda b, pt: (0, 0))
    out = pl.pallas_call(
        functools.partial(_attn_sample_kernel, lam_init, n_pages),
        grid_spec=pltpu.PrefetchScalarGridSpec(
            num_scalar_prefetch=1,
            grid=(n,),
            in_specs=[row_spec, row_spec, row_spec, lam_spec, lam_spec, lam_spec, lam_spec,
                      pl.BlockSpec((1, GROUP), lambda b, pt: (0, 0)),
                      pl.BlockSpec(memory_space=pl.ANY), pl.BlockSpec(memory_space=pl.ANY)],
            out_specs=row_spec,
            scratch_shapes=[pltpu.VMEM((2, n_pages, GROUP, page_size), F32),
                            pltpu.VMEM((2, n_pages * page_size * N_HEADS, HEAD), F32),
                            pltpu.SemaphoreType.DMA((2, 2))]),
        out_shape=jax.ShapeDtypeStruct((n, 1, GROUP), F32),
        compiler_params=pltpu.CompilerParams(
            dimension_semantics=("arbitrary",), vmem_limit_bytes=VMEM_LIMIT),
        name="attn_sample",
    )(page_table.reshape(-1), aq_bf.reshape(n, 1, GROUP), ak.reshape(n, 1, GROUP), av.reshape(n, 1, GROUP),
      lq1, lk1, lq2, lk2, subln4, ck, cv)
    return out.reshape(n, GROUP)


def _tail_kernel(x_ref, or_ref, gate_ref, oa_ref, gn_ref, wo_ref, nf_ref, wg_ref, wu_ref, wd_ref, nfin_ref,
                 y_ref, act_scr):
    tm = x_ref.shape[0]
    d_ff = wg_ref.shape[1]
    sub = min(tm, TAIL_ROWS)
    for g in range(tm // sub):
        rows = slice(g * sub, (g + 1) * sub)
        parts = []
        for h in range(N_HEADS):
            cs = slice(h * HEAD, (h + 1) * HEAD)
            parts.append((_rms(or_ref[rows, cs], gn_ref[...]) * gate_ref[rows, cs]).astype(BF16))
        parts.append(oa_ref[rows, :].astype(BF16))
        mix = jnp.concatenate(parts, axis=-1)
        x1 = x_ref[rows, :] + jnp.dot(mix, wo_ref[...], preferred_element_type=F32)
        h2 = _rms(x1, nf_ref[...]).astype(BF16)
        for j in range(d_ff // FF_CHUNK):
            fs = slice(j * FF_CHUNK, (j + 1) * FF_CHUNK)
            gt = jnp.dot(h2, wg_ref[:, fs], preferred_element_type=F32)
            up = jnp.dot(h2, wu_ref[:, fs], preferred_element_type=F32)
            act_scr[rows, fs] = (gt * _sigmoid(gt) * up).astype(BF16)
        y = x1 + jnp.dot(act_scr[rows, :], wd_ref[...], preferred_element_type=F32)
        y_ref[rows, :] = _rms(y, nfin_ref[...])


def _tail(x, o_r, gate, o_a, r_gnorm, wo_bf, norm_ffn, wg_bf, wu_bf, wd_bf, norm_final, tm):
    rows, d = x.shape
    d_ff = wg_bf.shape[1]

    def resident(shape):
        return pl.BlockSpec(shape, lambda i: (0, 0), pipeline_mode=pl.Buffered(1))

    return pl.pallas_call(
        _tail_kernel,
        grid=(rows // tm,),
        in_specs=[pl.BlockSpec((tm, d), lambda i: (i, 0)),
                  pl.BlockSpec((tm, GROUP), lambda i: (i, 0)),
                  pl.BlockSpec((tm, GROUP), lambda i: (i, 0)),
                  pl.BlockSpec((tm, GROUP), lambda i: (i, 0)),
                  _const_spec((1, HEAD)),
                  resident((d, d)),
                  _const_spec((1, d)),
                  resident((d, d_ff)), resident((d, d_ff)), resident((d_ff, d)),
                  _const_spec((1, d))],
        out_specs=pl.BlockSpec((tm, d), lambda i: (i, 0)),
        out_shape=jax.ShapeDtypeStruct((rows, d), F32),
        scratch_shapes=[pltpu.VMEM((tm, d_ff), BF16)],
        compiler_params=pltpu.CompilerParams(
            dimension_semantics=("parallel",), vmem_limit_bytes=VMEM_LIMIT),
        name="tail",
    )(x, o_r, gate, o_a, r_gnorm, wo_bf, norm_ffn, wg_bf, wu_bf, wd_bf, norm_final)


def kernel(x_prompt, x_sample, cache_k, cache_v, state_hgrn, page_table, w_in, w_out, lb_param,
           r_gnorm, lam_q1, lam_k1, lam_q2, lam_k2, a_subln, norm_mix, norm_ffn, w_gate, w_up,
           w_down, norm_final):
    batch, seq, d = x_prompt.shape
    n_dec = x_sample.shape[0]
    depth = w_in.shape[0]
    assert depth == 1 and x_sample.shape[1] == 1
    hp = x_prompt.reshape(batch * seq, d)
    hs = x_sample.reshape(n_dec, d)
    lb_param = lb_param.astype(F32)
    nfin = norm_final.reshape(1, d)
    kp, vp, sp, kss, vss, sss = [], [], [], [], [], []
    for l in range(depth):
        lam_init = 0.8 - 0.6 * math.exp(-0.3 * l)
        w_in_bf = w_in[l].astype(BF16)
        wo_bf = w_out[l].astype(BF16)
        wg_bf = w_gate[l].astype(BF16)
        wu_bf = w_up[l].astype(BF16)
        wd_bf = w_down[l].astype(BF16)
        nmix = norm_mix[l].reshape(1, d)
        nffn = norm_ffn[l].reshape(1, d)
        gn = r_gnorm[l].reshape(1, HEAD)
        sub = a_subln[l].reshape(1, HEAD)
        lams = [a[l].reshape(1, A_DK).astype(F32) for a in (lam_q1, lam_k1, lam_q2, lam_k2)]

        qr, logf, kr, vr, gate, aq_bf, akt, av, ak_bf, av_bf = _inproj(
            hp, nmix, w_in_bf, lb_param, l, ROW_BLOCK, seq)
        o_r, s_new = _hgrn_prompt(qr, logf, kr, vr, batch, seq)
        o_a = _attn_prompt(aq_bf, ak_bf, av_bf, *lams, sub, lam_init, batch, seq)
        hp = _tail(hp, o_r, gate, o_a, gn, wo_bf, nffn, wg_bf, wu_bf, wd_bf, nfin, TAIL_BLOCK)
        kp.append(jnp.transpose(akt.reshape(batch, N_HEADS, 2, A_DK, seq), (0, 4, 1, 2, 3)))
        vp.append(av.reshape(batch, seq, N_HEADS, HEAD))
        sp.append(s_new)

        qr, logf, kr, vr, gate, aq_bf, akt, av, ak_bf, _ = _inproj(hs, nmix, w_in_bf, lb_param, l, n_dec, n_dec)
        o_r, s_new = _hgrn_sample(qr, logf, kr, vr, state_hgrn[l].astype(F32))
        o_a = _attn_sample(aq_bf, ak_bf, av, cache_k[l], cache_v[l], page_table, *lams,
                           jnp.tile(sub, (1, N_HEADS)), lam_init)
        hs = _tail(hs, o_r, gate, o_a, gn, wo_bf, nffn, wg_bf, wu_bf, wd_bf, nfin, n_dec)
        kss.append(jnp.transpose(akt.reshape(N_HEADS, 2, A_DK, n_dec), (3, 0, 1, 2))[:, None])
        vss.append(av.reshape(n_dec, 1, N_HEADS, HEAD))
        sss.append(s_new)

    y_prompt = hp.reshape(batch, seq, d)
    y_sample = hs.reshape(n_dec, 1, d)
    return (y_prompt, y_sample, jnp.stack(kp), jnp.stack(vp), jnp.stack(sp),
            jnp.stack(kss), jnp.stack(vss), jnp.stack(sss))
```

```python
import functools
import math

import jax
import jax.numpy as jnp
from jax import lax
from jax.experimental import pallas as pl
from jax.experimental.pallas import tpu as pltpu

F32 = jnp.float32
BF16 = jnp.bfloat16

EPS = 1e-6
LOG2E = math.log2(math.e)
HEAD = 128
N_HEADS = 4
A_DK = 64
GROUP = N_HEADS * HEAD
N_PIECES = 7
SUBLANES = 8
HGRN_CHUNK = 128
HGRN_BLOCK = 256
ATT_BLOCK = 1024
ATT_ROWS = 128
ROW_BLOCK = 256
TAIL_BLOCK = 512
TAIL_ROWS = 256
FF_CHUNK = 256
VMEM_LIMIT = 48 * 1024 * 1024


def _sigmoid(x):
    return 1.0 / (1.0 + jnp.exp(-x))


def _rms(x, w):
    return x * lax.rsqrt(jnp.mean(x * x, axis=-1, keepdims=True) + EPS) * w


def _const_spec(shape):
    return pl.BlockSpec(shape, lambda *_: (0,) * len(shape))


def _inproj_kernel(layer, x_ref, nw_ref, w_ref, lb_ref,
                   qr_ref, logf_ref, kr_ref, vr_ref, gate_ref,
                   aq_ref, akt_ref, av_ref, akb_ref, avb_ref):
    h = _rms(x_ref[...], nw_ref[...]).astype(BF16)

    def piece(j):
        return jnp.dot(h, w_ref[:, j * GROUP:(j + 1) * GROUP], preferred_element_type=F32)

    lbp = lb_ref[...]
    e = jnp.exp(lbp - jnp.max(lbp, axis=0, keepdims=True))
    lb = jnp.sum(e[:layer + 1], axis=0, keepdims=True) / jnp.sum(e, axis=0, keepdims=True)

    rq = piece(0)
    qr_ref[...] = rq * _sigmoid(rq)
    f = lb + (1.0 - lb) * _sigmoid(piece(1))
    logf_ref[...] = jnp.log(f)
    kr_ref[...] = 1.0 - f
    vr_ref[...] = piece(2)
    rg = piece(3)
    gate_ref[...] = rg * _sigmoid(rg)
    aq_ref[...] = (piece(4) * (A_DK ** -0.5 * LOG2E)).astype(BF16)
    ak = piece(5)
    akt_ref[0] = ak.T
    akb_ref[...] = ak.astype(BF16)
    av = piece(6)
    av_ref[...] = av
    avb_ref[...] = av.astype(BF16)


def _inproj(x, norm_w, w_bf, lb_param, layer, tm, seq):
    rows, d = x.shape
    d_in = w_bf.shape[1]
    per_seq = seq // tm
    f32_out = jax.ShapeDtypeStruct((rows, GROUP), F32)
    bf_out = jax.ShapeDtypeStruct((rows, GROUP), BF16)
    row_spec = pl.BlockSpec((tm, GROUP), lambda i: (i, 0))
    kt_out = jax.ShapeDtypeStruct((rows // seq, GROUP, seq), F32)
    kt_spec = pl.BlockSpec((1, GROUP, tm), lambda i: (i // per_seq, 0, i % per_seq))
    return pl.pallas_call(
        functools.partial(_inproj_kernel, layer),
        grid=(rows // tm,),
        in_specs=[pl.BlockSpec((tm, d), lambda i: (i, 0)),
                  _const_spec((1, d)),
                  _const_spec((d, d_in)),
                  _const_spec(lb_param.shape)],
        out_specs=[row_spec] * 6 + [kt_spec] + [row_spec] * 3,
        out_shape=[f32_out] * 5 + [bf_out, kt_out, f32_out, bf_out, bf_out],
        compiler_params=pltpu.CompilerParams(
            dimension_semantics=("parallel",), vmem_limit_bytes=VMEM_LIMIT),
        name="inproj",
    )(x, norm_w, w_bf, lb_param)


def _cumsum_rows(x):
    n, lanes = x.shape
    groups = n // SUBLANES
    x3 = x.reshape(groups, SUBLANES, lanes)
    sub = lax.broadcasted_iota(jnp.int32, x3.shape, 1)
    shift = 1
    while shift < SUBLANES:
        x3 = x3 + jnp.where(sub >= shift, pltpu.roll(x3, shift, 1), 0.0)
        shift *= 2
    totals = x3[:, SUBLANES - 1:SUBLANES, :]
    pre = totals
    shift = 1
    while shift < groups:
        pre = pre + jnp.concatenate([jnp.zeros((shift, 1, lanes), F32), pre[:groups - shift]], axis=0)
        shift *= 2
    return (x3 + (pre - totals)).reshape(n, lanes)


def _hgrn_prompt_kernel(q_ref, g_ref, k_ref, v_ref, o_ref, s_ref, st_scr):
    i = pl.program_id(0)

    @pl.when(i == 0)
    def _():
        st_scr[...] = jnp.zeros_like(st_scr)

    c = HGRN_CHUNK
    half = c // 2
    row = lax.broadcasted_iota(jnp.int32, (c, c), 0)
    col = lax.broadcasted_iota(jnp.int32, (c, c), 1)
    tri = col <= row
    nt = (((1,), (1,)), ((), ()))
    tn = (((0,), (0,)), ((), ()))

    for bi in range(q_ref.shape[0]):
        for ci in range(HGRN_BLOCK // c):
            rs = slice(ci * c, (ci + 1) * c)
            for h in range(N_HEADS):
                cs = slice(h * HEAD, (h + 1) * HEAD)
                b = _cumsum_rows(g_ref[bi, rs, cs])
                b_mid = b[half - 1:half, :]
                b_last = b[c - 1:c, :]
                d = b - b_mid
                qm = q_ref[bi, rs, cs] * jnp.exp(d)
                km = k_ref[bi, rs, cs] * jnp.exp(-d)
                q0 = (qm * jnp.exp(b_mid)).astype(BF16)
                k2 = (km * jnp.exp(b_last - b_mid)).astype(BF16)
                v = v_ref[bi, rs, cs].astype(BF16)
                a = lax.dot_general(qm.astype(BF16), km.astype(BF16), nt, preferred_element_type=F32)
                a = jnp.where(tri, a, 0.0).astype(BF16)
                st = st_scr[bi, h]
                o = (lax.dot_general(q0, st.astype(BF16), nt, preferred_element_type=F32)
                     + jnp.dot(a, v, preferred_element_type=F32))
                o_ref[bi, rs, cs] = o
                st_scr[bi, h] = st * jnp.exp(b_last) + lax.dot_general(v, k2, tn, preferred_element_type=F32)

    @pl.when(i == pl.num_programs(0) - 1)
    def _():
        for bi in range(q_ref.shape[0]):
            for h in range(N_HEADS):
                s_ref[bi, h] = st_scr[bi, h].T


def _hgrn_prompt(qr, logf, kr, vr, batch, seq):
    nb = seq // HGRN_BLOCK
    spec = pl.BlockSpec((batch, HGRN_BLOCK, GROUP), lambda i: (0, i, 0))
    shape3 = (batch, seq, GROUP)
    o_r, s_new = pl.pallas_call(
        _hgrn_prompt_kernel,
        grid=(nb,),
        in_specs=[spec] * 4,
        out_specs=[spec, pl.BlockSpec((batch, N_HEADS, HEAD, HEAD), lambda i: (0, 0, 0, 0))],
        out_shape=[jax.ShapeDtypeStruct(shape3, F32),
                   jax.ShapeDtypeStruct((batch, N_HEADS, HEAD, HEAD), F32)],
        scratch_shapes=[pltpu.VMEM((batch, N_HEADS, HEAD, HEAD), F32)],
        compiler_params=pltpu.CompilerParams(
            dimension_semantics=("arbitrary",), vmem_limit_bytes=VMEM_LIMIT),
        name="hgrn_prompt",
    )(*(a.reshape(shape3) for a in (qr, logf, kr, vr)))
    return o_r.reshape(batch * seq, GROUP), s_new


def _hgrn_sample_kernel(q_ref, g_ref, k_ref, v_ref, s_in_ref, o_ref, s_out_ref):
    nb = q_ref.shape[0]
    for h in range(N_HEADS):
        cs = slice(h * HEAD, (h + 1) * HEAD)
        qt = q_ref[:, cs].T
        ft = jnp.exp(g_ref[:, cs]).T
        kt = k_ref[:, cs].T
        for j in range(nb):
            s_new = ft[:, j:j + 1] * s_in_ref[j, h] + kt[:, j:j + 1] * v_ref[j:j + 1, cs]
            s_out_ref[j, h] = s_new
            o_ref[j:j + 1, cs] = jnp.sum(qt[:, j:j + 1] * s_new, axis=0, keepdims=True)


def _hgrn_sample(qr, logf, kr, vr, state):
    n = qr.shape[0]
    nb = 8
    row_spec = pl.BlockSpec((nb, GROUP), lambda i: (i, 0))
    st_spec = pl.BlockSpec((nb, N_HEADS, HEAD, HEAD), lambda i: (i, 0, 0, 0))
    return pl.pallas_call(
        _hgrn_sample_kernel,
        grid=(n // nb,),
        in_specs=[row_spec] * 4 + [st_spec],
        out_specs=[row_spec, st_spec],
        out_shape=[jax.ShapeDtypeStruct((n, GROUP), F32),
                   jax.ShapeDtypeStruct(state.shape, F32)],
        compiler_params=pltpu.CompilerParams(
            dimension_semantics=("parallel",), vmem_limit_bytes=VMEM_LIMIT),
        name="hgrn_sample",
    )(qr, logf, kr, vr, state)


def _lambda(lq1_ref, lk1_ref, lq2_ref, lk2_ref, lam_init):
    a = jnp.sum(lq1_ref[...] * lk1_ref[...], axis=-1, keepdims=True)
    b = jnp.sum(lq2_ref[...] * lk2_ref[...], axis=-1, keepdims=True)
    return jnp.exp(a) - jnp.exp(b) + lam_init


ATT_FULL, ATT_FULL_DIAG, ATT_DIAG = 0, 1, 2


def _attn_prompt_kernel(lam_init, qi_tab, kb_tab, kind_tab, first_tab,
                        q_ref, k_ref, v_ref, lq1_ref, lk1_ref, lq2_ref, lk2_ref, sub_ref,
                        o_ref, m_scr, acc_scr):
    p = pl.program_id(2)
    kind = kind_tab[p]

    @pl.when(first_tab[p] == 1)
    def _():
        m_scr[...] = jnp.full_like(m_scr, -jnp.inf)
        acc_scr[...] = jnp.zeros_like(acc_scr)

    tq, tk = q_ref.shape[0], k_ref.shape[0]
    half_chunks = tq // HEAD
    nt = (((1,), (1,)), ((), ()))

    def sweep(step_kind):
        masked = step_kind != ATT_FULL
        v_ext = jnp.concatenate([v_ref[...], jnp.ones((tk, HEAD), BF16)], axis=-1)
        lane = lax.broadcasted_iota(jnp.int32, (ATT_ROWS, HEAD), 1)
        row = lax.broadcasted_iota(jnp.int32, (ATT_ROWS, HEAD), 0)
        zero = jnp.zeros((ATT_ROWS, HEAD), BF16)
        group_chunks = ATT_ROWS // HEAD
        for r in range(tq // ATT_ROWS):
            rows = slice(r * ATT_ROWS, (r + 1) * ATT_ROWS)
            q = q_ref[rows, :]
            diag_chunks = (r + 1) * group_chunks
            n_chunks = {ATT_FULL: 2 * half_chunks, ATT_FULL_DIAG: half_chunks + diag_chunks,
                        ATT_DIAG: diag_chunks}[step_kind]
            k = k_ref[0:n_chunks * HEAD, :]
            for mi, qm in enumerate((jnp.where(lane < A_DK, q, zero), jnp.where(lane >= A_DK, q, zero))):
                s = lax.dot_general(qm, k, nt, preferred_element_type=F32)
                chunks = [s[:, c * HEAD:(c + 1) * HEAD] for c in range(n_chunks)]
                if masked:
                    for g in range(group_chunks):
                        c = n_chunks - group_chunks + g
                        chunks[c] = jnp.where(lane + g * HEAD <= row, chunks[c], -jnp.inf)
                mc = functools.reduce(jnp.maximum, chunks)
                m_prev = m_scr[mi, rows]
                m_new = jnp.maximum(m_prev, jnp.max(mc, axis=-1, keepdims=True))
                alpha = jnp.exp2(m_prev - m_new)
                pr = jnp.concatenate([jnp.exp2(sc - m_new).astype(BF16) for sc in chunks], axis=-1)
                acc_scr[mi, rows] = (jnp.concatenate([alpha, alpha], axis=-1) * acc_scr[mi, rows]
                                     + jnp.dot(pr, v_ext[0:n_chunks * HEAD], preferred_element_type=F32))
                m_scr[mi, rows] = m_new

    def finalize():
        lam = _lambda(lq1_ref, lk1_ref, lq2_ref, lk2_ref, lam_init)
        a0 = acc_scr[0]
        a1 = acc_scr[1]
        o = a0[:, :HEAD] / a0[:, HEAD:] - lam * (a1[:, :HEAD] / a1[:, HEAD:])
        o_ref[...] = (_rms(o, sub_ref[...]) * (1.0 - lam_init)).astype(o_ref.dtype)

    @pl.when(kind == ATT_FULL)
    def _():
        sweep(ATT_FULL)

    @pl.when(kind == ATT_FULL_DIAG)
    def _():
        sweep(ATT_FULL_DIAG)
        finalize()

    @pl.when(kind == ATT_DIAG)
    def _():
        sweep(ATT_DIAG)
        finalize()


def _attn_prompt(aq_bf, ak_bf, av_bf, lq1, lk1, lq2, lk2, subln, lam_init, batch, seq):
    t = ATT_BLOCK
    nq = seq // t
    assert ATT_ROWS % HEAD == 0 and nq % 2 == 0
    steps = []
    for i in range(nq):
        for j in range(i // 2):
            steps.append((i, j, ATT_FULL, int(j == 0)))
        steps.append((i, i // 2, ATT_FULL_DIAG if i % 2 else ATT_DIAG, int(i < 2)))
    qi_tab, kb_tab, kind_tab, first_tab = (jnp.asarray([s[c] for s in steps], jnp.int32) for c in range(4))
    q_spec = pl.BlockSpec((t, HEAD), lambda b, h, p, qi, kb, kind, first: (b * nq + qi[p], h))
    kv_spec = pl.BlockSpec((2 * t, HEAD), lambda b, h, p, qi, kb, kind, first: (b * (nq // 2) + kb[p], h))
    lam_spec = pl.BlockSpec((1, A_DK), lambda *_: (0, 0))
    sub_spec = pl.BlockSpec((1, HEAD), lambda *_: (0, 0))
    return pl.pallas_call(
        functools.partial(_attn_prompt_kernel, lam_init),
        grid_spec=pltpu.PrefetchScalarGridSpec(
            num_scalar_prefetch=4,
            grid=(batch, N_HEADS, len(steps)),
            in_specs=[q_spec, kv_spec, kv_spec, lam_spec, lam_spec, lam_spec, lam_spec, sub_spec],
            out_specs=q_spec,
            scratch_shapes=[pltpu.VMEM((2, t, HEAD), F32), pltpu.VMEM((2, t, 2 * HEAD), F32)]),
        out_shape=jax.ShapeDtypeStruct((batch * seq, GROUP), BF16),
        compiler_params=pltpu.CompilerParams(
            dimension_semantics=("parallel", "parallel", "arbitrary"), vmem_limit_bytes=VMEM_LIMIT),
        name="attn_prompt",
    )(qi_tab, kb_tab, kind_tab, first_tab, aq_bf, ak_bf, av_bf, lq1, lk1, lq2, lk2, subln)


def _attn_sample_kernel(lam_init, n_pages, pt_ref,
                        q_ref, kn_ref, vn_ref, lq1_ref, lk1_ref, lq2_ref, lk2_ref, sub_ref,
                        ck_hbm, cv_hbm, o_ref, kbuf, vbuf, sem):
    b = pl.program_id(0)
    nb = pl.num_programs(0)

    page_rows = ck_hbm.shape[1]
    page_size = ck_hbm.shape[2]

    def page_copies(bb, slot):
        out = []
        for j in range(n_pages):
            page = pt_ref[bb * n_pages + j]
            out.append(pltpu.make_async_copy(ck_hbm.at[page], kbuf.at[slot, j], sem.at[slot, 0]))
            out.append(pltpu.make_async_copy(cv_hbm.at[page], vbuf.at[slot, pl.ds(j * page_rows, page_rows)],
                                             sem.at[slot, 1]))
        return out

    @pl.when(b == 0)
    def _():
        for cp in page_copies(0, 0):
            cp.start()

    slot = b % 2

    @pl.when(b + 1 < nb)
    def _():
        for cp in page_copies(b + 1, 1 - slot):
            cp.start()

    for cp in page_copies(b, slot):
        cp.wait()

    past = n_pages * page_size
    q = q_ref[0]
    rows = 2 * N_HEADS
    ri = lax.broadcasted_iota(jnp.int32, (rows, GROUP), 0)
    li = lax.broadcasted_iota(jnp.int32, (rows, GROUP), 1)
    mine = (li // A_DK) == (ri % N_HEADS) * 2 + ri // N_HEADS
    qbd = jnp.where(mine, jnp.broadcast_to(q.astype(F32), (rows, GROUP)), 0.0)
    qbd_bf = qbd.astype(BF16)
    s = jnp.concatenate(
        [jnp.dot(qbd_bf, kbuf[slot, j].astype(BF16), preferred_element_type=F32) for j in range(n_pages)],
        axis=-1)
    s_new = jnp.sum(qbd * kn_ref[0].astype(F32), axis=-1, keepdims=True)
    m = jnp.maximum(jnp.max(s, axis=-1, keepdims=True), s_new)
    p = jnp.exp2(s - m)
    p_new = jnp.exp2(s_new - m)
    inv_l = 1.0 / (jnp.sum(p, axis=-1, keepdims=True) + p_new)
    lam = _lambda(lq1_ref, lk1_ref, lq2_ref, lk2_ref, lam_init)
    pn = p * inv_l
    pn_new = p_new * inv_l
    attn = pn - lam * pltpu.roll(pn, N_HEADS, 0)
    attn_new = pn_new - lam * pltpu.roll(pn_new, N_HEADS, 0)
    attn_bf = attn.astype(BF16)
    o = jnp.concatenate(
        [jnp.dot(attn_bf, vbuf[slot, pl.ds(h, past, stride=N_HEADS), :].astype(BF16), preferred_element_type=F32)
         for h in range(N_HEADS)], axis=-1) + attn_new * vn_ref[0]
    keep = jnp.logical_and(ri < N_HEADS, li // HEAD == ri)
    o = jnp.where(keep, o, 0.0)
    ms = jnp.sum(o * o, axis=-1, keepdims=True) * (1.0 / HEAD)
    o = o * lax.rsqrt(ms + EPS) * sub_ref[...] * (1.0 - lam_init)
    o_ref[0] = jnp.sum(o, axis=0, keepdims=True)


def _attn_sample(aq_bf, ak, av, cache_k, cache_v, page_table, lq1, lk1, lq2, lk2, subln4, lam_init):
    n, n_pages = page_table.shape
    n_phys, page_size = cache_k.shape[0], cache_k.shape[1]
    ck = jnp.transpose(cache_k, (0, 2, 3, 4, 1)).reshape(n_phys, GROUP, page_size)
    cv = cache_v.reshape(n_phys, page_size * N_HEADS, HEAD)
    row_spec = pl.BlockSpec((1, 1, GROUP), lambda b, pt: (b, 0, 0))
    lam_spec = pl.BlockSpec((1, A_DK), lambda b, pt: (0, 0))
    out = pl.pallas_call(
        functools.partial(_attn_sample_kernel, lam_init, n_pages),
        grid_spec=pltpu.PrefetchScalarGridSpec(
            num_scalar_prefetch=1,
            grid=(n,),
            in_specs=[row_spec, row_spec, row_spec, lam_spec, lam_spec, lam_spec, lam_spec,
                      pl.BlockSpec((1, GROUP), lambda b, pt: (0, 0)),
                      pl.BlockSpec(memory_space=pl.ANY), pl.BlockSpec(memory_space=pl.ANY)],
            out_specs=row_spec,
            scratch_shapes=[pltpu.VMEM((2, n_pages, GROUP, page_size), F32),
                            pltpu.VMEM((2, n_pages * page_size * N_HEADS, HEAD), F32),
                            pltpu.SemaphoreType.DMA((2, 2))]),
        out_shape=jax.ShapeDtypeStruct((n, 1, GROUP), F32),
        compiler_params=pltpu.CompilerParams(
            dimension_semantics=("arbitrary",), vmem_limit_bytes=VMEM_LIMIT),
        name="attn_sample",
    )(page_table.reshape(-1), aq_bf.reshape(n, 1, GROUP), ak.reshape(n, 1, GROUP), av.reshape(n, 1, GROUP),
      lq1, lk1, lq2, lk2, subln4, ck, cv)
    return out.reshape(n, GROUP)


def _tail_kernel(x_ref, or_ref, gate_ref, oa_ref, gn_ref, wo_ref, nf_ref, wg_ref, wu_ref, wd_ref, nfin_ref,
                 y_ref, act_scr):
    tm = x_ref.shape[0]
    d_ff = wg_ref.shape[1]
    sub = min(tm, TAIL_ROWS)
    for g in range(tm // sub):
        rows = slice(g * sub, (g + 1) * sub)
        parts = []
        for h in range(N_HEADS):
            cs = slice(h * HEAD, (h + 1) * HEAD)
            parts.append((_rms(or_ref[rows, cs], gn_ref[...]) * gate_ref[rows, cs]).astype(BF16))
        parts.append(oa_ref[rows, :].astype(BF16))
        mix = jnp.concatenate(parts, axis=-1)
        x1 = x_ref[rows, :] + jnp.dot(mix, wo_ref[...], preferred_element_type=F32)
        h2 = _rms(x1, nf_ref[...]).astype(BF16)
        for j in range(d_ff // FF_CHUNK):
            fs = slice(j * FF_CHUNK, (j + 1) * FF_CHUNK)
            gt = jnp.dot(h2, wg_ref[:, fs], preferred_element_type=F32)
            up = jnp.dot(h2, wu_ref[:, fs], preferred_element_type=F32)
            act_scr[rows, fs] = (gt * _sigmoid(gt) * up).astype(BF16)
        y = x1 + jnp.dot(act_scr[rows, :], wd_ref[...], preferred_element_type=F32)
        y_ref[rows, :] = _rms(y, nfin_ref[...])


def _tail(x, o_r, gate, o_a, r_gnorm, wo_bf, norm_ffn, wg_bf, wu_bf, wd_bf, norm_final, tm):
    rows, d = x.shape
    d_ff = wg_bf.shape[1]

    def resident(shape):
        return pl.BlockSpec(shape, lambda i: (0, 0), pipeline_mode=pl.Buffered(1))

    return pl.pallas_call(
        _tail_kernel,
        grid=(rows // tm,),
        in_specs=[pl.BlockSpec((tm, d), lambda i: (i, 0)),
                  pl.BlockSpec((tm, GROUP), lambda i: (i, 0)),
                  pl.BlockSpec((tm, GROUP), lambda i: (i, 0)),
                  pl.BlockSpec((tm, GROUP), lambda i: (i, 0)),
                  _const_spec((1, HEAD)),
                  resident((d, d)),
                  _const_spec((1, d)),
                  resident((d, d_ff)), resident((d, d_ff)), resident((d_ff, d)),
                  _const_spec((1, d))],
        out_specs=pl.BlockSpec((tm, d), lambda i: (i, 0)),
        out_shape=jax.ShapeDtypeStruct((rows, d), F32),
        scratch_shapes=[pltpu.VMEM((tm, d_ff), BF16)],
        compiler_params=pltpu.CompilerParams(
            dimension_semantics=("parallel",), vmem_limit_bytes=VMEM_LIMIT),
        name="tail",
    )(x, o_r, gate, o_a, r_gnorm, wo_bf, norm_ffn, wg_bf, wu_bf, wd_bf, norm_final)


def kernel(x_prompt, x_sample, cache_k, cache_v, state_hgrn, page_table, w_in, w_out, lb_param,
           r_gnorm, lam_q1, lam_k1, lam_q2, lam_k2, a_subln, norm_mix, norm_ffn, w_gate, w_up,
           w_down, norm_final):
    batch, seq, d = x_prompt.shape
    n_dec = x_sample.shape[0]
    depth = w_in.shape[0]
    assert depth == 1 and x_sample.shape[1] == 1
    hp = x_prompt.reshape(batch * seq, d)
    hs = x_sample.reshape(n_dec, d)
    lb_param = lb_param.astype(F32)
    nfin = norm_final.reshape(1, d)
    kp, vp, sp, kss, vss, sss = [], [], [], [], [], []
    for l in range(depth):
        lam_init = 0.8 - 0.6 * math.exp(-0.3 * l)
        w_in_bf = w_in[l].astype(BF16)
        wo_bf = w_out[l].astype(BF16)
        wg_bf = w_gate[l].astype(BF16)
        wu_bf = w_up[l].astype(BF16)
        wd_bf = w_down[l].astype(BF16)
        nmix = norm_mix[l].reshape(1, d)
        nffn = norm_ffn[l].reshape(1, d)
        gn = r_gnorm[l].reshape(1, HEAD)
        sub = a_subln[l].reshape(1, HEAD)
        lams = [a[l].reshape(1, A_DK).astype(F32) for a in (lam_q1, lam_k1, lam_q2, lam_k2)]

        qr, logf, kr, vr, gate, aq_bf, akt, av, ak_bf, av_bf = _inproj(
            hp, nmix, w_in_bf, lb_param, l, ROW_BLOCK, seq)
        o_r, s_new = _hgrn_prompt(qr, logf, kr, vr, batch, seq)
        o_a = _attn_prompt(aq_bf, ak_bf, av_bf, *lams, sub, lam_init, batch, seq)
        hp = _tail(hp, o_r, gate, o_a, gn, wo_bf, nffn, wg_bf, wu_bf, wd_bf, nfin, TAIL_BLOCK)
        kp.append(jnp.transpose(akt.reshape(batch, N_HEADS, 2, A_DK, seq), (0, 4, 1, 2, 3)))
        vp.append(av.reshape(batch, seq, N_HEADS, HEAD))
        sp.append(s_new)

        qr, logf, kr, vr, gate, aq_bf, akt, av, ak_bf, _ = _inproj(hs, nmix, w_in_bf, lb_param, l, n_dec, n_dec)
        o_r, s_new = _hgrn_sample(qr, logf, kr, vr, state_hgrn[l].astype(F32))
        o_a = _attn_sample(aq_bf, ak_bf, av, cache_k[l], cache_v[l], page_table, *lams,
                           jnp.tile(sub, (1, N_HEADS)), lam_init)
        hs = _tail(hs, o_r, gate, o_a, gn, wo_bf, nffn, wg_bf, wu_bf, wd_bf, nfin, n_dec)
        kss.append(jnp.transpose(akt.reshape(N_HEADS, 2, A_DK, n_dec), (3, 0, 1, 2))[:, None])
        vss.append(av.reshape(n_dec, 1, N_HEADS, HEAD))
        sss.append(s_new)

    y_prompt = hp.reshape(batch, seq, d)
    y_sample = hs.reshape(n_dec, 1, d)
    return (y_prompt, y_sample, jnp.stack(kp), jnp.stack(vp), jnp.stack(sp),
            jnp.stack(kss), jnp.stack(vss), jnp.stack(sss))
```

```python
import functools
import math

import jax
import jax.numpy as jnp
from jax import lax
from jax.experimental import pallas as pl
from jax.experimental.pallas import tpu as pltpu

F32 = jnp.float32
BF16 = jnp.bfloat16

EPS = 1e-6
LOG2E = math.log2(math.e)
HEAD = 128
N_HEADS = 4
A_DK = 64
GROUP = N_HEADS * HEAD
N_PIECES = 7
SUBLANES = 8
HGRN_CHUNK = 128
HGRN_BLOCK = 256
HGRN_SAFE_EXPONENT = 60.0
ATT_BLOCK = 2048
ATT_KEY_BLOCK = 2048
ATT_ROWS = 128
ROW_BLOCK = 256
TAIL_BLOCK = 512
TAIL_ROWS = 256
FF_CHUNK = 256
VMEM_LIMIT = 48 * 1024 * 1024


def _sigmoid(x):
    return 1.0 / (1.0 + jnp.exp(-x))


def _rms(x, w):
    return x * lax.rsqrt(jnp.mean(x * x, axis=-1, keepdims=True) + EPS) * w


def _const_spec(shape):
    return pl.BlockSpec(shape, lambda *_: (0,) * len(shape))


def _inproj_kernel(layer, x_ref, nw_ref, w_ref, lb_ref,
                   qr_ref, logf_ref, kr_ref, vr_ref, gate_ref,
                   aq_ref, akt_ref, av_ref, akb_ref, avb_ref):
    h = _rms(x_ref[...], nw_ref[...]).astype(BF16)

    def piece(j):
        return jnp.dot(h, w_ref[:, j * GROUP:(j + 1) * GROUP], preferred_element_type=F32)

    lbp = lb_ref[...]
    e = jnp.exp(lbp - jnp.max(lbp, axis=0, keepdims=True))
    lb = jnp.sum(e[:layer + 1], axis=0, keepdims=True) / jnp.sum(e, axis=0, keepdims=True)

    rq = piece(0)
    qr_ref[...] = rq * _sigmoid(rq)
    f = lb + (1.0 - lb) * _sigmoid(piece(1))
    logf_ref[...] = jnp.log(f)
    kr_ref[...] = 1.0 - f
    vr_ref[...] = piece(2)
    rg = piece(3)
    gate_ref[...] = rg * _sigmoid(rg)
    aq_ref[...] = (piece(4) * (A_DK ** -0.5 * LOG2E)).astype(BF16)
    ak = piece(5)
    akt_ref[0] = ak.T
    akb_ref[...] = ak.astype(BF16)
    av = piece(6)
    av_ref[...] = av
    avb_ref[...] = av.astype(BF16)


def _inproj(x, norm_w, w_bf, lb_param, layer, tm, seq):
    rows, d = x.shape
    d_in = w_bf.shape[1]
    per_seq = seq // tm
    f32_out = jax.ShapeDtypeStruct((rows, GROUP), F32)
    bf_out = jax.ShapeDtypeStruct((rows, GROUP), BF16)
    row_spec = pl.BlockSpec((tm, GROUP), lambda i: (i, 0))
    kt_out = jax.ShapeDtypeStruct((rows // seq, GROUP, seq), F32)
    kt_spec = pl.BlockSpec((1, GROUP, tm), lambda i: (i // per_seq, 0, i % per_seq))
    return pl.pallas_call(
        functools.partial(_inproj_kernel, layer),
        grid=(rows // tm,),
        in_specs=[pl.BlockSpec((tm, d), lambda i: (i, 0)),
                  _const_spec((1, d)),
                  _const_spec((d, d_in)),
                  _const_spec(lb_param.shape)],
        out_specs=[row_spec] * 6 + [kt_spec] + [row_spec] * 3,
        out_shape=[f32_out] * 5 + [bf_out, kt_out, f32_out, bf_out, bf_out],
        compiler_params=pltpu.CompilerParams(
            dimension_semantics=("parallel",), vmem_limit_bytes=VMEM_LIMIT),
        name="inproj",
    )(x, norm_w, w_bf, lb_param)


def _cumsum_rows(x):
    n, lanes = x.shape
    groups = n // SUBLANES
    x3 = x.reshape(groups, SUBLANES, lanes)
    sub = lax.broadcasted_iota(jnp.int32, x3.shape, 1)
    shift = 1
    while shift < SUBLANES:
        x3 = x3 + jnp.where(sub >= shift, pltpu.roll(x3, shift, 1), 0.0)
        shift *= 2
    totals = x3[:, SUBLANES - 1:SUBLANES, :]
    pre = totals
    shift = 1
    while shift < groups:
        pre = pre + jnp.concatenate([jnp.zeros((shift, 1, lanes), F32), pre[:groups - shift]], axis=0)
        shift *= 2
    return (x3 + (pre - totals)).reshape(n, lanes)


def _hgrn_prompt_kernel(q_ref, g_ref, k_ref, v_ref, o_ref, s_ref, st_scr, b_scr, oi_scr):
    i = pl.program_id(0)

    @pl.when(i == 0)
    def _():
        st_scr[...] = jnp.zeros_like(st_scr)

    c = HGRN_CHUNK
    half = c // 2
    row = lax.broadcasted_iota(jnp.int32, (c, c), 0)
    col = lax.broadcasted_iota(jnp.int32, (c, c), 1)
    tri = col <= row
    nt = (((1,), (1,)), ((), ()))
    tn = (((0,), (0,)), ((), ()))
    chains = [(bi, ci, h) for bi in range(q_ref.shape[0]) for ci in range(HGRN_BLOCK // c)
              for h in range(N_HEADS)]

    def window(bi, ci, h):
        return bi, slice(ci * c, (ci + 1) * c), slice(h * HEAD, (h + 1) * HEAD)

    worst = jnp.zeros((1, HEAD), F32)
    for chain in chains:
        w = window(*chain)
        b = _cumsum_rows(g_ref[w])
        b_scr[w] = b
        b_mid = b[half - 1:half, :]
        worst = jnp.maximum(worst, jnp.maximum(-b_mid, b_mid - b[c - 1:c, :]))
    safe = jnp.max(worst) < HGRN_SAFE_EXPONENT

    def finish(chain, intra, q0, k2, v, b_last):
        bi, _, h = chain
        st = st_scr[bi, h]
        o_ref[window(*chain)] = lax.dot_general(q0, st.astype(BF16), nt, preferred_element_type=F32) + intra
        st_scr[bi, h] = st * jnp.exp(b_last) + lax.dot_general(v, k2, tn, preferred_element_type=F32)

    @pl.when(safe)
    def _():
        for chain in chains:
            w = window(*chain)
            b = b_scr[w]
            b_mid = b[half - 1:half, :]
            b_last = b[c - 1:c, :]
            d = b - b_mid
            qm = q_ref[w] * jnp.exp(d)
            km = k_ref[w] * jnp.exp(-d)
            q0 = (qm * jnp.exp(b_mid)).astype(BF16)
            k2 = (km * jnp.exp(b_last - b_mid)).astype(BF16)
            v = v_ref[w].astype(BF16)
            a = lax.dot_general(qm.astype(BF16), km.astype(BF16), nt, preferred_element_type=F32)
            a = jnp.where(tri, a, 0.0).astype(BF16)
            finish(chain, jnp.dot(a, v, preferred_element_type=F32), q0, k2, v, b_last)

    @pl.when(jnp.logical_not(safe))
    def _():
        srow = lax.broadcasted_iota(jnp.int32, (c, HEAD), 0)
        for chain in chains:
            bi, rs, cs = window(*chain)
            b = b_scr[bi, rs, cs]
            b_last = b[c - 1:c, :]
            kk = k_ref[bi, rs, cs]
            vv = v_ref[bi, rs, cs]

            def query_rows(tg, carry, bi=bi, rs=rs, cs=cs, b=b, kk=kk, vv=vv):
                base = pl.multiple_of(tg * SUBLANES, SUBLANES)
                at = pl.ds(rs.start + base, SUBLANES)
                b8 = b_scr[bi, at, cs]
                q8 = q_ref[bi, at, cs]
                out = []
                for j in range(SUBLANES):
                    decay = jnp.exp(jnp.where(srow <= base + j, b8[j:j + 1, :] - b, -jnp.inf))
                    a_col = jnp.sum(q8[j:j + 1, :] * kk * decay, axis=-1, keepdims=True)
                    out.append(jnp.sum(a_col * vv, axis=0, keepdims=True))
                oi_scr[pl.ds(base, SUBLANES), :] = jnp.concatenate(out, axis=0)
                return carry

            lax.fori_loop(0, c // SUBLANES, query_rows, 0)
            q0 = (q_ref[bi, rs, cs] * jnp.exp(b)).astype(BF16)
            k2 = (kk * jnp.exp(b_last - b)).astype(BF16)
            finish(chain, oi_scr[...], q0, k2, vv.astype(BF16), b_last)

    @pl.when(i == pl.num_programs(0) - 1)
    def _():
        for bi in range(q_ref.shape[0]):
            for h in range(N_HEADS):
                s_ref[bi, h] = st_scr[bi, h].T


def _hgrn_prompt(qr, logf, kr, vr, batch, seq):
    nb = seq // HGRN_BLOCK
    spec = pl.BlockSpec((batch, HGRN_BLOCK, GROUP), lambda i: (0, i, 0))
    shape3 = (batch, seq, GROUP)
    o_r, s_new = pl.pallas_call(
        _hgrn_prompt_kernel,
        grid=(nb,),
        in_specs=[spec] * 4,
        out_specs=[spec, pl.BlockSpec((batch, N_HEADS, HEAD, HEAD), lambda i: (0, 0, 0, 0))],
        out_shape=[jax.ShapeDtypeStruct(shape3, F32),
                   jax.ShapeDtypeStruct((batch, N_HEADS, HEAD, HEAD), F32)],
        scratch_shapes=[pltpu.VMEM((batch, N_HEADS, HEAD, HEAD), F32),
                        pltpu.VMEM((batch, HGRN_BLOCK, GROUP), F32),
                        pltpu.VMEM((HGRN_CHUNK, HEAD), F32)],
        compiler_params=pltpu.CompilerParams(
            dimension_semantics=("arbitrary",), vmem_limit_bytes=VMEM_LIMIT),
        name="hgrn_prompt",
    )(*(a.reshape(shape3) for a in (qr, logf, kr, vr)))
    return o_r.reshape(batch * seq, GROUP), s_new


def _hgrn_sample_kernel(q_ref, g_ref, k_ref, v_ref, s_in_ref, o_ref, s_out_ref):
    nb = q_ref.shape[0]
    for h in range(N_HEADS):
        cs = slice(h * HEAD, (h + 1) * HEAD)
        qt = q_ref[:, cs].T
        ft = jnp.exp(g_ref[:, cs]).T
        kt = k_ref[:, cs].T
        for j in range(nb):
            s_new = ft[:, j:j + 1] * s_in_ref[j, h] + kt[:, j:j + 1] * v_ref[j:j + 1, cs]
            s_out_ref[j, h] = s_new
            o_ref[j:j + 1, cs] = jnp.sum(qt[:, j:j + 1] * s_new, axis=0, keepdims=True)


def _hgrn_sample(qr, logf, kr, vr, state):
    n = qr.shape[0]
    nb = 8
    row_spec = pl.BlockSpec((nb, GROUP), lambda i: (i, 0))
    st_spec = pl.BlockSpec((nb, N_HEADS, HEAD, HEAD), lambda i: (i, 0, 0, 0))
    return pl.pallas_call(
        _hgrn_sample_kernel,
        grid=(n // nb,),
        in_specs=[row_spec] * 4 + [st_spec],
        out_specs=[row_spec, st_spec],
        out_shape=[jax.ShapeDtypeStruct((n, GROUP), F32),
                   jax.ShapeDtypeStruct(state.shape, F32)],
        compiler_params=pltpu.CompilerParams(
            dimension_semantics=("parallel",), vmem_limit_bytes=VMEM_LIMIT),
        name="hgrn_sample",
    )(qr, logf, kr, vr, state)


def _lambda(lq1_ref, lk1_ref, lq2_ref, lk2_ref, lam_init):
    a = jnp.sum(lq1_ref[...] * lk1_ref[...], axis=-1, keepdims=True)
    b = jnp.sum(lq2_ref[...] * lk2_ref[...], axis=-1, keepdims=True)
    return jnp.exp(a) - jnp.exp(b) + lam_init


ATT_FULL, ATT_FULL_DIAG, ATT_DIAG = 0, 1, 2


def _attn_prompt_kernel(lam_init, qi_tab, kb_tab, kind_tab, first_tab,
                        q_ref, k_ref, v_ref, lq1_ref, lk1_ref, lq2_ref, lk2_ref, sub_ref,
                        o_ref, m_scr, acc_scr):
    p = pl.program_id(2)
    kind = kind_tab[p]

    @pl.when(first_tab[p] == 1)
    def _():
        m_scr[...] = jnp.full_like(m_scr, -jnp.inf)
        acc_scr[...] = jnp.zeros_like(acc_scr)

    tq, tk = q_ref.shape[0], k_ref.shape[0]
    half_chunks = tq // HEAD
    nt = (((1,), (1,)), ((), ()))

    def sweep(step_kind):
        masked = step_kind != ATT_FULL
        v_ext = jnp.concatenate([v_ref[...], jnp.ones((tk, HEAD), BF16)], axis=-1)
        lane = lax.broadcasted_iota(jnp.int32, (ATT_ROWS, HEAD), 1)
        row = lax.broadcasted_iota(jnp.int32, (ATT_ROWS, HEAD), 0)
        zero = jnp.zeros((ATT_ROWS, HEAD), BF16)
        group_chunks = ATT_ROWS // HEAD
        for r in range(tq // ATT_ROWS):
            rows = slice(r * ATT_ROWS, (r + 1) * ATT_ROWS)
            q = q_ref[rows, :]
            diag_chunks = (r + 1) * group_chunks
            n_chunks = {ATT_FULL: tk // HEAD, ATT_FULL_DIAG: half_chunks + diag_chunks,
                        ATT_DIAG: diag_chunks}[step_kind]
            k = k_ref[0:n_chunks * HEAD, :]
            probs, alphas = [], []
            for mi, qm in enumerate((jnp.where(lane < A_DK, q, zero), jnp.where(lane >= A_DK, q, zero))):
                s = lax.dot_general(qm, k, nt, preferred_element_type=F32)
                chunks = [s[:, c * HEAD:(c + 1) * HEAD] for c in range(n_chunks)]
                if masked:
                    for g in range(group_chunks):
                        c = n_chunks - group_chunks + g
                        chunks[c] = jnp.where(lane + g * HEAD <= row, chunks[c], -jnp.inf)
                mc = functools.reduce(jnp.maximum, chunks)
                m_prev = m_scr[mi, rows]
                m_new = jnp.maximum(m_prev, jnp.max(mc, axis=-1, keepdims=True))
                alphas.append(jnp.exp2(m_prev - m_new))
                probs.append(jnp.concatenate([jnp.exp2(sc - m_new).astype(BF16) for sc in chunks], axis=-1))
                m_scr[mi, rows] = m_new
            pv = jnp.dot(jnp.concatenate(probs, axis=0), v_ext[0:n_chunks * HEAD], preferred_element_type=F32)
            for mi, alpha in enumerate(alphas):
                acc_scr[mi, rows] = (jnp.concatenate([alpha, alpha], axis=-1) * acc_scr[mi, rows]
                                     + pv[mi * ATT_ROWS:(mi + 1) * ATT_ROWS])

    def finalize():
        lam = _lambda(lq1_ref, lk1_ref, lq2_ref, lk2_ref, lam_init)
        a0 = acc_scr[0]
        a1 = acc_scr[1]
        o = a0[:, :HEAD] / a0[:, HEAD:] - lam * (a1[:, :HEAD] / a1[:, HEAD:])
        o_ref[...] = (_rms(o, sub_ref[...]) * (1.0 - lam_init)).astype(o_ref.dtype)

    @pl.when(kind == ATT_FULL)
    def _():
        sweep(ATT_FULL)

    if tk > tq:
        @pl.when(kind == ATT_FULL_DIAG)
        def _():
            sweep(ATT_FULL_DIAG)
            finalize()

    @pl.when(kind == ATT_DIAG)
    def _():
        sweep(ATT_DIAG)
        finalize()


def _attn_prompt(aq_bf, ak_bf, av_bf, lq1, lk1, lq2, lk2, subln, lam_init, batch, seq):
    t = ATT_BLOCK
    ratio = ATT_KEY_BLOCK // t
    nq = seq // t
    assert ATT_ROWS % HEAD == 0 and ratio in (1, 2) and nq % ratio == 0
    steps = []
    for i in range(nq):
        for j in range(i // ratio):
            steps.append((i, j, ATT_FULL, int(j == 0)))
        steps.append((i, i // ratio, ATT_FULL_DIAG if i % ratio else ATT_DIAG, int(i < ratio)))
    qi_tab, kb_tab, kind_tab, first_tab = (jnp.asarray([s[c] for s in steps], jnp.int32) for c in range(4))
    q_spec = pl.BlockSpec((t, HEAD), lambda b, h, p, qi, kb, kind, first: (b * nq + qi[p], h))
    kv_spec = pl.BlockSpec((ratio * t, HEAD),
                           lambda b, h, p, qi, kb, kind, first: (b * (nq // ratio) + kb[p], h))
    lam_spec = pl.BlockSpec((1, A_DK), lambda *_: (0, 0))
    sub_spec = pl.BlockSpec((1, HEAD), lambda *_: (0, 0))
    return pl.pallas_call(
        functools.partial(_attn_prompt_kernel, lam_init),
        grid_spec=pltpu.PrefetchScalarGridSpec(
            num_scalar_prefetch=4,
            grid=(batch, N_HEADS, len(steps)),
            in_specs=[q_spec, kv_spec, kv_spec, lam_spec, lam_spec, lam_spec, lam_spec, sub_spec],
            out_specs=q_spec,
            scratch_shapes=[pltpu.VMEM((2, t, HEAD), F32), pltpu.VMEM((2, t, 2 * HEAD), F32)]),
        out_shape=jax.ShapeDtypeStruct((batch * seq, GROUP), BF16),
        compiler_params=pltpu.CompilerParams(
            dimension_semantics=("parallel", "parallel", "arbitrary"), vmem_limit_bytes=VMEM_LIMIT),
        name="attn_prompt",
    )(qi_tab, kb_tab, kind_tab, first_tab, aq_bf, ak_bf, av_bf, lq1, lk1, lq2, lk2, subln)


def _attn_sample_kernel(lam_init, n_pages, pt_ref,
                        q_ref, kn_ref, vn_ref, lq1_ref, lk1_ref, lq2_ref, lk2_ref, sub_ref,
                        ck_hbm, cv_hbm, o_ref, kbuf, vbuf, sem):
    b = pl.program_id(0)
    nb = pl.num_programs(0)

    page_rows = ck_hbm.shape[1]
    page_size = ck_hbm.shape[2]

    def page_copies(bb, slot):
        out = []
        for j in range(n_pages):
            page = pt_ref[bb * n_pages + j]
            out.append(pltpu.make_async_copy(ck_hbm.at[page], kbuf.at[slot, j], sem.at[slot, 0]))
            out.append(pltpu.make_async_copy(cv_hbm.at[page], vbuf.at[slot, pl.ds(j * page_rows, page_rows)],
                                             sem.at[slot, 1]))
        return out

    @pl.when(b == 0)
    def _():
        for cp in page_copies(0, 0):
            cp.start()

    slot = b % 2

    @pl.when(b + 1 < nb)
    def _():
        for cp in page_copies(b + 1, 1 - slot):
            cp.start()

    for cp in page_copies(b, slot):
        cp.wait()

    past = n_pages * page_size
    q = q_ref[0]
    rows = 2 * N_HEADS
    ri = lax.broadcasted_iota(jnp.int32, (rows, GROUP), 0)
    li = lax.broadcasted_iota(jnp.int32, (rows, GROUP), 1)
    mine = (li // A_DK) == (ri % N_HEADS) * 2 + ri // N_HEADS
    qbd = jnp.where(mine, jnp.broadcast_to(q.astype(F32), (rows, GROUP)), 0.0)
    qbd_bf = qbd.astype(BF16)
    s = jnp.concatenate(
        [jnp.dot(qbd_bf, kbuf[slot, j].astype(BF16), preferred_element_type=F32) for j in range(n_pages)],
        axis=-1)
    s_new = jnp.sum(qbd * kn_ref[0].astype(F32), axis=-1, keepdims=True)
    m = jnp.maximum(jnp.max(s, axis=-1, keepdims=True), s_new)
    p = jnp.exp2(s - m)
    p_new = jnp.exp2(s_new - m)
    inv_l = 1.0 / (jnp.sum(p, axis=-1, keepdims=True) + p_new)
    lam = _lambda(lq1_ref, lk1_ref, lq2_ref, lk2_ref, lam_init)
    pn = p * inv_l
    pn_new = p_new * inv_l
    attn = pn - lam * pltpu.roll(pn, N_HEADS, 0)
    attn_new = pn_new - lam * pltpu.roll(pn_new, N_HEADS, 0)
    attn_bf = attn.astype(BF16)
    o = jnp.concatenate(
        [jnp.dot(attn_bf, vbuf[slot, pl.ds(h, past, stride=N_HEADS), :].astype(BF16), preferred_element_type=F32)
         for h in range(N_HEADS)], axis=-1) + attn_new * vn_ref[0]
    keep = jnp.logical_and(ri < N_HEADS, li // HEAD == ri)
    o = jnp.where(keep, o, 0.0)
    ms = jnp.sum(o * o, axis=-1, keepdims=True) * (1.0 / HEAD)
    o = o * lax.rsqrt(ms + EPS) * sub_ref[...] * (1.0 - lam_init)
    o_ref[0] = jnp.sum(o, axis=0, keepdims=True)


def _attn_sample(aq_bf, ak, av, cache_k, cache_v, page_table, lq1, lk1, lq2, lk2, subln4, lam_init):
    n, n_pages = page_table.shape
    n_phys, page_size = cache_k.shape[0], cache_k.shape[1]
    ck = jnp.transpose(cache_k, (0, 2, 3, 4, 1)).reshape(n_phys, GROUP, page_size)
    cv = cache_v.reshape(n_phys, page_size * N_HEADS, HEAD)
    row_spec = pl.BlockSpec((1, 1, GROUP), lambda b, pt: (b, 0, 0))
    lam_spec = pl.BlockSpec((1, A_DK), lambda b, pt: (0, 0))
    out = pl.pallas_call(
        functools.partial(_attn_sample_kernel, lam_init, n_pages),
        grid_spec=pltpu.PrefetchScalarGridSpec(
            num_scalar_prefetch=1,
            grid=(n,),
            in_specs=[row_spec, row_spec, row_spec, lam_spec, lam_spec, lam_spec, lam_spec,
                      pl.BlockSpec((1, GROUP), lambda b, pt: (0, 0)),
                      pl.BlockSpec(memory_space=pl.ANY), pl.BlockSpec(memory_space=pl.ANY)],
            out_specs=row_spec,
            scratch_shapes=[pltpu.VMEM((2, n_pages, GROUP, page_size), F32),
                            pltpu.VMEM((2, n_pages * page_size * N_HEADS, HEAD), F32),
                            pltpu.SemaphoreType.DMA((2, 2))]),
        out_shape=jax.ShapeDtypeStruct((n, 1, GROUP), F32),
        compiler_params=pltpu.CompilerParams(
            dimension_semantics=("arbitrary",), vmem_limit_bytes=VMEM_LIMIT),
        name="attn_sample",
    )(page_table.reshape(-1), aq_bf.reshape(n, 1, GROUP), ak.reshape(n, 1, GROUP), av.reshape(n, 1, GROUP),
      lq1, lk1, lq2, lk2, subln4, ck, cv)
    return out.reshape(n, GROUP)


def _tail_kernel(x_ref, or_ref, gate_ref, oa_ref, gn_ref, wo_ref, nf_ref, wg_ref, wu_ref, wd_ref, nfin_ref,
                 y_ref, act_scr):
    tm = x_ref.shape[0]
    d_ff = wg_ref.shape[1]
    sub = min(tm, TAIL_ROWS)
    for g in range(tm // sub):
        rows = slice(g * sub, (g + 1) * sub)
        parts = []
        for h in range(N_HEADS):
            cs = slice(h * HEAD, (h + 1) * HEAD)
            parts.append((_rms(or_ref[rows, cs], gn_ref[...]) * gate_ref[rows, cs]).astype(BF16))
        parts.append(oa_ref[rows, :].astype(BF16))
        mix = jnp.concatenate(parts, axis=-1)
        x1 = x_ref[rows, :] + jnp.dot(mix, wo_ref[...], preferred_element_type=F32)
        h2 = _rms(x1, nf_ref[...]).astype(BF16)
        for j in range(d_ff // FF_CHUNK):
            fs = slice(j * FF_CHUNK, (j + 1) * FF_CHUNK)
            gt = jnp.dot(h2, wg_ref[:, fs], preferred_element_type=F32)
            up = jnp.dot(h2, wu_ref[:, fs], preferred_element_type=F32)
            act_scr[rows, fs] = (gt * _sigmoid(gt) * up).astype(BF16)
        y = x1 + jnp.dot(act_scr[rows, :], wd_ref[...], preferred_element_type=F32)
        y_ref[rows, :] = _rms(y, nfin_ref[...])


def _tail(x, o_r, gate, o_a, r_gnorm, wo_bf, norm_ffn, wg_bf, wu_bf, wd_bf, norm_final, tm):
    rows, d = x.shape
    d_ff = wg_bf.shape[1]

    def resident(shape):
        return pl.BlockSpec(shape, lambda i: (0, 0), pipeline_mode=pl.Buffered(1))

    return pl.pallas_call(
        _tail_kernel,
        grid=(rows // tm,),
        in_specs=[pl.BlockSpec((tm, d), lambda i: (i, 0)),
                  pl.BlockSpec((tm, GROUP), lambda i: (i, 0)),
                  pl.BlockSpec((tm, GROUP), lambda i: (i, 0)),
                  pl.BlockSpec((tm, GROUP), lambda i: (i, 0)),
                  _const_spec((1, HEAD)),
                  resident((d, d)),
                  _const_spec((1, d)),
                  resident((d, d_ff)), resident((d, d_ff)), resident((d_ff, d)),
                  _const_spec((1, d))],
        out_specs=pl.BlockSpec((tm, d), lambda i: (i, 0)),
        out_shape=jax.ShapeDtypeStruct((rows, d), F32),
        scratch_shapes=[pltpu.VMEM((tm, d_ff), BF16)],
        compiler_params=pltpu.CompilerParams(
            dimension_semantics=("parallel",), vmem_limit_bytes=VMEM_LIMIT),
        name="tail",
    )(x, o_r, gate, o_a, r_gnorm, wo_bf, norm_ffn, wg_bf, wu_bf, wd_bf, norm_final)


def kernel(x_prompt, x_sample, cache_k, cache_v, state_hgrn, page_table, w_in, w_out, lb_param,
           r_gnorm, lam_q1, lam_k1, lam_q2, lam_k2, a_subln, norm_mix, norm_ffn, w_gate, w_up,
           w_down, norm_final):
    batch, seq, d = x_prompt.shape
    n_dec = x_sample.shape[0]
    depth = w_in.shape[0]
    assert depth == 1 and x_sample.shape[1] == 1
    hp = x_prompt.reshape(batch * seq, d)
    hs = x_sample.reshape(n_dec, d)
    lb_param = lb_param.astype(F32)
    nfin = norm_final.reshape(1, d)
    kp, vp, sp, kss, vss, sss = [], [], [], [], [], []
    for l in range(depth):
        lam_init = 0.8 - 0.6 * math.exp(-0.3 * l)
        w_in_bf = w_in[l].astype(BF16)
        wo_bf = w_out[l].astype(BF16)
        wg_bf = w_gate[l].astype(BF16)
        wu_bf = w_up[l].astype(BF16)
        wd_bf = w_down[l].astype(BF16)
        nmix = norm_mix[l].reshape(1, d)
        nffn = norm_ffn[l].reshape(1, d)
        gn = r_gnorm[l].reshape(1, HEAD)
        sub = a_subln[l].reshape(1, HEAD)
        lams = [a[l].reshape(1, A_DK).astype(F32) for a in (lam_q1, lam_k1, lam_q2, lam_k2)]

        qr, logf, kr, vr, gate, aq_bf, akt, av, ak_bf, av_bf = _inproj(
            hp, nmix, w_in_bf, lb_param, l, ROW_BLOCK, seq)
        o_r, s_new = _hgrn_prompt(qr, logf, kr, vr, batch, seq)
        o_a = _attn_prompt(aq_bf, ak_bf, av_bf, *lams, sub, lam_init, batch, seq)
        hp = _tail(hp, o_r, gate, o_a, gn, wo_bf, nffn, wg_bf, wu_bf, wd_bf, nfin, TAIL_BLOCK)
        kp.append(jnp.transpose(akt.reshape(batch, N_HEADS, 2, A_DK, seq), (0, 4, 1, 2, 3)))
        vp.append(av.reshape(batch, seq, N_HEADS, HEAD))
        sp.append(s_new)

        qr, logf, kr, vr, gate, aq_bf, akt, av, ak_bf, _ = _inproj(hs, nmix, w_in_bf, lb_param, l, n_dec, n_dec)
        o_r, s_new = _hgrn_sample(qr, logf, kr, vr, state_hgrn[l].astype(F32))
        o_a = _attn_sample(aq_bf, ak_bf, av, cache_k[l], cache_v[l], page_table, *lams,
                           jnp.tile(sub, (1, N_HEADS)), lam_init)
        hs = _tail(hs, o_r, gate, o_a, gn, wo_bf, nffn, wg_bf, wu_bf, wd_bf, nfin, n_dec)
        kss.append(jnp.transpose(akt.reshape(N_HEADS, 2, A_DK, n_dec), (3, 0, 1, 2))[:, None])
        vss.append(av.reshape(n_dec, 1, N_HEADS, HEAD))
        sss.append(s_new)

    y_prompt = hp.reshape(batch, seq, d)
    y_sample = hs.reshape(n_dec, 1, d)
    return (y_prompt, y_sample, jnp.stack(kp), jnp.stack(vp), jnp.stack(sp),
            jnp.stack(kss), jnp.stack(vss), jnp.stack(sss))
```

```python
import functools
import math

import jax
import jax.numpy as jnp
from jax import lax
from jax.experimental import pallas as pl
from jax.experimental.pallas import tpu as pltpu

F32 = jnp.float32
BF16 = jnp.bfloat16

EPS = 1e-6
LOG2E = math.log2(math.e)
HEAD = 128
N_HEADS = 4
A_DK = 64
GROUP = N_HEADS * HEAD
N_PIECES = 7
SUBLANES = 8
HGRN_CHUNK = 128
HGRN_BLOCK = 256
HGRN_SAFE_EXPONENT = 60.0
ATT_BLOCK = 2048
ATT_KEY_BLOCK = 2048
ATT_ROWS = 128
ROW_BLOCK = 256
TAIL_BLOCK = 256
TAIL_ROWS = 256
TAIL_VMEM_LIMIT = 56 * 1024 * 1024
FF_CHUNK = 256
VMEM_LIMIT = 48 * 1024 * 1024


def _sigmoid(x):
    return 1.0 / (1.0 + jnp.exp(-x))


def _rms(x, w):
    return x * lax.rsqrt(jnp.mean(x * x, axis=-1, keepdims=True) + EPS) * w


def _const_spec(shape):
    return pl.BlockSpec(shape, lambda *_: (0,) * len(shape))


def _inproj_kernel(layer, x_ref, nw_ref, w_ref, lb_ref,
                   qr_ref, logf_ref, kr_ref, vr_ref, gate_ref,
                   aq_ref, akt_ref, av_ref, akb_ref, avb_ref):
    h = _rms(x_ref[...], nw_ref[...]).astype(BF16)

    def piece(j):
        return jnp.dot(h, w_ref[:, j * GROUP:(j + 1) * GROUP], preferred_element_type=F32)

    lbp = lb_ref[...]
    e = jnp.exp(lbp - jnp.max(lbp, axis=0, keepdims=True))
    lb = jnp.sum(e[:layer + 1], axis=0, keepdims=True) / jnp.sum(e, axis=0, keepdims=True)

    rq = piece(0)
    qr_ref[...] = rq * _sigmoid(rq)
    f = lb + (1.0 - lb) * _sigmoid(piece(1))
    logf_ref[...] = jnp.log(f)
    kr_ref[...] = 1.0 - f
    vr_ref[...] = piece(2)
    rg = piece(3)
    gate_ref[...] = rg * _sigmoid(rg)
    aq_ref[...] = (piece(4) * (A_DK ** -0.5 * LOG2E)).astype(BF16)
    ak = piece(5)
    akt_ref[0] = ak.T
    akb_ref[...] = ak.astype(BF16)
    av = piece(6)
    av_ref[...] = av
    avb_ref[...] = av.astype(BF16)


def _inproj(x, norm_w, w_bf, lb_param, layer, tm, seq):
    rows, d = x.shape
    d_in = w_bf.shape[1]
    per_seq = seq // tm
    f32_out = jax.ShapeDtypeStruct((rows, GROUP), F32)
    bf_out = jax.ShapeDtypeStruct((rows, GROUP), BF16)
    row_spec = pl.BlockSpec((tm, GROUP), lambda i: (i, 0))
    kt_out = jax.ShapeDtypeStruct((rows // seq, GROUP, seq), F32)
    kt_spec = pl.BlockSpec((1, GROUP, tm), lambda i: (i // per_seq, 0, i % per_seq))
    return pl.pallas_call(
        functools.partial(_inproj_kernel, layer),
        grid=(rows // tm,),
        in_specs=[pl.BlockSpec((tm, d), lambda i: (i, 0)),
                  _const_spec((1, d)),
                  _const_spec((d, d_in)),
                  _const_spec(lb_param.shape)],
        out_specs=[row_spec] * 6 + [kt_spec] + [row_spec] * 3,
        out_shape=[f32_out] * 5 + [bf_out, kt_out, f32_out, bf_out, bf_out],
        compiler_params=pltpu.CompilerParams(
            dimension_semantics=("parallel",), vmem_limit_bytes=VMEM_LIMIT),
        name="inproj",
    )(x, norm_w, w_bf, lb_param)


def _cumsum_rows(x):
    n, lanes = x.shape
    groups = n // SUBLANES
    x3 = x.reshape(groups, SUBLANES, lanes)
    sub = lax.broadcasted_iota(jnp.int32, x3.shape, 1)
    shift = 1
    while shift < SUBLANES:
        x3 = x3 + jnp.where(sub >= shift, pltpu.roll(x3, shift, 1), 0.0)
        shift *= 2
    totals = x3[:, SUBLANES - 1:SUBLANES, :]
    pre = totals
    shift = 1
    while shift < groups:
        pre = pre + jnp.concatenate([jnp.zeros((shift, 1, lanes), F32), pre[:groups - shift]], axis=0)
        shift *= 2
    return (x3 + (pre - totals)).reshape(n, lanes)


def _hgrn_prompt_kernel(q_ref, g_ref, k_ref, v_ref, o_ref, s_ref, st_scr, b_scr, oi_scr):
    i = pl.program_id(0)

    @pl.when(i == 0)
    def _():
        st_scr[...] = jnp.zeros_like(st_scr)

    c = HGRN_CHUNK
    half = c // 2
    row = lax.broadcasted_iota(jnp.int32, (c, c), 0)
    col = lax.broadcasted_iota(jnp.int32, (c, c), 1)
    tri = col <= row
    nt = (((1,), (1,)), ((), ()))
    tn = (((0,), (0,)), ((), ()))
    chains = [(bi, ci, h) for bi in range(q_ref.shape[0]) for ci in range(HGRN_BLOCK // c)
              for h in range(N_HEADS)]

    def window(bi, ci, h):
        return bi, slice(ci * c, (ci + 1) * c), slice(h * HEAD, (h + 1) * HEAD)

    worst = jnp.zeros((1, HEAD), F32)
    for chain in chains:
        w = window(*chain)
        b = _cumsum_rows(g_ref[w])
        b_scr[w] = b
        b_mid = b[half - 1:half, :]
        worst = jnp.maximum(worst, jnp.maximum(-b_mid, b_mid - b[c - 1:c, :]))
    safe = jnp.max(worst) < HGRN_SAFE_EXPONENT

    def finish(chain, intra, q0, k2, v, b_last):
        bi, _, h = chain
        st = st_scr[bi, h]
        o_ref[window(*chain)] = lax.dot_general(q0, st.astype(BF16), nt, preferred_element_type=F32) + intra
        st_scr[bi, h] = st * jnp.exp(b_last) + lax.dot_general(v, k2, tn, preferred_element_type=F32)

    @pl.when(safe)
    def _():
        for chain in chains:
            w = window(*chain)
            b = b_scr[w]
            b_mid = b[half - 1:half, :]
            b_last = b[c - 1:c, :]
            d = b - b_mid
            qm = q_ref[w] * jnp.exp(d)
            km = k_ref[w] * jnp.exp(-d)
            q0 = (qm * jnp.exp(b_mid)).astype(BF16)
            k2 = (km * jnp.exp(b_last - b_mid)).astype(BF16)
            v = v_ref[w].astype(BF16)
            a = lax.dot_general(qm.astype(BF16), km.astype(BF16), nt, preferred_element_type=F32)
            a = jnp.where(tri, a, 0.0).astype(BF16)
            finish(chain, jnp.dot(a, v, preferred_element_type=F32), q0, k2, v, b_last)

    @pl.when(jnp.logical_not(safe))
    def _():
        srow = lax.broadcasted_iota(jnp.int32, (c, HEAD), 0)
        for chain in chains:
            bi, rs, cs = window(*chain)
            b = b_scr[bi, rs, cs]
            b_last = b[c - 1:c, :]
            kk = k_ref[bi, rs, cs]
            vv = v_ref[bi, rs, cs]

            def query_rows(tg, carry, bi=bi, rs=rs, cs=cs, b=b, kk=kk, vv=vv):
                base = pl.multiple_of(tg * SUBLANES, SUBLANES)
                at = pl.ds(rs.start + base, SUBLANES)
                b8 = b_scr[bi, at, cs]
                q8 = q_ref[bi, at, cs]
                out = []
                for j in range(SUBLANES):
                    decay = jnp.exp(jnp.where(srow <= base + j, b8[j:j + 1, :] - b, -jnp.inf))
                    a_col = jnp.sum(q8[j:j + 1, :] * kk * decay, axis=-1, keepdims=True)
                    out.append(jnp.sum(a_col * vv, axis=0, keepdims=True))
                oi_scr[pl.ds(base, SUBLANES), :] = jnp.concatenate(out, axis=0)
                return carry

            lax.fori_loop(0, c // SUBLANES, query_rows, 0)
            q0 = (q_ref[bi, rs, cs] * jnp.exp(b)).astype(BF16)
            k2 = (kk * jnp.exp(b_last - b)).astype(BF16)
            finish(chain, oi_scr[...], q0, k2, vv.astype(BF16), b_last)

    @pl.when(i == pl.num_programs(0) - 1)
    def _():
        for bi in range(q_ref.shape[0]):
            for h in range(N_HEADS):
                s_ref[bi, h] = st_scr[bi, h].T


def _hgrn_prompt(qr, logf, kr, vr, batch, seq):
    nb = seq // HGRN_BLOCK
    spec = pl.BlockSpec((batch, HGRN_BLOCK, GROUP), lambda i: (0, i, 0))
    shape3 = (batch, seq, GROUP)
    o_r, s_new = pl.pallas_call(
        _hgrn_prompt_kernel,
        grid=(nb,),
        in_specs=[spec] * 4,
        out_specs=[spec, pl.BlockSpec((batch, N_HEADS, HEAD, HEAD), lambda i: (0, 0, 0, 0))],
        out_shape=[jax.ShapeDtypeStruct(shape3, F32),
                   jax.ShapeDtypeStruct((batch, N_HEADS, HEAD, HEAD), F32)],
        scratch_shapes=[pltpu.VMEM((batch, N_HEADS, HEAD, HEAD), F32),
                        pltpu.VMEM((batch, HGRN_BLOCK, GROUP), F32),
                        pltpu.VMEM((HGRN_CHUNK, HEAD), F32)],
        compiler_params=pltpu.CompilerParams(
            dimension_semantics=("arbitrary",), vmem_limit_bytes=VMEM_LIMIT),
        name="hgrn_prompt",
    )(*(a.reshape(shape3) for a in (qr, logf, kr, vr)))
    return o_r.reshape(batch * seq, GROUP), s_new


def _hgrn_sample_kernel(q_ref, g_ref, k_ref, v_ref, s_in_ref, o_ref, s_out_ref):
    nb = q_ref.shape[0]
    for h in range(N_HEADS):
        cs = slice(h * HEAD, (h + 1) * HEAD)
        qt = q_ref[:, cs].T
        ft = jnp.exp(g_ref[:, cs]).T
        kt = k_ref[:, cs].T
        for j in range(nb):
            s_new = ft[:, j:j + 1] * s_in_ref[j, h] + kt[:, j:j + 1] * v_ref[j:j + 1, cs]
            s_out_ref[j, h] = s_new
            o_ref[j:j + 1, cs] = jnp.sum(qt[:, j:j + 1] * s_new, axis=0, keepdims=True)


def _hgrn_sample(qr, logf, kr, vr, state):
    n = qr.shape[0]
    nb = 8
    row_spec = pl.BlockSpec((nb, GROUP), lambda i: (i, 0))
    st_spec = pl.BlockSpec((nb, N_HEADS, HEAD, HEAD), lambda i: (i, 0, 0, 0))
    return pl.pallas_call(
        _hgrn_sample_kernel,
        grid=(n // nb,),
        in_specs=[row_spec] * 4 + [st_spec],
        out_specs=[row_spec, st_spec],
        out_shape=[jax.ShapeDtypeStruct((n, GROUP), F32),
                   jax.ShapeDtypeStruct(state.shape, F32)],
        compiler_params=pltpu.CompilerParams(
            dimension_semantics=("parallel",), vmem_limit_bytes=VMEM_LIMIT),
        name="hgrn_sample",
    )(qr, logf, kr, vr, state)


def _lambda(lq1_ref, lk1_ref, lq2_ref, lk2_ref, lam_init):
    a = jnp.sum(lq1_ref[...] * lk1_ref[...], axis=-1, keepdims=True)
    b = jnp.sum(lq2_ref[...] * lk2_ref[...], axis=-1, keepdims=True)
    return jnp.exp(a) - jnp.exp(b) + lam_init


ATT_FULL, ATT_FULL_DIAG, ATT_DIAG = 0, 1, 2


def _attn_prompt_kernel(lam_init, qi_tab, kb_tab, kind_tab, first_tab,
                        q_ref, k_ref, v_ref, lq1_ref, lk1_ref, lq2_ref, lk2_ref, sub_ref,
                        o_ref, m_scr, acc_scr):
    p = pl.program_id(2)
    kind = kind_tab[p]

    @pl.when(first_tab[p] == 1)
    def _():
        m_scr[...] = jnp.full_like(m_scr, -jnp.inf)
        acc_scr[...] = jnp.zeros_like(acc_scr)

    tq, tk = q_ref.shape[0], k_ref.shape[0]
    half_chunks = tq // HEAD
    nt = (((1,), (1,)), ((), ()))

    def sweep(step_kind):
        masked = step_kind != ATT_FULL
        v_ext = jnp.concatenate([v_ref[...], jnp.ones((tk, HEAD), BF16)], axis=-1)
        lane = lax.broadcasted_iota(jnp.int32, (ATT_ROWS, HEAD), 1)
        row = lax.broadcasted_iota(jnp.int32, (ATT_ROWS, HEAD), 0)
        zero = jnp.zeros((ATT_ROWS, HEAD), BF16)
        group_chunks = ATT_ROWS // HEAD
        for r in range(tq // ATT_ROWS):
            rows = slice(r * ATT_ROWS, (r + 1) * ATT_ROWS)
            q = q_ref[rows, :]
            diag_chunks = (r + 1) * group_chunks
            n_chunks = {ATT_FULL: tk // HEAD, ATT_FULL_DIAG: half_chunks + diag_chunks,
                        ATT_DIAG: diag_chunks}[step_kind]
            k = k_ref[0:n_chunks * HEAD, :]
            probs, alphas = [], []
            for mi, qm in enumerate((jnp.where(lane < A_DK, q, zero), jnp.where(lane >= A_DK, q, zero))):
                s = lax.dot_general(qm, k, nt, preferred_element_type=F32)
                chunks = [s[:, c * HEAD:(c + 1) * HEAD] for c in range(n_chunks)]
                if masked:
                    for g in range(group_chunks):
                        c = n_chunks - group_chunks + g
                        chunks[c] = jnp.where(lane + g * HEAD <= row, chunks[c], -jnp.inf)
                mc = functools.reduce(jnp.maximum, chunks)
                m_prev = m_scr[mi, rows]
                m_new = jnp.maximum(m_prev, jnp.max(mc, axis=-1, keepdims=True))
                alphas.append(jnp.exp2(m_prev - m_new))
                probs.append(jnp.concatenate([jnp.exp2(sc - m_new).astype(BF16) for sc in chunks], axis=-1))
                m_scr[mi, rows] = m_new
            pv = jnp.dot(jnp.concatenate(probs, axis=0), v_ext[0:n_chunks * HEAD], preferred_element_type=F32)
            for mi, alpha in enumerate(alphas):
                acc_scr[mi, rows] = (jnp.concatenate([alpha, alpha], axis=-1) * acc_scr[mi, rows]
                                     + pv[mi * ATT_ROWS:(mi + 1) * ATT_ROWS])

    def finalize():
        lam = _lambda(lq1_ref, lk1_ref, lq2_ref, lk2_ref, lam_init)
        a0 = acc_scr[0]
        a1 = acc_scr[1]
        o = a0[:, :HEAD] / a0[:, HEAD:] - lam * (a1[:, :HEAD] / a1[:, HEAD:])
        o_ref[...] = (_rms(o, sub_ref[...]) * (1.0 - lam_init)).astype(o_ref.dtype)

    @pl.when(kind == ATT_FULL)
    def _():
        sweep(ATT_FULL)

    if tk > tq:
        @pl.when(kind == ATT_FULL_DIAG)
        def _():
            sweep(ATT_FULL_DIAG)
            finalize()

    @pl.when(kind == ATT_DIAG)
    def _():
        sweep(ATT_DIAG)
        finalize()


def _attn_prompt(aq_bf, ak_bf, av_bf, lq1, lk1, lq2, lk2, subln, lam_init, batch, seq):
    t = ATT_BLOCK
    ratio = ATT_KEY_BLOCK // t
    nq = seq // t
    assert ATT_ROWS % HEAD == 0 and ratio in (1, 2) and nq % ratio == 0
    steps = []
    for i in range(nq):
        for j in range(i // ratio):
            steps.append((i, j, ATT_FULL, int(j == 0)))
        steps.append((i, i // ratio, ATT_FULL_DIAG if i % ratio else ATT_DIAG, int(i < ratio)))
    qi_tab, kb_tab, kind_tab, first_tab = (jnp.asarray([s[c] for s in steps], jnp.int32) for c in range(4))
    q_spec = pl.BlockSpec((t, HEAD), lambda b, h, p, qi, kb, kind, first: (b * nq + qi[p], h))
    kv_spec = pl.BlockSpec((ratio * t, HEAD),
                           lambda b, h, p, qi, kb, kind, first: (b * (nq // ratio) + kb[p], h))
    lam_spec = pl.BlockSpec((1, A_DK), lambda *_: (0, 0))
    sub_spec = pl.BlockSpec((1, HEAD), lambda *_: (0, 0))
    return pl.pallas_call(
        functools.partial(_attn_prompt_kernel, lam_init),
        grid_spec=pltpu.PrefetchScalarGridSpec(
            num_scalar_prefetch=4,
            grid=(batch, N_HEADS, len(steps)),
            in_specs=[q_spec, kv_spec, kv_spec, lam_spec, lam_spec, lam_spec, lam_spec, sub_spec],
            out_specs=q_spec,
            scratch_shapes=[pltpu.VMEM((2, t, HEAD), F32), pltpu.VMEM((2, t, 2 * HEAD), F32)]),
        out_shape=jax.ShapeDtypeStruct((batch * seq, GROUP), BF16),
        compiler_params=pltpu.CompilerParams(
            dimension_semantics=("parallel", "parallel", "arbitrary"), vmem_limit_bytes=VMEM_LIMIT),
        name="attn_prompt",
    )(qi_tab, kb_tab, kind_tab, first_tab, aq_bf, ak_bf, av_bf, lq1, lk1, lq2, lk2, subln)


def _page_copies(pt_ref, item, n_pages, ck_hbm, cv_hbm, kbuf, vbuf, sem, slot):
    page_rows = ck_hbm.shape[1]
    out = []
    for j in range(n_pages):
        page = pt_ref[item * n_pages + j]
        out.append(pltpu.make_async_copy(ck_hbm.at[page], kbuf.at[slot, j], sem.at[slot, 0]))
        out.append(pltpu.make_async_copy(cv_hbm.at[page], vbuf.at[slot, pl.ds(j * page_rows, page_rows)],
                                         sem.at[slot, 1]))
    return out


def _decode_item(lam, lam_init, q, kn, vn, sub4, kpages, vrows):
    n_pages = kpages.shape[0]
    past = n_pages * kpages.shape[2]
    rows = 2 * N_HEADS
    ri = lax.broadcasted_iota(jnp.int32, (rows, GROUP), 0)
    li = lax.broadcasted_iota(jnp.int32, (rows, GROUP), 1)
    mine = (li // A_DK) == (ri % N_HEADS) * 2 + ri // N_HEADS
    qbd = jnp.where(mine, jnp.broadcast_to(q.astype(F32), (rows, GROUP)), 0.0)
    qbd_bf = qbd.astype(BF16)
    s = jnp.concatenate(
        [jnp.dot(qbd_bf, kpages[j].astype(BF16), preferred_element_type=F32) for j in range(n_pages)],
        axis=-1)
    s_new = jnp.sum(qbd * kn.astype(F32), axis=-1, keepdims=True)
    m = jnp.maximum(jnp.max(s, axis=-1, keepdims=True), s_new)
    p = jnp.exp2(s - m)
    p_new = jnp.exp2(s_new - m)
    inv_l = 1.0 / (jnp.sum(p, axis=-1, keepdims=True) + p_new)
    pn = p * inv_l
    pn_new = p_new * inv_l
    attn = pn - lam * pltpu.roll(pn, N_HEADS, 0)
    attn_new = pn_new - lam * pltpu.roll(pn_new, N_HEADS, 0)
    attn_bf = attn.astype(BF16)
    o = jnp.concatenate(
        [jnp.dot(attn_bf, vrows[pl.ds(h, past, stride=N_HEADS), :].astype(BF16), preferred_element_type=F32)
         for h in range(N_HEADS)], axis=-1) + attn_new * vn
    keep = jnp.logical_and(ri < N_HEADS, li // HEAD == ri)
    o = jnp.where(keep, o, 0.0)
    ms = jnp.sum(o * o, axis=-1, keepdims=True) * (1.0 / HEAD)
    o = o * lax.rsqrt(ms + EPS) * sub4 * (1.0 - lam_init)
    return jnp.sum(o, axis=0, keepdims=True)


def _tail_kernel(decode, *refs):
    if decode is None:
        (x_ref, or_ref, gate_ref, oa_ref, gn_ref, wo_ref, nf_ref, wg_ref, wu_ref, wd_ref, nfin_ref,
         y_ref, act_scr) = refs
    else:
        (pt_ref, x_ref, or_ref, gate_ref, oa_ref, gn_ref, wo_ref, nf_ref, wg_ref, wu_ref, wd_ref, nfin_ref,
         q_ref, kn_ref, vn_ref, lq1_ref, lk1_ref, lq2_ref, lk2_ref, sub4_ref, ck_hbm, cv_hbm,
         y_ref, os_ref, act_scr, kbuf, vbuf, sem) = refs
        lam_init, n_pages, items = decode
        step = pl.program_id(0)
        copies = functools.partial(_page_copies, pt_ref, n_pages=n_pages, ck_hbm=ck_hbm, cv_hbm=cv_hbm,
                                   kbuf=kbuf, vbuf=vbuf, sem=sem)

        @pl.when(step == 0)
        def _():
            for cp in copies(item=0, slot=0):
                cp.start()

    def decode_item(jj):
        if decode is None:
            return
        item = step * items + jj
        slot = jj % 2
        if jj + 1 < items:
            for cp in copies(item=item + 1, slot=1 - slot):
                cp.start()
        else:
            @pl.when(step + 1 < pl.num_programs(0))
            def _():
                for cp in copies(item=item + 1, slot=1 - slot):
                    cp.start()
        for cp in copies(item=item, slot=slot):
            cp.wait()
        lam = _lambda(lq1_ref, lk1_ref, lq2_ref, lk2_ref, lam_init)
        os_ref[jj] = _decode_item(lam, lam_init, q_ref[jj], kn_ref[jj], vn_ref[jj], sub4_ref[...],
                                  kbuf.at[slot], vbuf.at[slot])

    tm = x_ref.shape[0]
    d_ff = wg_ref.shape[1]
    sub = min(tm, TAIL_ROWS)
    groups = tm // sub
    per_group = 0 if decode is None else items // (2 * groups)
    for g in range(groups):
        rows = slice(g * sub, (g + 1) * sub)
        for jj in range(per_group):
            decode_item(2 * g * per_group + jj)
        parts = []
        for h in range(N_HEADS):
            cs = slice(h * HEAD, (h + 1) * HEAD)
            parts.append((_rms(or_ref[rows, cs], gn_ref[...]) * gate_ref[rows, cs]).astype(BF16))
        parts.append(oa_ref[rows, :].astype(BF16))
        mix = jnp.concatenate(parts, axis=-1)
        x1 = x_ref[rows, :] + jnp.dot(mix, wo_ref[...], preferred_element_type=F32)
        h2 = _rms(x1, nf_ref[...]).astype(BF16)
        for j in range(d_ff // FF_CHUNK):
            fs = slice(j * FF_CHUNK, (j + 1) * FF_CHUNK)
            gt = jnp.dot(h2, wg_ref[:, fs], preferred_element_type=F32)
            up = jnp.dot(h2, wu_ref[:, fs], preferred_element_type=F32)
            act_scr[rows, fs] = (gt * _sigmoid(gt) * up).astype(BF16)
        for jj in range(per_group):
            decode_item((2 * g + 1) * per_group + jj)
        y = x1 + jnp.dot(act_scr[rows, :], wd_ref[...], preferred_element_type=F32)
        y_ref[rows, :] = _rms(y, nfin_ref[...])


def _tail(x, o_r, gate, o_a, r_gnorm, wo_bf, norm_ffn, wg_bf, wu_bf, wd_bf, norm_final, tm, decode=None):
    rows, d = x.shape
    d_ff = wg_bf.shape[1]
    steps = rows // tm

    def resident(shape):
        return pl.BlockSpec(shape, lambda i, *_: (0, 0), pipeline_mode=pl.Buffered(1))

    def const(shape):
        return pl.BlockSpec(shape, lambda i, *_: (0,) * len(shape))

    def row_block(width):
        return pl.BlockSpec((tm, width), lambda i, *_: (i, 0))

    in_specs = [row_block(d), row_block(GROUP), row_block(GROUP), row_block(GROUP), const((1, HEAD)),
                resident((d, d)), const((1, d)), resident((d, d_ff)), resident((d, d_ff)), resident((d_ff, d)),
                const((1, d))]
    operands = [x, o_r, gate, o_a, r_gnorm, wo_bf, norm_ffn, wg_bf, wu_bf, wd_bf, norm_final]
    out_specs = [row_block(d)]
    out_shape = [jax.ShapeDtypeStruct((rows, d), F32)]
    scratch = [pltpu.VMEM((tm, d_ff), BF16)]
    config = None
    prefetch = []
    if decode is not None:
        lam_init, page_table, aq_bf, ak_bf, av, lams, subln4, cache_k, cache_v = decode
        n, n_pages = page_table.shape
        n_phys, page_size = cache_k.shape[0], cache_k.shape[1]
        items = n // steps
        assert items * steps == n and items % (2 * (tm // min(tm, TAIL_ROWS))) == 0
        ck = jnp.transpose(cache_k, (0, 2, 3, 4, 1)).reshape(n_phys, GROUP, page_size)
        cv = cache_v.reshape(n_phys, page_size * N_HEADS, HEAD)
        item_block = pl.BlockSpec((items, 1, GROUP), lambda i, *_: (i, 0, 0))
        in_specs += [item_block, item_block, item_block] + [const((1, A_DK))] * 4 + [
            const((1, GROUP)), pl.BlockSpec(memory_space=pl.ANY), pl.BlockSpec(memory_space=pl.ANY)]
        operands += [a.reshape(n, 1, GROUP) for a in (aq_bf, ak_bf, av)] + list(lams) + [subln4, ck, cv]
        out_specs.append(item_block)
        out_shape.append(jax.ShapeDtypeStruct((n, 1, GROUP), F32))
        scratch += [pltpu.VMEM((2, n_pages, GROUP, page_size), F32),
                    pltpu.VMEM((2, n_pages * page_size * N_HEADS, HEAD), F32),
                    pltpu.SemaphoreType.DMA((2, 2))]
        config = (lam_init, n_pages, items)
        prefetch = [page_table.reshape(-1)]
    out = pl.pallas_call(
        functools.partial(_tail_kernel, config),
        grid_spec=pltpu.PrefetchScalarGridSpec(
            num_scalar_prefetch=len(prefetch), grid=(steps,),
            in_specs=in_specs, out_specs=out_specs, scratch_shapes=scratch),
        out_shape=out_shape,
        compiler_params=pltpu.CompilerParams(
            dimension_semantics=("arbitrary",), vmem_limit_bytes=TAIL_VMEM_LIMIT),
        name="tail",
    )(*prefetch, *operands)
    if decode is None:
        return out[0]
    return out[0], out[1].reshape(n, GROUP)


def kernel(x_prompt, x_sample, cache_k, cache_v, state_hgrn, page_table, w_in, w_out, lb_param,
           r_gnorm, lam_q1, lam_k1, lam_q2, lam_k2, a_subln, norm_mix, norm_ffn, w_gate, w_up,
           w_down, norm_final):
    batch, seq, d = x_prompt.shape
    n_dec = x_sample.shape[0]
    depth = w_in.shape[0]
    assert depth == 1 and x_sample.shape[1] == 1
    hp = x_prompt.reshape(batch * seq, d)
    hs = x_sample.reshape(n_dec, d)
    lb_param = lb_param.astype(F32)
    nfin = norm_final.reshape(1, d)
    kp, vp, sp, kss, vss, sss = [], [], [], [], [], []
    for l in range(depth):
        lam_init = 0.8 - 0.6 * math.exp(-0.3 * l)
        w_in_bf = w_in[l].astype(BF16)
        wo_bf = w_out[l].astype(BF16)
        wg_bf = w_gate[l].astype(BF16)
        wu_bf = w_up[l].astype(BF16)
        wd_bf = w_down[l].astype(BF16)
        nmix = norm_mix[l].reshape(1, d)
        nffn = norm_ffn[l].reshape(1, d)
        gn = r_gnorm[l].reshape(1, HEAD)
        sub = a_subln[l].reshape(1, HEAD)
        lams = [a[l].reshape(1, A_DK).astype(F32) for a in (lam_q1, lam_k1, lam_q2, lam_k2)]

        qr_s, logf_s, kr_s, vr_s, gate_s, aq_s, akt_s, av_s, ak_s, _ = _inproj(
            hs, nmix, w_in_bf, lb_param, l, n_dec, n_dec)
        o_r_s, s_new_s = _hgrn_sample(qr_s, logf_s, kr_s, vr_s, state_hgrn[l].astype(F32))

        qr, logf, kr, vr, gate, aq_bf, akt, av, ak_bf, av_bf = _inproj(
            hp, nmix, w_in_bf, lb_param, l, ROW_BLOCK, seq)
        o_r, s_new = _hgrn_prompt(qr, logf, kr, vr, batch, seq)
        o_a = _attn_prompt(aq_bf, ak_bf, av_bf, *lams, sub, lam_init, batch, seq)
        paged = (lam_init, page_table, aq_s, ak_s, av_s, lams, jnp.tile(sub, (1, N_HEADS)),
                 cache_k[l], cache_v[l])
        hp, o_a_s = _tail(hp, o_r, gate, o_a, gn, wo_bf, nffn, wg_bf, wu_bf, wd_bf, nfin, TAIL_BLOCK, paged)
        kp.append(jnp.transpose(akt.reshape(batch, N_HEADS, 2, A_DK, seq), (0, 4, 1, 2, 3)))
        vp.append(av.reshape(batch, seq, N_HEADS, HEAD))
        sp.append(s_new)

        hs = _tail(hs, o_r_s, gate_s, o_a_s, gn, wo_bf, nffn, wg_bf, wu_bf, wd_bf, nfin, n_dec)
        kss.append(jnp.transpose(akt_s.reshape(N_HEADS, 2, A_DK, n_dec), (3, 0, 1, 2))[:, None])
        vss.append(av_s.reshape(n_dec, 1, N_HEADS, HEAD))
        sss.append(s_new_s)

    y_prompt = hp.reshape(batch, seq, d)
    y_sample = hs.reshape(n_dec, 1, d)
    return (y_prompt, y_sample, jnp.stack(kp), jnp.stack(vp), jnp.stack(sp),
            jnp.stack(kss), jnp.stack(vss), jnp.stack(sss))
```

```python
import functools
import math

import jax
import jax.numpy as jnp
from jax import lax
from jax.experimental import pallas as pl
from jax.experimental.pallas import tpu as pltpu

F32 = jnp.float32
BF16 = jnp.bfloat16

EPS = 1e-6
LOG2E = math.log2(math.e)
HEAD = 128
N_HEADS = 4
A_DK = 64
GROUP = N_HEADS * HEAD
N_PIECES = 7
SUBLANES = 8
HGRN_CHUNK = 128
HGRN_BLOCK = 256
HGRN_SAFE_EXPONENT = 60.0
ATT_BLOCK = 2048
ATT_KEY_BLOCK = 2048
ATT_ROWS = 128
ROW_BLOCK = 256
TAIL_BLOCK = 256
TAIL_ROWS = 256
TAIL_VMEM_LIMIT = 56 * 1024 * 1024
FF_CHUNK = 256
VMEM_LIMIT = 48 * 1024 * 1024


def _sigmoid(x):
    return 1.0 / (1.0 + jnp.exp(-x))


def _rms(x, w):
    return x * lax.rsqrt(jnp.mean(x * x, axis=-1, keepdims=True) + EPS) * w


def _const_spec(shape):
    return pl.BlockSpec(shape, lambda *_: (0,) * len(shape))


def _inproj_kernel(layer, prompt, x_ref, nw_ref, w_ref, lb_ref,
                   qr_ref, logf_ref, kr_ref, vr_ref, gate_ref, aq_ref, akt_ref, *kv_refs):
    h = _rms(x_ref[...], nw_ref[...]).astype(BF16)

    def piece(j):
        return jnp.dot(h, w_ref[:, j * GROUP:(j + 1) * GROUP], preferred_element_type=F32)

    lbp = lb_ref[...]
    e = jnp.exp(lbp - jnp.max(lbp, axis=0, keepdims=True))
    lb = jnp.sum(e[:layer + 1], axis=0, keepdims=True) / jnp.sum(e, axis=0, keepdims=True)

    rq = piece(0)
    qr_ref[...] = rq * _sigmoid(rq)
    f = lb + (1.0 - lb) * _sigmoid(piece(1))
    logf_ref[...] = jnp.log(f)
    kr_ref[...] = 1.0 - f
    vr_ref[...] = piece(2).astype(vr_ref.dtype)
    rg = piece(3)
    gate_ref[...] = rg * _sigmoid(rg)
    aq_ref[...] = (piece(4) * (A_DK ** -0.5 * LOG2E)).astype(BF16)
    ak = piece(5)
    akt = ak.T
    akt_ref[0] = akt
    av = piece(6)
    if prompt:
        avh_ref, aktb_ref, avb_ref = kv_refs
        tm = av.shape[0]
        for hd in range(N_HEADS):
            avh_ref[pl.ds(hd, tm, stride=N_HEADS), :] = av[:, hd * HEAD:(hd + 1) * HEAD]
        aktb_ref[0] = akt.astype(BF16)
        avb_ref[...] = av.astype(BF16)
    else:
        av_ref, akb_ref = kv_refs
        av_ref[...] = av
        akb_ref[...] = ak.astype(BF16)


def _inproj(x, norm_w, w_bf, lb_param, layer, tm, seq, prompt):
    rows, d = x.shape
    d_in = w_bf.shape[1]
    per_seq = seq // tm
    f32_out = jax.ShapeDtypeStruct((rows, GROUP), F32)
    bf_out = jax.ShapeDtypeStruct((rows, GROUP), BF16)
    row_spec = pl.BlockSpec((tm, GROUP), lambda i: (i, 0))
    kt_spec = pl.BlockSpec((1, GROUP, tm), lambda i: (i // per_seq, 0, i % per_seq))
    kt_shape = (rows // seq, GROUP, seq)
    if prompt:
        narrow = bf_out
        kv_specs = [pl.BlockSpec((tm * N_HEADS, HEAD), lambda i: (i, 0)), kt_spec, row_spec]
        kv_shapes = [jax.ShapeDtypeStruct((rows * N_HEADS, HEAD), F32),
                     jax.ShapeDtypeStruct(kt_shape, BF16), bf_out]
    else:
        narrow = f32_out
        kv_specs = [row_spec, row_spec]
        kv_shapes = [f32_out, bf_out]
    return pl.pallas_call(
        functools.partial(_inproj_kernel, layer, prompt),
        grid=(rows // tm,),
        in_specs=[pl.BlockSpec((tm, d), lambda i: (i, 0)),
                  _const_spec((1, d)),
                  _const_spec((d, d_in)),
                  _const_spec(lb_param.shape)],
        out_specs=[row_spec] * 6 + [kt_spec] + kv_specs,
        out_shape=[f32_out] * 3 + [narrow, f32_out, bf_out, jax.ShapeDtypeStruct(kt_shape, F32)] + kv_shapes,
        compiler_params=pltpu.CompilerParams(
            dimension_semantics=("parallel",), vmem_limit_bytes=VMEM_LIMIT),
        name="inproj",
    )(x, norm_w, w_bf, lb_param)


def _cumsum_rows(x):
    n, lanes = x.shape
    groups = n // SUBLANES
    x3 = x.reshape(groups, SUBLANES, lanes)
    sub = lax.broadcasted_iota(jnp.int32, x3.shape, 1)
    shift = 1
    while shift < SUBLANES:
        x3 = x3 + jnp.where(sub >= shift, pltpu.roll(x3, shift, 1), 0.0)
        shift *= 2
    totals = x3[:, SUBLANES - 1:SUBLANES, :]
    pre = totals
    shift = 1
    while shift < groups:
        pre = pre + jnp.concatenate([jnp.zeros((shift, 1, lanes), F32), pre[:groups - shift]], axis=0)
        shift *= 2
    return (x3 + (pre - totals)).reshape(n, lanes)


def _hgrn_prompt_kernel(q_ref, g_ref, k_ref, v_ref, o_ref, s_ref, st_scr, b_scr, oi_scr):
    i = pl.program_id(0)

    @pl.when(i == 0)
    def _():
        st_scr[...] = jnp.zeros_like(st_scr)

    c = HGRN_CHUNK
    half = c // 2
    row = lax.broadcasted_iota(jnp.int32, (c, c), 0)
    col = lax.broadcasted_iota(jnp.int32, (c, c), 1)
    tri = col <= row
    nt = (((1,), (1,)), ((), ()))
    tn = (((0,), (0,)), ((), ()))
    chains = [(bi, ci, h) for bi in range(q_ref.shape[0]) for ci in range(HGRN_BLOCK // c)
              for h in range(N_HEADS)]

    def window(bi, ci, h):
        return bi, slice(ci * c, (ci + 1) * c), slice(h * HEAD, (h + 1) * HEAD)

    worst = jnp.zeros((1, HEAD), F32)
    for chain in chains:
        w = window(*chain)
        b = _cumsum_rows(g_ref[w])
        b_scr[w] = b
        b_mid = b[half - 1:half, :]
        worst = jnp.maximum(worst, jnp.maximum(-b_mid, b_mid - b[c - 1:c, :]))
    safe = jnp.max(worst) < HGRN_SAFE_EXPONENT

    def finish(chain, intra, q0, k2, v, b_last):
        bi, _, h = chain
        st = st_scr[bi, h]
        o_ref[window(*chain)] = lax.dot_general(q0, st.astype(BF16), nt, preferred_element_type=F32) + intra
        st_scr[bi, h] = st * jnp.exp(b_last) + lax.dot_general(v, k2, tn, preferred_element_type=F32)

    @pl.when(safe)
    def _():
        for chain in chains:
            w = window(*chain)
            b = b_scr[w]
            b_mid = b[half - 1:half, :]
            b_last = b[c - 1:c, :]
            d = b - b_mid
            qm = q_ref[w] * jnp.exp(d)
            km = k_ref[w] * jnp.exp(-d)
            q0 = (qm * jnp.exp(b_mid)).astype(BF16)
            k2 = (km * jnp.exp(b_last - b_mid)).astype(BF16)
            v = v_ref[w].astype(BF16)
            a = lax.dot_general(qm.astype(BF16), km.astype(BF16), nt, preferred_element_type=F32)
            a = jnp.where(tri, a, 0.0).astype(BF16)
            finish(chain, jnp.dot(a, v, preferred_element_type=F32), q0, k2, v, b_last)

    @pl.when(jnp.logical_not(safe))
    def _():
        srow = lax.broadcasted_iota(jnp.int32, (c, HEAD), 0)
        for chain in chains:
            bi, rs, cs = window(*chain)
            b = b_scr[bi, rs, cs]
            b_last = b[c - 1:c, :]
            kk = k_ref[bi, rs, cs]
            vv = v_ref[bi, rs, cs]

            def query_rows(tg, carry, bi=bi, rs=rs, cs=cs, b=b, kk=kk, vv=vv):
                base = pl.multiple_of(tg * SUBLANES, SUBLANES)
                at = pl.ds(rs.start + base, SUBLANES)
                b8 = b_scr[bi, at, cs]
                q8 = q_ref[bi, at, cs]
                out = []
                for j in range(SUBLANES):
                    decay = jnp.exp(jnp.where(srow <= base + j, b8[j:j + 1, :] - b, -jnp.inf))
                    a_col = jnp.sum(q8[j:j + 1, :] * kk * decay, axis=-1, keepdims=True)
                    out.append(jnp.sum(a_col * vv, axis=0, keepdims=True))
                oi_scr[pl.ds(base, SUBLANES), :] = jnp.concatenate(out, axis=0)
                return carry

            lax.fori_loop(0, c // SUBLANES, query_rows, 0)
            q0 = (q_ref[bi, rs, cs] * jnp.exp(b)).astype(BF16)
            k2 = (kk * jnp.exp(b_last - b)).astype(BF16)
            finish(chain, oi_scr[...], q0, k2, vv.astype(BF16), b_last)

    @pl.when(i == pl.num_programs(0) - 1)
    def _():
        for bi in range(q_ref.shape[0]):
            for h in range(N_HEADS):
                s_ref[bi, h] = st_scr[bi, h].T


def _hgrn_prompt(qr, logf, kr, vr, batch, seq):
    nb = seq // HGRN_BLOCK
    spec = pl.BlockSpec((batch, HGRN_BLOCK, GROUP), lambda i: (0, i, 0))
    shape3 = (batch, seq, GROUP)
    o_r, s_new = pl.pallas_call(
        _hgrn_prompt_kernel,
        grid=(nb,),
        in_specs=[spec] * 4,
        out_specs=[spec, pl.BlockSpec((batch, N_HEADS, HEAD, HEAD), lambda i: (0, 0, 0, 0))],
        out_shape=[jax.ShapeDtypeStruct(shape3, F32),
                   jax.ShapeDtypeStruct((batch, N_HEADS, HEAD, HEAD), F32)],
        scratch_shapes=[pltpu.VMEM((batch, N_HEADS, HEAD, HEAD), F32),
                        pltpu.VMEM((batch, HGRN_BLOCK, GROUP), F32),
                        pltpu.VMEM((HGRN_CHUNK, HEAD), F32)],
        compiler_params=pltpu.CompilerParams(
            dimension_semantics=("arbitrary",), vmem_limit_bytes=VMEM_LIMIT),
        name="hgrn_prompt",
    )(*(a.reshape(shape3) for a in (qr, logf, kr, vr)))
    return o_r.reshape(batch * seq, GROUP), s_new


def _hgrn_sample_kernel(q_ref, g_ref, k_ref, v_ref, s_in_ref, o_ref, s_out_ref):
    nb = q_ref.shape[0]
    for h in range(N_HEADS):
        cs = slice(h * HEAD, (h + 1) * HEAD)
        qt = q_ref[:, cs].T
        ft = jnp.exp(g_ref[:, cs]).T
        kt = k_ref[:, cs].T
        for j in range(nb):
            s_new = ft[:, j:j + 1] * s_in_ref[j, h] + kt[:, j:j + 1] * v_ref[j:j + 1, cs]
            s_out_ref[j, h] = s_new
            o_ref[j:j + 1, cs] = jnp.sum(qt[:, j:j + 1] * s_new, axis=0, keepdims=True)


def _hgrn_sample(qr, logf, kr, vr, state):
    n = qr.shape[0]
    nb = 8
    row_spec = pl.BlockSpec((nb, GROUP), lambda i: (i, 0))
    st_spec = pl.BlockSpec((nb, N_HEADS, HEAD, HEAD), lambda i: (i, 0, 0, 0))
    return pl.pallas_call(
        _hgrn_sample_kernel,
        grid=(n // nb,),
        in_specs=[row_spec] * 4 + [st_spec],
        out_specs=[row_spec, st_spec],
        out_shape=[jax.ShapeDtypeStruct((n, GROUP), F32),
                   jax.ShapeDtypeStruct(state.shape, F32)],
        compiler_params=pltpu.CompilerParams(
            dimension_semantics=("parallel",), vmem_limit_bytes=VMEM_LIMIT),
        name="hgrn_sample",
    )(qr, logf, kr, vr, state)


def _lambda(lq1_ref, lk1_ref, lq2_ref, lk2_ref, lam_init):
    a = jnp.sum(lq1_ref[...] * lk1_ref[...], axis=-1, keepdims=True)
    b = jnp.sum(lq2_ref[...] * lk2_ref[...], axis=-1, keepdims=True)
    return jnp.exp(a) - jnp.exp(b) + lam_init


ATT_FULL, ATT_FULL_DIAG, ATT_DIAG = 0, 1, 2


def _attn_prompt_kernel(lam_init, qi_tab, kb_tab, kind_tab, first_tab,
                        q_ref, kt_ref, v_ref, lq1_ref, lk1_ref, lq2_ref, lk2_ref, sub_ref,
                        o_ref, m_scr, acc_scr):
    p = pl.program_id(2)
    kind = kind_tab[p]

    @pl.when(first_tab[p] == 1)
    def _():
        m_scr[...] = jnp.full_like(m_scr, -jnp.inf)
        acc_scr[...] = jnp.zeros_like(acc_scr)

    tq, tk = q_ref.shape[0], v_ref.shape[0]
    half_chunks = tq // HEAD

    def sweep(step_kind):
        masked = step_kind != ATT_FULL
        v_ext = jnp.concatenate([v_ref[...], jnp.ones((tk, HEAD), BF16)], axis=-1)
        lane = lax.broadcasted_iota(jnp.int32, (ATT_ROWS, HEAD), 1)
        row = lax.broadcasted_iota(jnp.int32, (ATT_ROWS, HEAD), 0)
        zero = jnp.zeros((ATT_ROWS, HEAD), BF16)
        group_chunks = ATT_ROWS // HEAD
        for r in range(tq // ATT_ROWS):
            rows = slice(r * ATT_ROWS, (r + 1) * ATT_ROWS)
            q = q_ref[rows, :]
            diag_chunks = (r + 1) * group_chunks
            n_chunks = {ATT_FULL: tk // HEAD, ATT_FULL_DIAG: half_chunks + diag_chunks,
                        ATT_DIAG: diag_chunks}[step_kind]
            kt = kt_ref[0, :, 0:n_chunks * HEAD]
            probs, alphas = [], []
            for mi, qm in enumerate((jnp.where(lane < A_DK, q, zero), jnp.where(lane >= A_DK, q, zero))):
                s = jnp.dot(qm, kt, preferred_element_type=F32)
                chunks = [s[:, c * HEAD:(c + 1) * HEAD] for c in range(n_chunks)]
                if masked:
                    for g in range(group_chunks):
                        c = n_chunks - group_chunks + g
                        chunks[c] = jnp.where(lane + g * HEAD <= row, chunks[c], -jnp.inf)
                mc = functools.reduce(jnp.maximum, chunks)
                m_prev = m_scr[mi, rows]
                m_new = jnp.maximum(m_prev, jnp.max(mc, axis=-1, keepdims=True))
                alphas.append(jnp.exp2(m_prev - m_new))
                probs.append(jnp.concatenate([jnp.exp2(sc - m_new).astype(BF16) for sc in chunks], axis=-1))
                m_scr[mi, rows] = m_new
            pv = jnp.dot(jnp.concatenate(probs, axis=0), v_ext[0:n_chunks * HEAD], preferred_element_type=F32)
            for mi, alpha in enumerate(alphas):
                acc_scr[mi, rows] = (jnp.concatenate([alpha, alpha], axis=-1) * acc_scr[mi, rows]
                                     + pv[mi * ATT_ROWS:(mi + 1) * ATT_ROWS])

    def finalize():
        lam = _lambda(lq1_ref, lk1_ref, lq2_ref, lk2_ref, lam_init)
        a0 = acc_scr[0]
        a1 = acc_scr[1]
        o = a0[:, :HEAD] / a0[:, HEAD:] - lam * (a1[:, :HEAD] / a1[:, HEAD:])
        o_ref[...] = (_rms(o, sub_ref[...]) * (1.0 - lam_init)).astype(o_ref.dtype)

    @pl.when(kind == ATT_FULL)
    def _():
        sweep(ATT_FULL)

    if tk > tq:
        @pl.when(kind == ATT_FULL_DIAG)
        def _():
            sweep(ATT_FULL_DIAG)
            finalize()

    @pl.when(kind == ATT_DIAG)
    def _():
        sweep(ATT_DIAG)
        finalize()


def _attn_prompt(aq_bf, akt_bf, av_bf, lq1, lk1, lq2, lk2, subln, lam_init, batch, seq):
    t = ATT_BLOCK
    ratio = ATT_KEY_BLOCK // t
    nq = seq // t
    assert ATT_ROWS % HEAD == 0 and ratio in (1, 2) and nq % ratio == 0
    steps = []
    for i in range(nq):
        for j in range(i // ratio):
            steps.append((i, j, ATT_FULL, int(j == 0)))
        steps.append((i, i // ratio, ATT_FULL_DIAG if i % ratio else ATT_DIAG, int(i < ratio)))
    qi_tab, kb_tab, kind_tab, first_tab = (jnp.asarray([s[c] for s in steps], jnp.int32) for c in range(4))
    q_spec = pl.BlockSpec((t, HEAD), lambda b, h, p, qi, kb, kind, first: (b * nq + qi[p], h))
    v_spec = pl.BlockSpec((ratio * t, HEAD),
                          lambda b, h, p, qi, kb, kind, first: (b * (nq // ratio) + kb[p], h))
    kt_spec = pl.BlockSpec((1, HEAD, ratio * t), lambda b, h, p, qi, kb, kind, first: (b, h, kb[p]))
    lam_spec = pl.BlockSpec((1, A_DK), lambda *_: (0, 0))
    sub_spec = pl.BlockSpec((1, HEAD), lambda *_: (0, 0))
    return pl.pallas_call(
        functools.partial(_attn_prompt_kernel, lam_init),
        grid_spec=pltpu.PrefetchScalarGridSpec(
            num_scalar_prefetch=4,
            grid=(batch, N_HEADS, len(steps)),
            in_specs=[q_spec, kt_spec, v_spec, lam_spec, lam_spec, lam_spec, lam_spec, sub_spec],
            out_specs=q_spec,
            scratch_shapes=[pltpu.VMEM((2, t, HEAD), F32), pltpu.VMEM((2, t, 2 * HEAD), F32)]),
        out_shape=jax.ShapeDtypeStruct((batch * seq, GROUP), BF16),
        compiler_params=pltpu.CompilerParams(
            dimension_semantics=("parallel", "parallel", "arbitrary"), vmem_limit_bytes=VMEM_LIMIT),
        name="attn_prompt",
    )(qi_tab, kb_tab, kind_tab, first_tab, aq_bf, akt_bf, av_bf, lq1, lk1, lq2, lk2, subln)


def _page_copies(pt_ref, item, n_pages, ck_hbm, cv_hbm, kbuf, vbuf, sem, slot):
    page_rows = ck_hbm.shape[1]
    out = []
    for j in range(n_pages):
        page = pt_ref[item * n_pages + j]
        out.append(pltpu.make_async_copy(ck_hbm.at[page], kbuf.at[slot, j], sem.at[slot, 0]))
        out.append(pltpu.make_async_copy(cv_hbm.at[page], vbuf.at[slot, pl.ds(j * page_rows, page_rows)],
                                         sem.at[slot, 1]))
    return out


def _decode_item(lam, lam_init, q, kn, vn, sub4, kpages, vrows):
    n_pages = kpages.shape[0]
    past = n_pages * kpages.shape[2]
    rows = 2 * N_HEADS
    ri = lax.broadcasted_iota(jnp.int32, (rows, GROUP), 0)
    li = lax.broadcasted_iota(jnp.int32, (rows, GROUP), 1)
    mine = (li // A_DK) == (ri % N_HEADS) * 2 + ri // N_HEADS
    qbd = jnp.where(mine, jnp.broadcast_to(q.astype(F32), (rows, GROUP)), 0.0)
    qbd_bf = qbd.astype(BF16)
    s = jnp.concatenate(
        [jnp.dot(qbd_bf, kpages[j].astype(BF16), preferred_element_type=F32) for j in range(n_pages)],
        axis=-1)
    s_new = jnp.sum(qbd * kn.astype(F32), axis=-1, keepdims=True)
    m = jnp.maximum(jnp.max(s, axis=-1, keepdims=True), s_new)
    p = jnp.exp2(s - m)
    p_new = jnp.exp2(s_new - m)
    inv_l = 1.0 / (jnp.sum(p, axis=-1, keepdims=True) + p_new)
    pn = p * inv_l
    pn_new = p_new * inv_l
    attn = pn - lam * pltpu.roll(pn, N_HEADS, 0)
    attn_new = pn_new - lam * pltpu.roll(pn_new, N_HEADS, 0)
    attn_bf = attn.astype(BF16)
    o = jnp.concatenate(
        [jnp.dot(attn_bf, vrows[pl.ds(h, past, stride=N_HEADS), :].astype(BF16), preferred_element_type=F32)
         for h in range(N_HEADS)], axis=-1) + attn_new * vn
    keep = jnp.logical_and(ri < N_HEADS, li // HEAD == ri)
    o = jnp.where(keep, o, 0.0)
    ms = jnp.sum(o * o, axis=-1, keepdims=True) * (1.0 / HEAD)
    o = o * lax.rsqrt(ms + EPS) * sub4 * (1.0 - lam_init)
    return jnp.sum(o, axis=0, keepdims=True)


def _tail_kernel(decode, *refs):
    if decode is None:
        (x_ref, or_ref, gate_ref, oa_ref, gn_ref, wo_ref, nf_ref, wg_ref, wu_ref, wd_ref, nfin_ref,
         y_ref, act_scr) = refs
    else:
        (pt_ref, x_ref, or_ref, gate_ref, oa_ref, gn_ref, wo_ref, nf_ref, wg_ref, wu_ref, wd_ref, nfin_ref,
         q_ref, kn_ref, vn_ref, lq1_ref, lk1_ref, lq2_ref, lk2_ref, sub4_ref, ck_hbm, cv_hbm,
         y_ref, os_ref, act_scr, kbuf, vbuf, sem) = refs
        lam_init, n_pages, items = decode
        step = pl.program_id(0)
        copies = functools.partial(_page_copies, pt_ref, n_pages=n_pages, ck_hbm=ck_hbm, cv_hbm=cv_hbm,
                                   kbuf=kbuf, vbuf=vbuf, sem=sem)

        @pl.when(step == 0)
        def _():
            for cp in copies(item=0, slot=0):
                cp.start()

    def decode_item(jj):
        if decode is None:
            return
        item = step * items + jj
        slot = jj % 2
        if jj + 1 < items:
            for cp in copies(item=item + 1, slot=1 - slot):
                cp.start()
        else:
            @pl.when(step + 1 < pl.num_programs(0))
            def _():
                for cp in copies(item=item + 1, slot=1 - slot):
                    cp.start()
        for cp in copies(item=item, slot=slot):
            cp.wait()
        lam = _lambda(lq1_ref, lk1_ref, lq2_ref, lk2_ref, lam_init)
        os_ref[jj] = _decode_item(lam, lam_init, q_ref[jj], kn_ref[jj], vn_ref[jj], sub4_ref[...],
                                  kbuf.at[slot], vbuf.at[slot])

    tm = x_ref.shape[0]
    d_ff = wg_ref.shape[1]
    sub = min(tm, TAIL_ROWS)
    groups = tm // sub
    per_group = 0 if decode is None else items // (2 * groups)
    for g in range(groups):
        rows = slice(g * sub, (g + 1) * sub)
        for jj in range(per_group):
            decode_item(2 * g * per_group + jj)
        parts = []
        for h in range(N_HEADS):
            cs = slice(h * HEAD, (h + 1) * HEAD)
            parts.append((_rms(or_ref[rows, cs], gn_ref[...]) * gate_ref[rows, cs]).astype(BF16))
        parts.append(oa_ref[rows, :].astype(BF16))
        mix = jnp.concatenate(parts, axis=-1)
        x1 = x_ref[rows, :] + jnp.dot(mix, wo_ref[...], preferred_element_type=F32)
        h2 = _rms(x1, nf_ref[...]).astype(BF16)
        for j in range(d_ff // FF_CHUNK):
            fs = slice(j * FF_CHUNK, (j + 1) * FF_CHUNK)
            gt = jnp.dot(h2, wg_ref[:, fs], preferred_element_type=F32)
            up = jnp.dot(h2, wu_ref[:, fs], preferred_element_type=F32)
            act_scr[rows, fs] = (gt * _sigmoid(gt) * up).astype(BF16)
        for jj in range(per_group):
            decode_item((2 * g + 1) * per_group + jj)
        y = x1 + jnp.dot(act_scr[rows, :], wd_ref[...], preferred_element_type=F32)
        y_ref[rows, :] = _rms(y, nfin_ref[...])


def _tail(x, o_r, gate, o_a, r_gnorm, wo_bf, norm_ffn, wg_bf, wu_bf, wd_bf, norm_final, tm, decode=None):
    rows, d = x.shape
    d_ff = wg_bf.shape[1]
    steps = rows // tm

    def resident(shape):
        return pl.BlockSpec(shape, lambda i, *_: (0, 0), pipeline_mode=pl.Buffered(1))

    def const(shape):
        return pl.BlockSpec(shape, lambda i, *_: (0,) * len(shape))

    def row_block(width):
        return pl.BlockSpec((tm, width), lambda i, *_: (i, 0))

    in_specs = [row_block(d), row_block(GROUP), row_block(GROUP), row_block(GROUP), const((1, HEAD)),
                resident((d, d)), const((1, d)), resident((d, d_ff)), resident((d, d_ff)), resident((d_ff, d)),
                const((1, d))]
    operands = [x, o_r, gate, o_a, r_gnorm, wo_bf, norm_ffn, wg_bf, wu_bf, wd_bf, norm_final]
    out_specs = [row_block(d)]
    out_shape = [jax.ShapeDtypeStruct((rows, d), F32)]
    scratch = [pltpu.VMEM((tm, d_ff), BF16)]
    config = None
    prefetch = []
    if decode is not None:
        lam_init, page_table, aq_bf, ak_bf, av, lams, subln4, cache_k, cache_v = decode
        n, n_pages = page_table.shape
        n_phys, page_size = cache_k.shape[0], cache_k.shape[1]
        items = n // steps
        assert items * steps == n and items % (2 * (tm // min(tm, TAIL_ROWS))) == 0
        ck = jnp.transpose(cache_k, (0, 2, 3, 4, 1)).reshape(n_phys, GROUP, page_size)
        cv = cache_v.reshape(n_phys, page_size * N_HEADS, HEAD)
        item_block = pl.BlockSpec((items, 1, GROUP), lambda i, *_: (i, 0, 0))
        in_specs += [item_block, item_block, item_block] + [const((1, A_DK))] * 4 + [
            const((1, GROUP)), pl.BlockSpec(memory_space=pl.ANY), pl.BlockSpec(memory_space=pl.ANY)]
        operands += [a.reshape(n, 1, GROUP) for a in (aq_bf, ak_bf, av)] + list(lams) + [subln4, ck, cv]
        out_specs.append(item_block)
        out_shape.append(jax.ShapeDtypeStruct((n, 1, GROUP), F32))
        scratch += [pltpu.VMEM((2, n_pages, GROUP, page_size), F32),
                    pltpu.VMEM((2, n_pages * page_size * N_HEADS, HEAD), F32),
                    pltpu.SemaphoreType.DMA((2, 2))]
        config = (lam_init, n_pages, items)
        prefetch = [page_table.reshape(-1)]
    out = pl.pallas_call(
        functools.partial(_tail_kernel, config),
        grid_spec=pltpu.PrefetchScalarGridSpec(
            num_scalar_prefetch=len(prefetch), grid=(steps,),
            in_specs=in_specs, out_specs=out_specs, scratch_shapes=scratch),
        out_shape=out_shape,
        compiler_params=pltpu.CompilerParams(
            dimension_semantics=("arbitrary",), vmem_limit_bytes=TAIL_VMEM_LIMIT),
        name="tail",
    )(*prefetch, *operands)
    if decode is None:
        return out[0]
    return out[0], out[1].reshape(n, GROUP)


def kernel(x_prompt, x_sample, cache_k, cache_v, state_hgrn, page_table, w_in, w_out, lb_param,
           r_gnorm, lam_q1, lam_k1, lam_q2, lam_k2, a_subln, norm_mix, norm_ffn, w_gate, w_up,
           w_down, norm_final):
    batch, seq, d = x_prompt.shape
    n_dec = x_sample.shape[0]
    depth = w_in.shape[0]
    assert depth == 1 and x_sample.shape[1] == 1
    hp = x_prompt.reshape(batch * seq, d)
    hs = x_sample.reshape(n_dec, d)
    lb_param = lb_param.astype(F32)
    nfin = norm_final.reshape(1, d)
    kp, vp, sp, kss, vss, sss = [], [], [], [], [], []
    for l in range(depth):
        lam_init = 0.8 - 0.6 * math.exp(-0.3 * l)
        w_in_bf = w_in[l].astype(BF16)
        wo_bf = w_out[l].astype(BF16)
        wg_bf = w_gate[l].astype(BF16)
        wu_bf = w_up[l].astype(BF16)
        wd_bf = w_down[l].astype(BF16)
        nmix = norm_mix[l].reshape(1, d)
        nffn = norm_ffn[l].reshape(1, d)
        gn = r_gnorm[l].reshape(1, HEAD)
        sub = a_subln[l].reshape(1, HEAD)
        lams = [a[l].reshape(1, A_DK).astype(F32) for a in (lam_q1, lam_k1, lam_q2, lam_k2)]

        qr_s, logf_s, kr_s, vr_s, gate_s, aq_s, akt_s, av_s, ak_s = _inproj(
            hs, nmix, w_in_bf, lb_param, l, n_dec, n_dec, prompt=False)
        o_r_s, s_new_s = _hgrn_sample(qr_s, logf_s, kr_s, vr_s, state_hgrn[l].astype(F32))

        qr, logf, kr, vr, gate, aq_bf, akt, av_heads, akt_bf, av_bf = _inproj(
            hp, nmix, w_in_bf, lb_param, l, ROW_BLOCK, seq, prompt=True)
        o_r, s_new = _hgrn_prompt(qr, logf, kr, vr, batch, seq)
        o_a = _attn_prompt(aq_bf, akt_bf, av_bf, *lams, sub, lam_init, batch, seq)
        paged = (lam_init, page_table, aq_s, ak_s, av_s, lams, jnp.tile(sub, (1, N_HEADS)),
                 cache_k[l], cache_v[l])
        hp, o_a_s = _tail(hp, o_r, gate, o_a, gn, wo_bf, nffn, wg_bf, wu_bf, wd_bf, nfin, TAIL_BLOCK, paged)
        kp.append(jnp.transpose(akt.reshape(batch, N_HEADS, 2, A_DK, seq), (0, 4, 1, 2, 3)))
        vp.append(av_heads.reshape(batch, seq, N_HEADS, HEAD))
        sp.append(s_new)

        hs = _tail(hs, o_r_s, gate_s, o_a_s, gn, wo_bf, nffn, wg_bf, wu_bf, wd_bf, nfin, n_dec)
        kss.append(jnp.transpose(akt_s.reshape(N_HEADS, 2, A_DK, n_dec), (3, 0, 1, 2))[:, None])
        vss.append(av_s.reshape(n_dec, 1, N_HEADS, HEAD))
        sss.append(s_new_s)

    y_prompt = hp.reshape(batch, seq, d)
    y_sample = hs.reshape(n_dec, 1, d)
    return (y_prompt, y_sample, jnp.stack(kp), jnp.stack(vp), jnp.stack(sp),
            jnp.stack(kss), jnp.stack(vss), jnp.stack(sss))
```

```python
import functools
import math

import jax
import jax.numpy as jnp
from jax import lax
from jax.experimental import pallas as pl
from jax.experimental.pallas import tpu as pltpu

F32 = jnp.float32
BF16 = jnp.bfloat16

EPS = 1e-6
LOG2E = math.log2(math.e)
HEAD = 128
N_HEADS = 4
A_DK = 64
GROUP = N_HEADS * HEAD
N_PIECES = 7
SUBLANES = 8
HGRN_CHUNK = 128
HGRN_BLOCK = 256
HGRN_SAFE_EXPONENT = 60.0
ATT_BLOCK = 2048
ATT_KEY_BLOCK = 2048
ATT_ROWS = 128
ROW_BLOCK = 512
IN_ROWS = 256
TAIL_BLOCK = 256
TAIL_ROWS = 256
TAIL_VMEM_LIMIT = 56 * 1024 * 1024
FF_CHUNK = 256
VMEM_LIMIT = 48 * 1024 * 1024


def _sigmoid(x):
    return 1.0 / (1.0 + jnp.exp(-x))


def _rms(x, w):
    return x * lax.rsqrt(jnp.mean(x * x, axis=-1, keepdims=True) + EPS) * w


def _const_spec(shape):
    return pl.BlockSpec(shape, lambda *_: (0,) * len(shape))


def _inproj_kernel(layer, prompt, x_ref, nw_ref, w_ref, lb_ref,
                   qr_ref, logf_ref, kr_ref, vr_ref, gate_ref, aq_ref, akt_ref, *kv_refs):
    lbp = lb_ref[...]
    e = jnp.exp(lbp - jnp.max(lbp, axis=0, keepdims=True))
    lb = jnp.sum(e[:layer + 1], axis=0, keepdims=True) / jnp.sum(e, axis=0, keepdims=True)

    tm = x_ref.shape[0]
    sub = min(tm, IN_ROWS)
    for g in range(tm // sub):
        rows = slice(g * sub, (g + 1) * sub)
        h = _rms(x_ref[rows, :], nw_ref[...]).astype(BF16)

        def piece(j, h=h):
            return jnp.dot(h, w_ref[:, j * GROUP:(j + 1) * GROUP], preferred_element_type=F32)

        rq = piece(0)
        qr_ref[rows, :] = rq * _sigmoid(rq)
        f = lb + (1.0 - lb) * _sigmoid(piece(1))
        logf = jnp.log(f)
        if prompt:
            c, half = HGRN_CHUNK, HGRN_CHUNK // 2
            worst = jnp.zeros((1, GROUP), F32)
            for ci in range(sub // c):
                b = _cumsum_rows(logf[ci * c:(ci + 1) * c])
                logf_ref[g * sub + ci * c:g * sub + (ci + 1) * c, :] = b
                b_mid = b[half - 1:half, :]
                worst = jnp.maximum(worst, jnp.maximum(-b_mid, b_mid - b[c - 1:c, :]))
            worst = functools.reduce(jnp.maximum,
                                     [worst[:, hd * HEAD:(hd + 1) * HEAD] for hd in range(N_HEADS)])
        else:
            logf_ref[rows, :] = logf
        kr_ref[rows, :] = 1.0 - f
        vr_ref[rows, :] = piece(2).astype(vr_ref.dtype)
        rg = piece(3)
        gate_ref[rows, :] = rg * _sigmoid(rg)
        aq_ref[rows, :] = (piece(4) * (A_DK ** -0.5 * LOG2E)).astype(BF16)
        ak = piece(5)
        akt = ak.T
        akt_ref[0, :, rows] = akt
        av = piece(6)
        if prompt:
            avh_ref, aktb_ref, avb_ref, worst_ref = kv_refs
            worst_ref[g] = jnp.broadcast_to(worst, (SUBLANES, HEAD))
            for hd in range(N_HEADS):
                avh_ref[pl.ds(g * sub * N_HEADS + hd, sub, stride=N_HEADS), :] = av[:, hd * HEAD:(hd + 1) * HEAD]
            aktb_ref[0, :, rows] = akt.astype(BF16)
            avb_ref[rows, :] = av.astype(BF16)
        else:
            av_ref, akb_ref = kv_refs
            av_ref[rows, :] = av
            akb_ref[rows, :] = ak.astype(BF16)


def _inproj(x, norm_w, w_bf, lb_param, layer, tm, seq, prompt):
    rows, d = x.shape
    d_in = w_bf.shape[1]
    per_seq = seq // tm
    f32_out = jax.ShapeDtypeStruct((rows, GROUP), F32)
    bf_out = jax.ShapeDtypeStruct((rows, GROUP), BF16)
    row_spec = pl.BlockSpec((tm, GROUP), lambda i: (i, 0))
    kt_spec = pl.BlockSpec((1, GROUP, tm), lambda i: (i // per_seq, 0, i % per_seq))
    kt_shape = (rows // seq, GROUP, seq)
    if prompt:
        narrow = bf_out
        sub = min(tm, IN_ROWS)
        assert sub % HGRN_CHUNK == 0 and tm % sub == 0
        kv_specs = [pl.BlockSpec((tm * N_HEADS, HEAD), lambda i: (i, 0)), kt_spec, row_spec,
                    pl.BlockSpec((tm // sub, SUBLANES, HEAD), lambda i: (i, 0, 0))]
        kv_shapes = [jax.ShapeDtypeStruct((rows * N_HEADS, HEAD), F32),
                     jax.ShapeDtypeStruct(kt_shape, BF16), bf_out,
                     jax.ShapeDtypeStruct((rows // sub, SUBLANES, HEAD), F32)]
    else:
        narrow = f32_out
        kv_specs = [row_spec, row_spec]
        kv_shapes = [f32_out, bf_out]
    return pl.pallas_call(
        functools.partial(_inproj_kernel, layer, prompt),
        grid=(rows // tm,),
        in_specs=[pl.BlockSpec((tm, d), lambda i: (i, 0)),
                  _const_spec((1, d)),
                  pl.BlockSpec((d, d_in), lambda i: (0, 0), pipeline_mode=pl.Buffered(1)),
                  _const_spec(lb_param.shape)],
        out_specs=[row_spec] * 6 + [kt_spec] + kv_specs,
        out_shape=[f32_out] * 3 + [narrow, f32_out, bf_out, jax.ShapeDtypeStruct(kt_shape, F32)] + kv_shapes,
        compiler_params=pltpu.CompilerParams(
            dimension_semantics=("parallel",), vmem_limit_bytes=VMEM_LIMIT),
        name="inproj",
    )(x, norm_w, w_bf, lb_param)


def _cumsum_rows(x):
    n, lanes = x.shape
    groups = n // SUBLANES
    x3 = x.reshape(groups, SUBLANES, lanes)
    sub = lax.broadcasted_iota(jnp.int32, x3.shape, 1)
    shift = 1
    while shift < SUBLANES:
        x3 = x3 + jnp.where(sub >= shift, pltpu.roll(x3, shift, 1), 0.0)
        shift *= 2
    totals = x3[:, SUBLANES - 1:SUBLANES, :]
    pre = totals
    shift = 1
    while shift < groups:
        pre = pre + jnp.concatenate([jnp.zeros((shift, 1, lanes), F32), pre[:groups - shift]], axis=0)
        shift *= 2
    return (x3 + (pre - totals)).reshape(n, lanes)


def _hgrn_prompt_kernel(q_ref, b_ref, k_ref, v_ref, worst_ref, o_ref, s_ref, st_scr, oi_scr):
    i = pl.program_id(0)

    @pl.when(i == 0)
    def _():
        st_scr[...] = jnp.zeros_like(st_scr)

    c = HGRN_CHUNK
    half = c // 2
    row = lax.broadcasted_iota(jnp.int32, (c, c), 0)
    col = lax.broadcasted_iota(jnp.int32, (c, c), 1)
    tri = col <= row
    nt = (((1,), (1,)), ((), ()))
    tn = (((0,), (0,)), ((), ()))
    chains = [(bi, ci, h) for bi in range(q_ref.shape[0]) for ci in range(HGRN_BLOCK // c)
              for h in range(N_HEADS)]

    def window(bi, ci, h):
        return bi, slice(ci * c, (ci + 1) * c), slice(h * HEAD, (h + 1) * HEAD)

    safe = jnp.max(worst_ref[...]) < HGRN_SAFE_EXPONENT

    def finish(chain, intra, q0, k2, v, b_last):
        bi, _, h = chain
        st = st_scr[bi, h]
        o_ref[window(*chain)] = lax.dot_general(q0, st.astype(BF16), nt, preferred_element_type=F32) + intra
        st_scr[bi, h] = st * jnp.exp(b_last) + lax.dot_general(v, k2, tn, preferred_element_type=F32)

    @pl.when(safe)
    def _():
        for chain in chains:
            w = window(*chain)
            b = b_ref[w]
            b_mid = b[half - 1:half, :]
            b_last = b[c - 1:c, :]
            d = b - b_mid
            qm = q_ref[w] * jnp.exp(d)
            km = k_ref[w] * jnp.exp(-d)
            q0 = (qm * jnp.exp(b_mid)).astype(BF16)
            k2 = (km * jnp.exp(b_last - b_mid)).astype(BF16)
            v = v_ref[w].astype(BF16)
            a = lax.dot_general(qm.astype(BF16), km.astype(BF16), nt, preferred_element_type=F32)
            a = jnp.where(tri, a, 0.0).astype(BF16)
            finish(chain, jnp.dot(a, v, preferred_element_type=F32), q0, k2, v, b_last)

    @pl.when(jnp.logical_not(safe))
    def _():
        srow = lax.broadcasted_iota(jnp.int32, (c, HEAD), 0)
        for chain in chains:
            bi, rs, cs = window(*chain)
            b = b_ref[bi, rs, cs]
            b_last = b[c - 1:c, :]
            kk = k_ref[bi, rs, cs]
            vv = v_ref[bi, rs, cs]

            def query_rows(tg, carry, bi=bi, rs=rs, cs=cs, b=b, kk=kk, vv=vv):
                base = pl.multiple_of(tg * SUBLANES, SUBLANES)
                at = pl.ds(rs.start + base, SUBLANES)
                b8 = b_ref[bi, at, cs]
                q8 = q_ref[bi, at, cs]
                out = []
                for j in range(SUBLANES):
                    decay = jnp.exp(jnp.where(srow <= base + j, b8[j:j + 1, :] - b, -jnp.inf))
                    a_col = jnp.sum(q8[j:j + 1, :] * kk * decay, axis=-1, keepdims=True)
                    out.append(jnp.sum(a_col * vv, axis=0, keepdims=True))
                oi_scr[pl.ds(base, SUBLANES), :] = jnp.concatenate(out, axis=0)
                return carry

            lax.fori_loop(0, c // SUBLANES, query_rows, 0)
            q0 = (q_ref[bi, rs, cs] * jnp.exp(b)).astype(BF16)
            k2 = (kk * jnp.exp(b_last - b)).astype(BF16)
            finish(chain, oi_scr[...], q0, k2, vv.astype(BF16), b_last)

    @pl.when(i == pl.num_programs(0) - 1)
    def _():
        for bi in range(q_ref.shape[0]):
            for h in range(N_HEADS):
                s_ref[bi, h] = st_scr[bi, h].T


def _hgrn_prompt(qr, cum_logf, kr, vr, worst, batch, seq):
    nb = seq // HGRN_BLOCK
    spec = pl.BlockSpec((batch, HGRN_BLOCK, GROUP), lambda i: (0, i, 0))
    shape3 = (batch, seq, GROUP)
    o_r, s_new = pl.pallas_call(
        _hgrn_prompt_kernel,
        grid=(nb,),
        in_specs=[spec] * 4 + [pl.BlockSpec((batch, 1, SUBLANES, HEAD), lambda i: (0, i, 0, 0))],
        out_specs=[spec, pl.BlockSpec((batch, N_HEADS, HEAD, HEAD), lambda i: (0, 0, 0, 0))],
        out_shape=[jax.ShapeDtypeStruct(shape3, F32),
                   jax.ShapeDtypeStruct((batch, N_HEADS, HEAD, HEAD), F32)],
        scratch_shapes=[pltpu.VMEM((batch, N_HEADS, HEAD, HEAD), F32),
                        pltpu.VMEM((HGRN_CHUNK, HEAD), F32)],
        compiler_params=pltpu.CompilerParams(
            dimension_semantics=("arbitrary",), vmem_limit_bytes=VMEM_LIMIT),
        name="hgrn_prompt",
    )(*(a.reshape(shape3) for a in (qr, cum_logf, kr, vr)), worst.reshape(batch, nb, SUBLANES, HEAD))
    return o_r.reshape(batch * seq, GROUP), s_new


def _hgrn_sample_kernel(q_ref, g_ref, k_ref, v_ref, s_in_ref, o_ref, s_out_ref):
    nb = q_ref.shape[0]
    for h in range(N_HEADS):
        cs = slice(h * HEAD, (h + 1) * HEAD)
        qt = q_ref[:, cs].T
        ft = jnp.exp(g_ref[:, cs]).T
        kt = k_ref[:, cs].T
        for j in range(nb):
            s_new = ft[:, j:j + 1] * s_in_ref[j, h] + kt[:, j:j + 1] * v_ref[j:j + 1, cs]
            s_out_ref[j, h] = s_new
            o_ref[j:j + 1, cs] = jnp.sum(qt[:, j:j + 1] * s_new, axis=0, keepdims=True)


def _hgrn_sample(qr, logf, kr, vr, state):
    n = qr.shape[0]
    nb = 8
    row_spec = pl.BlockSpec((nb, GROUP), lambda i: (i, 0))
    st_spec = pl.BlockSpec((nb, N_HEADS, HEAD, HEAD), lambda i: (i, 0, 0, 0))
    return pl.pallas_call(
        _hgrn_sample_kernel,
        grid=(n // nb,),
        in_specs=[row_spec] * 4 + [st_spec],
        out_specs=[row_spec, st_spec],
        out_shape=[jax.ShapeDtypeStruct((n, GROUP), F32),
                   jax.ShapeDtypeStruct(state.shape, F32)],
        compiler_params=pltpu.CompilerParams(
            dimension_semantics=("parallel",), vmem_limit_bytes=VMEM_LIMIT),
        name="hgrn_sample",
    )(qr, logf, kr, vr, state)


def _lambda(lq1_ref, lk1_ref, lq2_ref, lk2_ref, lam_init):
    a = jnp.sum(lq1_ref[...] * lk1_ref[...], axis=-1, keepdims=True)
    b = jnp.sum(lq2_ref[...] * lk2_ref[...], axis=-1, keepdims=True)
    return jnp.exp(a) - jnp.exp(b) + lam_init


ATT_FULL, ATT_FULL_DIAG, ATT_DIAG = 0, 1, 2


def _attn_prompt_kernel(lam_init, qi_tab, kb_tab, kind_tab, first_tab,
                        q_ref, kt_ref, v_ref, lq1_ref, lk1_ref, lq2_ref, lk2_ref, sub_ref,
                        o_ref, m_scr, acc_scr):
    p = pl.program_id(2)
    kind = kind_tab[p]

    @pl.when(first_tab[p] == 1)
    def _():
        m_scr[...] = jnp.full_like(m_scr, -jnp.inf)
        acc_scr[...] = jnp.zeros_like(acc_scr)

    tq, tk = q_ref.shape[0], v_ref.shape[0]
    half_chunks = tq // HEAD

    def sweep(step_kind):
        masked = step_kind != ATT_FULL
        v_ext = jnp.concatenate([v_ref[...], jnp.ones((tk, HEAD), BF16)], axis=-1)
        lane = lax.broadcasted_iota(jnp.int32, (ATT_ROWS, HEAD), 1)
        row = lax.broadcasted_iota(jnp.int32, (ATT_ROWS, HEAD), 0)
        zero = jnp.zeros((ATT_ROWS, HEAD), BF16)
        group_chunks = ATT_ROWS // HEAD
        groups = range(tq // ATT_ROWS)
        for r in (reversed(groups) if masked else groups):
            rows = slice(r * ATT_ROWS, (r + 1) * ATT_ROWS)
            q = q_ref[rows, :]
            diag_chunks = (r + 1) * group_chunks
            n_chunks = {ATT_FULL: tk // HEAD, ATT_FULL_DIAG: half_chunks + diag_chunks,
                        ATT_DIAG: diag_chunks}[step_kind]
            kt = kt_ref[0, :, 0:n_chunks * HEAD]
            probs, alphas = [], []
            for mi, qm in enumerate((jnp.where(lane < A_DK, q, zero), jnp.where(lane >= A_DK, q, zero))):
                s = jnp.dot(qm, kt, preferred_element_type=F32)
                chunks = [s[:, c * HEAD:(c + 1) * HEAD] for c in range(n_chunks)]
                if masked:
                    for g in range(group_chunks):
                        c = n_chunks - group_chunks + g
                        chunks[c] = jnp.where(lane + g * HEAD <= row, chunks[c], -jnp.inf)
                mc = functools.reduce(jnp.maximum, chunks)
                m_prev = m_scr[mi, rows]
                m_new = jnp.maximum(m_prev, jnp.max(mc, axis=-1, keepdims=True))
                alphas.append(jnp.exp2(m_prev - m_new))
                probs.append(jnp.concatenate([jnp.exp2(sc - m_new).astype(BF16) for sc in chunks], axis=-1))
                m_scr[mi, rows] = m_new
            pv = jnp.dot(jnp.concatenate(probs, axis=0), v_ext[0:n_chunks * HEAD], preferred_element_type=F32)
            for mi, alpha in enumerate(alphas):
                acc_scr[mi, rows] = (jnp.concatenate([alpha, alpha], axis=-1) * acc_scr[mi, rows]
                                     + pv[mi * ATT_ROWS:(mi + 1) * ATT_ROWS])

    def finalize():
        lam = _lambda(lq1_ref, lk1_ref, lq2_ref, lk2_ref, lam_init)
        a0 = acc_scr[0]
        a1 = acc_scr[1]
        o = a0[:, :HEAD] / a0[:, HEAD:] - lam * (a1[:, :HEAD] / a1[:, HEAD:])
        o_ref[...] = (_rms(o, sub_ref[...]) * (1.0 - lam_init)).astype(o_ref.dtype)

    @pl.when(kind == ATT_FULL)
    def _():
        sweep(ATT_FULL)

    if tk > tq:
        @pl.when(kind == ATT_FULL_DIAG)
        def _():
            sweep(ATT_FULL_DIAG)
            finalize()

    @pl.when(kind == ATT_DIAG)
    def _():
        sweep(ATT_DIAG)
        finalize()


def _attn_prompt(aq_bf, akt_bf, av_bf, lq1, lk1, lq2, lk2, subln, lam_init, batch, seq):
    t = ATT_BLOCK
    ratio = ATT_KEY_BLOCK // t
    nq = seq // t
    assert ATT_ROWS % HEAD == 0 and ratio in (1, 2) and nq % ratio == 0
    steps = []
    for i in range(nq):
        for j in range(i // ratio):
            steps.append((i, j, ATT_FULL, int(j == 0)))
        steps.append((i, i // ratio, ATT_FULL_DIAG if i % ratio else ATT_DIAG, int(i < ratio)))
    qi_tab, kb_tab, kind_tab, first_tab = (jnp.asarray([s[c] for s in steps], jnp.int32) for c in range(4))
    q_spec = pl.BlockSpec((t, HEAD), lambda b, h, p, qi, kb, kind, first: (b * nq + qi[p], h))
    v_spec = pl.BlockSpec((ratio * t, HEAD),
                          lambda b, h, p, qi, kb, kind, first: (b * (nq // ratio) + kb[p], h))
    kt_spec = pl.BlockSpec((1, HEAD, ratio * t), lambda b, h, p, qi, kb, kind, first: (b, h, kb[p]))
    lam_spec = pl.BlockSpec((1, A_DK), lambda *_: (0, 0))
    sub_spec = pl.BlockSpec((1, HEAD), lambda *_: (0, 0))
    return pl.pallas_call(
        functools.partial(_attn_prompt_kernel, lam_init),
        grid_spec=pltpu.PrefetchScalarGridSpec(
            num_scalar_prefetch=4,
            grid=(batch, N_HEADS, len(steps)),
            in_specs=[q_spec, kt_spec, v_spec, lam_spec, lam_spec, lam_spec, lam_spec, sub_spec],
            out_specs=q_spec,
            scratch_shapes=[pltpu.VMEM((2, t, HEAD), F32), pltpu.VMEM((2, t, 2 * HEAD), F32)]),
        out_shape=jax.ShapeDtypeStruct((batch * seq, GROUP), BF16),
        compiler_params=pltpu.CompilerParams(
            dimension_semantics=("parallel", "parallel", "arbitrary"), vmem_limit_bytes=VMEM_LIMIT),
        name="attn_prompt",
    )(qi_tab, kb_tab, kind_tab, first_tab, aq_bf, akt_bf, av_bf, lq1, lk1, lq2, lk2, subln)


def _page_copies(pt_ref, item, n_pages, ck_hbm, cv_hbm, kbuf, vbuf, sem, slot):
    page_rows = ck_hbm.shape[1]
    out = []
    for j in range(n_pages):
        page = pt_ref[item * n_pages + j]
        out.append(pltpu.make_async_copy(ck_hbm.at[page], kbuf.at[slot, j], sem.at[slot, 0]))
        out.append(pltpu.make_async_copy(cv_hbm.at[page], vbuf.at[slot, pl.ds(j * page_rows, page_rows)],
                                         sem.at[slot, 1]))
    return out


def _decode_item(lam, lam_init, q, kn, vn, sub4, kpages, vrows):
    n_pages = kpages.shape[0]
    past = n_pages * kpages.shape[2]
    rows = 2 * N_HEADS
    ri = lax.broadcasted_iota(jnp.int32, (rows, GROUP), 0)
    li = lax.broadcasted_iota(jnp.int32, (rows, GROUP), 1)
    mine = (li // A_DK) == (ri % N_HEADS) * 2 + ri // N_HEADS
    qbd = jnp.where(mine, jnp.broadcast_to(q.astype(F32), (rows, GROUP)), 0.0)
    qbd_bf = qbd.astype(BF16)
    s = jnp.concatenate(
        [jnp.dot(qbd_bf, kpages[j].astype(BF16), preferred_element_type=F32) for j in range(n_pages)],
        axis=-1)
    s_new = jnp.sum(qbd * kn.astype(F32), axis=-1, keepdims=True)
    m = jnp.maximum(jnp.max(s, axis=-1, keepdims=True), s_new)
    p = jnp.exp2(s - m)
    p_new = jnp.exp2(s_new - m)
    inv_l = 1.0 / (jnp.sum(p, axis=-1, keepdims=True) + p_new)
    pn = p * inv_l
    pn_new = p_new * inv_l
    attn = pn - lam * pltpu.roll(pn, N_HEADS, 0)
    attn_new = pn_new - lam * pltpu.roll(pn_new, N_HEADS, 0)
    attn_bf = attn.astype(BF16)
    o = jnp.concatenate(
        [jnp.dot(attn_bf, vrows[pl.ds(h, past, stride=N_HEADS), :].astype(BF16), preferred_element_type=F32)
         for h in range(N_HEADS)], axis=-1) + attn_new * vn
    keep = jnp.logical_and(ri < N_HEADS, li // HEAD == ri)
    o = jnp.where(keep, o, 0.0)
    ms = jnp.sum(o * o, axis=-1, keepdims=True) * (1.0 / HEAD)
    o = o * lax.rsqrt(ms + EPS) * sub4 * (1.0 - lam_init)
    return jnp.sum(o, axis=0, keepdims=True)


def _tail_kernel(decode, *refs):
    if decode is None:
        (x_ref, or_ref, gate_ref, oa_ref, gn_ref, wo_ref, nf_ref, wg_ref, wu_ref, wd_ref, nfin_ref,
         y_ref, act_scr) = refs
    else:
        (pt_ref, x_ref, or_ref, gate_ref, oa_ref, gn_ref, wo_ref, nf_ref, wg_ref, wu_ref, wd_ref, nfin_ref,
         q_ref, kn_ref, vn_ref, lq1_ref, lk1_ref, lq2_ref, lk2_ref, sub4_ref, ck_hbm, cv_hbm,
         y_ref, os_ref, act_scr, kbuf, vbuf, sem) = refs
        lam_init, n_pages, items = decode
        step = pl.program_id(0)
        copies = functools.partial(_page_copies, pt_ref, n_pages=n_pages, ck_hbm=ck_hbm, cv_hbm=cv_hbm,
                                   kbuf=kbuf, vbuf=vbuf, sem=sem)

        @pl.when(step == 0)
        def _():
            for cp in copies(item=0, slot=0):
                cp.start()

    def decode_item(jj):
        if decode is None:
            return
        item = step * items + jj
        slot = jj % 2
        if jj + 1 < items:
            for cp in copies(item=item + 1, slot=1 - slot):
                cp.start()
        else:
            @pl.when(step + 1 < pl.num_programs(0))
            def _():
                for cp in copies(item=item + 1, slot=1 - slot):
                    cp.start()
        for cp in copies(item=item, slot=slot):
            cp.wait()
        lam = _lambda(lq1_ref, lk1_ref, lq2_ref, lk2_ref, lam_init)
        os_ref[jj] = _decode_item(lam, lam_init, q_ref[jj], kn_ref[jj], vn_ref[jj], sub4_ref[...],
                                  kbuf.at[slot], vbuf.at[slot])

    tm = x_ref.shape[0]
    d_ff = wg_ref.shape[1]
    sub = min(tm, TAIL_ROWS)
    groups = tm // sub
    per_group = 0 if decode is None else items // (2 * groups)
    for g in range(groups):
        rows = slice(g * sub, (g + 1) * sub)
        for jj in range(per_group):
            decode_item(2 * g * per_group + jj)
        parts = []
        for h in range(N_HEADS):
            cs = slice(h * HEAD, (h + 1) * HEAD)
            parts.append((_rms(or_ref[rows, cs], gn_ref[...]) * gate_ref[rows, cs]).astype(BF16))
        parts.append(oa_ref[rows, :].astype(BF16))
        mix = jnp.concatenate(parts, axis=-1)
        x1 = x_ref[rows, :] + jnp.dot(mix, wo_ref[...], preferred_element_type=F32)
        h2 = _rms(x1, nf_ref[...]).astype(BF16)
        for j in range(d_ff // FF_CHUNK):
            fs = slice(j * FF_CHUNK, (j + 1) * FF_CHUNK)
            gt = jnp.dot(h2, wg_ref[:, fs], preferred_element_type=F32)
            up = jnp.dot(h2, wu_ref[:, fs], preferred_element_type=F32)
            act_scr[rows, fs] = (gt * _sigmoid(gt) * up).astype(BF16)
        for jj in range(per_group):
            decode_item((2 * g + 1) * per_group + jj)
        y = x1 + jnp.dot(act_scr[rows, :], wd_ref[...], preferred_element_type=F32)
        y_ref[rows, :] = _rms(y, nfin_ref[...])


def _tail(x, o_r, gate, o_a, r_gnorm, wo_bf, norm_ffn, wg_bf, wu_bf, wd_bf, norm_final, tm, decode=None):
    rows, d = x.shape
    d_ff = wg_bf.shape[1]
    steps = rows // tm

    def resident(shape):
        return pl.BlockSpec(shape, lambda i, *_: (0, 0), pipeline_mode=pl.Buffered(1))

    def const(shape):
        return pl.BlockSpec(shape, lambda i, *_: (0,) * len(shape))

    def row_block(width):
        return pl.BlockSpec((tm, width), lambda i, *_: (i, 0))

    in_specs = [row_block(d), row_block(GROUP), row_block(GROUP), row_block(GROUP), const((1, HEAD)),
                resident((d, d)), const((1, d)), resident((d, d_ff)), resident((d, d_ff)), resident((d_ff, d)),
                const((1, d))]
    operands = [x, o_r, gate, o_a, r_gnorm, wo_bf, norm_ffn, wg_bf, wu_bf, wd_bf, norm_final]
    out_specs = [row_block(d)]
    out_shape = [jax.ShapeDtypeStruct((rows, d), F32)]
    scratch = [pltpu.VMEM((tm, d_ff), BF16)]
    config = None
    prefetch = []
    if decode is not None:
        lam_init, page_table, aq_bf, ak_bf, av, lams, subln4, cache_k, cache_v = decode
        n, n_pages = page_table.shape
        n_phys, page_size = cache_k.shape[0], cache_k.shape[1]
        items = n // steps
        assert items * steps == n and items % (2 * (tm // min(tm, TAIL_ROWS))) == 0
        ck = jnp.transpose(cache_k, (0, 2, 3, 4, 1)).reshape(n_phys, GROUP, page_size)
        cv = cache_v.reshape(n_phys, page_size * N_HEADS, HEAD)
        item_block = pl.BlockSpec((items, 1, GROUP), lambda i, *_: (i, 0, 0))
        in_specs += [item_block, item_block, item_block] + [const((1, A_DK))] * 4 + [
            const((1, GROUP)), pl.BlockSpec(memory_space=pl.ANY), pl.BlockSpec(memory_space=pl.ANY)]
        operands += [a.reshape(n, 1, GROUP) for a in (aq_bf, ak_bf, av)] + list(lams) + [subln4, ck, cv]
        out_specs.append(item_block)
        out_shape.append(jax.ShapeDtypeStruct((n, 1, GROUP), F32))
        scratch += [pltpu.VMEM((2, n_pages, GROUP, page_size), F32),
                    pltpu.VMEM((2, n_pages * page_size * N_HEADS, HEAD), F32),
                    pltpu.SemaphoreType.DMA((2, 2))]
        config = (lam_init, n_pages, items)
        prefetch = [page_table.reshape(-1)]
    out = pl.pallas_call(
        functools.partial(_tail_kernel, config),
        grid_spec=pltpu.PrefetchScalarGridSpec(
            num_scalar_prefetch=len(prefetch), grid=(steps,),
            in_specs=in_specs, out_specs=out_specs, scratch_shapes=scratch),
        out_shape=out_shape,
        compiler_params=pltpu.CompilerParams(
            dimension_semantics=("arbitrary",), vmem_limit_bytes=TAIL_VMEM_LIMIT),
        name="tail",
    )(*prefetch, *operands)
    if decode is None:
        return out[0]
    return out[0], out[1].reshape(n, GROUP)


def kernel(x_prompt, x_sample, cache_k, cache_v, state_hgrn, page_table, w_in, w_out, lb_param,
           r_gnorm, lam_q1, lam_k1, lam_q2, lam_k2, a_subln, norm_mix, norm_ffn, w_gate, w_up,
           w_down, norm_final):
    batch, seq, d = x_prompt.shape
    n_dec = x_sample.shape[0]
    depth = w_in.shape[0]
    assert depth == 1 and x_sample.shape[1] == 1
    hp = x_prompt.reshape(batch * seq, d)
    hs = x_sample.reshape(n_dec, d)
    lb_param = lb_param.astype(F32)
    nfin = norm_final.reshape(1, d)
    kp, vp, sp, kss, vss, sss = [], [], [], [], [], []
    for l in range(depth):
        lam_init = 0.8 - 0.6 * math.exp(-0.3 * l)
        w_in_bf = w_in[l].astype(BF16)
        wo_bf = w_out[l].astype(BF16)
        wg_bf = w_gate[l].astype(BF16)
        wu_bf = w_up[l].astype(BF16)
        wd_bf = w_down[l].astype(BF16)
        nmix = norm_mix[l].reshape(1, d)
        nffn = norm_ffn[l].reshape(1, d)
        gn = r_gnorm[l].reshape(1, HEAD)
        sub = a_subln[l].reshape(1, HEAD)
        lams = [a[l].reshape(1, A_DK).astype(F32) for a in (lam_q1, lam_k1, lam_q2, lam_k2)]

        qr_s, logf_s, kr_s, vr_s, gate_s, aq_s, akt_s, av_s, ak_s = _inproj(
            hs, nmix, w_in_bf, lb_param, l, n_dec, n_dec, prompt=False)
        o_r_s, s_new_s = _hgrn_sample(qr_s, logf_s, kr_s, vr_s, state_hgrn[l].astype(F32))

        assert IN_ROWS == HGRN_BLOCK
        qr, cum_logf, kr, vr, gate, aq_bf, akt, av_heads, akt_bf, av_bf, worst = _inproj(
            hp, nmix, w_in_bf, lb_param, l, ROW_BLOCK, seq, prompt=True)
        o_r, s_new = _hgrn_prompt(qr, cum_logf, kr, vr, worst, batch, seq)
        o_a = _attn_prompt(aq_bf, akt_bf, av_bf, *lams, sub, lam_init, batch, seq)
        paged = (lam_init, page_table, aq_s, ak_s, av_s, lams, jnp.tile(sub, (1, N_HEADS)),
                 cache_k[l], cache_v[l])
        hp, o_a_s = _tail(hp, o_r, gate, o_a, gn, wo_bf, nffn, wg_bf, wu_bf, wd_bf, nfin, TAIL_BLOCK, paged)
        kp.append(jnp.transpose(akt.reshape(batch, N_HEADS, 2, A_DK, seq), (0, 4, 1, 2, 3)))
        vp.append(av_heads.reshape(batch, seq, N_HEADS, HEAD))
        sp.append(s_new)

        hs = _tail(hs, o_r_s, gate_s, o_a_s, gn, wo_bf, nffn, wg_bf, wu_bf, wd_bf, nfin, n_dec)
        kss.append(jnp.transpose(akt_s.reshape(N_HEADS, 2, A_DK, n_dec), (3, 0, 1, 2))[:, None])
        vss.append(av_s.reshape(n_dec, 1, N_HEADS, HEAD))
        sss.append(s_new_s)

    y_prompt = hp.reshape(batch, seq, d)
    y_sample = hs.reshape(n_dec, 1, d)
    return (y_prompt, y_sample, jnp.stack(kp), jnp.stack(vp), jnp.stack(sp),
            jnp.stack(kss), jnp.stack(vss), jnp.stack(sss))
```

```python
import functools
import math

import jax
import jax.numpy as jnp
from jax import lax
from jax.experimental import pallas as pl
from jax.experimental.pallas import tpu as pltpu

F32 = jnp.float32
BF16 = jnp.bfloat16

EPS = 1e-6
LOG2E = math.log2(math.e)
HEAD = 128
N_HEADS = 4
A_DK = 64
GROUP = N_HEADS * HEAD
N_PIECES = 7
SUBLANES = 8
HGRN_CHUNK = 128
HGRN_BLOCK = 256
HGRN_SAFE_EXPONENT = 60.0
ATT_BLOCK = 2048
ATT_KEY_BLOCK = 2048
ATT_ROWS = 128
ROW_BLOCK = 512
IN_ROWS = 256
TAIL_BLOCK = 256
TAIL_ROWS = 256
TAIL_VMEM_LIMIT = 56 * 1024 * 1024
FF_CHUNK = 256
VMEM_LIMIT = 48 * 1024 * 1024


def _sigmoid(x):
    return 1.0 / (1.0 + jnp.exp(-x))


def _rms(x, w):
    return x * lax.rsqrt(jnp.mean(x * x, axis=-1, keepdims=True) + EPS) * w


def _const_spec(shape):
    return pl.BlockSpec(shape, lambda *_: (0,) * len(shape))


def _inproj_kernel(layer, prompt, x_ref, nw_ref, w_ref, lb_ref,
                   qr_ref, logf_ref, kr_ref, vr_ref, gate_ref, aq_ref, akt_ref, *kv_refs):
    lbp = lb_ref[...]
    e = jnp.exp(lbp - jnp.max(lbp, axis=0, keepdims=True))
    lb = jnp.sum(e[:layer + 1], axis=0, keepdims=True) / jnp.sum(e, axis=0, keepdims=True)

    tm = x_ref.shape[0]
    sub = min(tm, IN_ROWS)
    for g in range(tm // sub):
        rows = slice(g * sub, (g + 1) * sub)
        h = _rms(x_ref[rows, :], nw_ref[...]).astype(BF16)

        def piece(j, h=h):
            return jnp.dot(h, w_ref[:, j * GROUP:(j + 1) * GROUP], preferred_element_type=F32)

        rq = piece(0)
        qr_ref[rows, :] = (rq * _sigmoid(rq)).astype(qr_ref.dtype)
        f = lb + (1.0 - lb) * _sigmoid(piece(1))
        logf = jnp.log(f)
        if prompt:
            c, half = HGRN_CHUNK, HGRN_CHUNK // 2
            worst = jnp.zeros((1, GROUP), F32)
            for ci in range(sub // c):
                b = _cumsum_rows(logf[ci * c:(ci + 1) * c])
                logf_ref[g * sub + ci * c:g * sub + (ci + 1) * c, :] = b
                b_mid = b[half - 1:half, :]
                worst = jnp.maximum(worst, jnp.maximum(-b_mid, b_mid - b[c - 1:c, :]))
            worst = functools.reduce(jnp.maximum,
                                     [worst[:, hd * HEAD:(hd + 1) * HEAD] for hd in range(N_HEADS)])
        else:
            logf_ref[rows, :] = logf
        kr_ref[rows, :] = (1.0 - f).astype(kr_ref.dtype)
        vr_ref[rows, :] = piece(2).astype(vr_ref.dtype)
        rg = piece(3)
        gate_ref[rows, :] = (rg * _sigmoid(rg)).astype(gate_ref.dtype)
        aq_ref[rows, :] = (piece(4) * (A_DK ** -0.5 * LOG2E)).astype(BF16)
        ak = piece(5)
        akt = ak.T
        akt_ref[0, :, rows] = akt
        av = piece(6)
        if prompt:
            avh_ref, aktb_ref, avb_ref, worst_ref = kv_refs
            worst_ref[g] = jnp.broadcast_to(worst, (SUBLANES, HEAD))
            for hd in range(N_HEADS):
                avh_ref[pl.ds(g * sub * N_HEADS + hd, sub, stride=N_HEADS), :] = av[:, hd * HEAD:(hd + 1) * HEAD]
            aktb_ref[0, :, rows] = akt.astype(BF16)
            avb_ref[rows, :] = av.astype(BF16)
        else:
            av_ref, akb_ref = kv_refs
            av_ref[rows, :] = av
            akb_ref[rows, :] = ak.astype(BF16)


def _inproj(x, norm_w, w_bf, lb_param, layer, tm, seq, prompt):
    rows, d = x.shape
    d_in = w_bf.shape[1]
    per_seq = seq // tm
    f32_out = jax.ShapeDtypeStruct((rows, GROUP), F32)
    bf_out = jax.ShapeDtypeStruct((rows, GROUP), BF16)
    row_spec = pl.BlockSpec((tm, GROUP), lambda i: (i, 0))
    kt_spec = pl.BlockSpec((1, GROUP, tm), lambda i: (i // per_seq, 0, i % per_seq))
    kt_shape = (rows // seq, GROUP, seq)
    if prompt:
        narrow = bf_out
        sub = min(tm, IN_ROWS)
        assert sub % HGRN_CHUNK == 0 and tm % sub == 0
        kv_specs = [pl.BlockSpec((tm * N_HEADS, HEAD), lambda i: (i, 0)), kt_spec, row_spec,
                    pl.BlockSpec((tm // sub, SUBLANES, HEAD), lambda i: (i, 0, 0))]
        kv_shapes = [jax.ShapeDtypeStruct((rows * N_HEADS, HEAD), F32),
                     jax.ShapeDtypeStruct(kt_shape, BF16), bf_out,
                     jax.ShapeDtypeStruct((rows // sub, SUBLANES, HEAD), F32)]
    else:
        narrow = f32_out
        kv_specs = [row_spec, row_spec]
        kv_shapes = [f32_out, bf_out]
    return pl.pallas_call(
        functools.partial(_inproj_kernel, layer, prompt),
        grid=(rows // tm,),
        in_specs=[pl.BlockSpec((tm, d), lambda i: (i, 0)),
                  _const_spec((1, d)),
                  pl.BlockSpec((d, d_in), lambda i: (0, 0), pipeline_mode=pl.Buffered(1)),
                  _const_spec(lb_param.shape)],
        out_specs=[row_spec] * 6 + [kt_spec] + kv_specs,
        out_shape=[narrow, f32_out, narrow, narrow, narrow, bf_out,
                   jax.ShapeDtypeStruct(kt_shape, F32)] + kv_shapes,
        compiler_params=pltpu.CompilerParams(
            dimension_semantics=("parallel",), vmem_limit_bytes=VMEM_LIMIT),
        name="inproj",
    )(x, norm_w, w_bf, lb_param)


def _cumsum_rows(x):
    n, lanes = x.shape
    groups = n // SUBLANES
    x3 = x.reshape(groups, SUBLANES, lanes)
    sub = lax.broadcasted_iota(jnp.int32, x3.shape, 1)
    shift = 1
    while shift < SUBLANES:
        x3 = x3 + jnp.where(sub >= shift, pltpu.roll(x3, shift, 1), 0.0)
        shift *= 2
    totals = x3[:, SUBLANES - 1:SUBLANES, :]
    pre = totals
    shift = 1
    while shift < groups:
        pre = pre + jnp.concatenate([jnp.zeros((shift, 1, lanes), F32), pre[:groups - shift]], axis=0)
        shift *= 2
    return (x3 + (pre - totals)).reshape(n, lanes)


def _hgrn_prompt_kernel(q_ref, b_ref, k_ref, v_ref, worst_ref, o_ref, s_ref, st_scr, oi_scr):
    i = pl.program_id(0)

    @pl.when(i == 0)
    def _():
        st_scr[...] = jnp.zeros_like(st_scr)

    c = HGRN_CHUNK
    half = c // 2
    row = lax.broadcasted_iota(jnp.int32, (c, c), 0)
    col = lax.broadcasted_iota(jnp.int32, (c, c), 1)
    tri = col <= row
    nt = (((1,), (1,)), ((), ()))
    tn = (((0,), (0,)), ((), ()))
    chains = [(bi, ci, h) for bi in range(q_ref.shape[0]) for ci in range(HGRN_BLOCK // c)
              for h in range(N_HEADS)]

    def window(bi, ci, h):
        return bi, slice(ci * c, (ci + 1) * c), slice(h * HEAD, (h + 1) * HEAD)

    safe = jnp.max(worst_ref[...]) < HGRN_SAFE_EXPONENT

    def finish(chain, intra, q0, k2, v, b_last):
        bi, _, h = chain
        st = st_scr[bi, h]
        o = lax.dot_general(q0, st.astype(BF16), nt, preferred_element_type=F32) + intra
        o_ref[window(*chain)] = o.astype(o_ref.dtype)
        st_scr[bi, h] = st * jnp.exp(b_last) + lax.dot_general(v, k2, tn, preferred_element_type=F32)

    @pl.when(safe)
    def _():
        for chain in chains:
            w = window(*chain)
            b = b_ref[w]
            b_mid = b[half - 1:half, :]
            b_last = b[c - 1:c, :]
            d = b - b_mid
            qm = q_ref[w] * jnp.exp(d)
            km = k_ref[w] * jnp.exp(-d)
            q0 = (qm * jnp.exp(b_mid)).astype(BF16)
            k2 = (km * jnp.exp(b_last - b_mid)).astype(BF16)
            v = v_ref[w].astype(BF16)
            a = lax.dot_general(qm.astype(BF16), km.astype(BF16), nt, preferred_element_type=F32)
            a = jnp.where(tri, a, 0.0).astype(BF16)
            finish(chain, jnp.dot(a, v, preferred_element_type=F32), q0, k2, v, b_last)

    @pl.when(jnp.logical_not(safe))
    def _():
        srow = lax.broadcasted_iota(jnp.int32, (c, HEAD), 0)
        for chain in chains:
            bi, rs, cs = window(*chain)
            b = b_ref[bi, rs, cs]
            b_last = b[c - 1:c, :]
            kk = k_ref[bi, rs, cs]
            vv = v_ref[bi, rs, cs]

            def query_rows(tg, carry, bi=bi, rs=rs, cs=cs, b=b, kk=kk, vv=vv):
                base = pl.multiple_of(tg * SUBLANES, SUBLANES)
                at = pl.ds(rs.start + base, SUBLANES)
                b8 = b_ref[bi, at, cs]
                q8 = q_ref[bi, at, cs]
                out = []
                for j in range(SUBLANES):
                    decay = jnp.exp(jnp.where(srow <= base + j, b8[j:j + 1, :] - b, -jnp.inf))
                    a_col = jnp.sum(q8[j:j + 1, :] * kk * decay, axis=-1, keepdims=True)
                    out.append(jnp.sum(a_col * vv, axis=0, keepdims=True))
                oi_scr[pl.ds(base, SUBLANES), :] = jnp.concatenate(out, axis=0)
                return carry

            lax.fori_loop(0, c // SUBLANES, query_rows, 0)
            q0 = (q_ref[bi, rs, cs] * jnp.exp(b)).astype(BF16)
            k2 = (kk * jnp.exp(b_last - b)).astype(BF16)
            finish(chain, oi_scr[...], q0, k2, vv.astype(BF16), b_last)

    @pl.when(i == pl.num_programs(0) - 1)
    def _():
        for bi in range(q_ref.shape[0]):
            for h in range(N_HEADS):
                s_ref[bi, h] = st_scr[bi, h].T


def _hgrn_prompt(qr, cum_logf, kr, vr, worst, batch, seq):
    nb = seq // HGRN_BLOCK
    spec = pl.BlockSpec((batch, HGRN_BLOCK, GROUP), lambda i: (0, i, 0))
    shape3 = (batch, seq, GROUP)
    o_r, s_new = pl.pallas_call(
        _hgrn_prompt_kernel,
        grid=(nb,),
        in_specs=[spec] * 4 + [pl.BlockSpec((batch, 1, SUBLANES, HEAD), lambda i: (0, i, 0, 0))],
        out_specs=[spec, pl.BlockSpec((batch, N_HEADS, HEAD, HEAD), lambda i: (0, 0, 0, 0))],
        out_shape=[jax.ShapeDtypeStruct(shape3, BF16),
                   jax.ShapeDtypeStruct((batch, N_HEADS, HEAD, HEAD), F32)],
        scratch_shapes=[pltpu.VMEM((batch, N_HEADS, HEAD, HEAD), F32),
                        pltpu.VMEM((HGRN_CHUNK, HEAD), F32)],
        compiler_params=pltpu.CompilerParams(
            dimension_semantics=("arbitrary",), vmem_limit_bytes=VMEM_LIMIT),
        name="hgrn_prompt",
    )(*(a.reshape(shape3) for a in (qr, cum_logf, kr, vr)), worst.reshape(batch, nb, SUBLANES, HEAD))
    return o_r.reshape(batch * seq, GROUP), s_new


def _hgrn_sample_kernel(q_ref, g_ref, k_ref, v_ref, s_in_ref, o_ref, s_out_ref):
    nb = q_ref.shape[0]
    for h in range(N_HEADS):
        cs = slice(h * HEAD, (h + 1) * HEAD)
        qt = q_ref[:, cs].T
        ft = jnp.exp(g_ref[:, cs]).T
        kt = k_ref[:, cs].T
        for j in range(nb):
            s_new = ft[:, j:j + 1] * s_in_ref[j, h] + kt[:, j:j + 1] * v_ref[j:j + 1, cs]
            s_out_ref[j, h] = s_new
            o_ref[j:j + 1, cs] = jnp.sum(qt[:, j:j + 1] * s_new, axis=0, keepdims=True)


def _hgrn_sample(qr, logf, kr, vr, state):
    n = qr.shape[0]
    nb = 8
    row_spec = pl.BlockSpec((nb, GROUP), lambda i: (i, 0))
    st_spec = pl.BlockSpec((nb, N_HEADS, HEAD, HEAD), lambda i: (i, 0, 0, 0))
    return pl.pallas_call(
        _hgrn_sample_kernel,
        grid=(n // nb,),
        in_specs=[row_spec] * 4 + [st_spec],
        out_specs=[row_spec, st_spec],
        out_shape=[jax.ShapeDtypeStruct((n, GROUP), F32),
                   jax.ShapeDtypeStruct(state.shape, F32)],
        compiler_params=pltpu.CompilerParams(
            dimension_semantics=("parallel",), vmem_limit_bytes=VMEM_LIMIT),
        name="hgrn_sample",
    )(qr, logf, kr, vr, state)


def _lambda(lq1_ref, lk1_ref, lq2_ref, lk2_ref, lam_init):
    a = jnp.sum(lq1_ref[...] * lk1_ref[...], axis=-1, keepdims=True)
    b = jnp.sum(lq2_ref[...] * lk2_ref[...], axis=-1, keepdims=True)
    return jnp.exp(a) - jnp.exp(b) + lam_init


ATT_FULL, ATT_FULL_DIAG, ATT_DIAG = 0, 1, 2


def _attn_prompt_kernel(lam_init, qi_tab, kb_tab, kind_tab, first_tab,
                        q_ref, kt_ref, v_ref, lq1_ref, lk1_ref, lq2_ref, lk2_ref, sub_ref,
                        o_ref, m_scr, acc_scr):
    p = pl.program_id(2)
    kind = kind_tab[p]

    @pl.when(first_tab[p] == 1)
    def _():
        m_scr[...] = jnp.full_like(m_scr, -jnp.inf)
        acc_scr[...] = jnp.zeros_like(acc_scr)

    tq, tk = q_ref.shape[0], v_ref.shape[0]
    half_chunks = tq // HEAD

    def sweep(step_kind):
        masked = step_kind != ATT_FULL
        v_ext = jnp.concatenate([v_ref[...], jnp.ones((tk, HEAD), BF16)], axis=-1)
        lane = lax.broadcasted_iota(jnp.int32, (ATT_ROWS, HEAD), 1)
        row = lax.broadcasted_iota(jnp.int32, (ATT_ROWS, HEAD), 0)
        zero = jnp.zeros((ATT_ROWS, HEAD), BF16)
        group_chunks = ATT_ROWS // HEAD
        groups = range(tq // ATT_ROWS)
        for r in (reversed(groups) if masked else groups):
            rows = slice(r * ATT_ROWS, (r + 1) * ATT_ROWS)
            q = q_ref[rows, :]
            diag_chunks = (r + 1) * group_chunks
            n_chunks = {ATT_FULL: tk // HEAD, ATT_FULL_DIAG: half_chunks + diag_chunks,
                        ATT_DIAG: diag_chunks}[step_kind]
            kt = kt_ref[0, :, 0:n_chunks * HEAD]
            probs, alphas = [], []
            for mi, qm in enumerate((jnp.where(lane < A_DK, q, zero), jnp.where(lane >= A_DK, q, zero))):
                s = jnp.dot(qm, kt, preferred_element_type=F32)
                chunks = [s[:, c * HEAD:(c + 1) * HEAD] for c in range(n_chunks)]
                if masked:
                    for g in range(group_chunks):
                        c = n_chunks - group_chunks + g
                        chunks[c] = jnp.where(lane + g * HEAD <= row, chunks[c], -jnp.inf)
                mc = functools.reduce(jnp.maximum, chunks)
                m_prev = m_scr[mi, rows]
                m_new = jnp.maximum(m_prev, jnp.max(mc, axis=-1, keepdims=True))
                alphas.append(jnp.exp2(m_prev - m_new))
                probs.append(jnp.concatenate([jnp.exp2(sc - m_new).astype(BF16) for sc in chunks], axis=-1))
                m_scr[mi, rows] = m_new
            pv = jnp.dot(jnp.concatenate(probs, axis=0), v_ext[0:n_chunks * HEAD], preferred_element_type=F32)
            for mi, alpha in enumerate(alphas):
                acc_scr[mi, rows] = (jnp.concatenate([alpha, alpha], axis=-1) * acc_scr[mi, rows]
                                     + pv[mi * ATT_ROWS:(mi + 1) * ATT_ROWS])

    def finalize():
        lam = _lambda(lq1_ref, lk1_ref, lq2_ref, lk2_ref, lam_init)
        a0 = acc_scr[0]
        a1 = acc_scr[1]
        o = a0[:, :HEAD] / a0[:, HEAD:] - lam * (a1[:, :HEAD] / a1[:, HEAD:])
        o_ref[...] = (_rms(o, sub_ref[...]) * (1.0 - lam_init)).astype(o_ref.dtype)

    @pl.when(kind == ATT_FULL)
    def _():
        sweep(ATT_FULL)

    if tk > tq:
        @pl.when(kind == ATT_FULL_DIAG)
        def _():
            sweep(ATT_FULL_DIAG)
            finalize()

    @pl.when(kind == ATT_DIAG)
    def _():
        sweep(ATT_DIAG)
        finalize()


def _attn_prompt(aq_bf, akt_bf, av_bf, lq1, lk1, lq2, lk2, subln, lam_init, batch, seq):
    t = ATT_BLOCK
    ratio = ATT_KEY_BLOCK // t
    nq = seq // t
    assert ATT_ROWS % HEAD == 0 and ratio in (1, 2) and nq % ratio == 0
    steps = []
    for i in range(nq):
        for j in range(i // ratio):
            steps.append((i, j, ATT_FULL, int(j == 0)))
        steps.append((i, i // ratio, ATT_FULL_DIAG if i % ratio else ATT_DIAG, int(i < ratio)))
    qi_tab, kb_tab, kind_tab, first_tab = (jnp.asarray([s[c] for s in steps], jnp.int32) for c in range(4))
    q_spec = pl.BlockSpec((t, HEAD), lambda b, h, p, qi, kb, kind, first: (b * nq + qi[p], h))
    v_spec = pl.BlockSpec((ratio * t, HEAD),
                          lambda b, h, p, qi, kb, kind, first: (b * (nq // ratio) + kb[p], h))
    kt_spec = pl.BlockSpec((1, HEAD, ratio * t), lambda b, h, p, qi, kb, kind, first: (b, h, kb[p]))
    lam_spec = pl.BlockSpec((1, A_DK), lambda *_: (0, 0))
    sub_spec = pl.BlockSpec((1, HEAD), lambda *_: (0, 0))
    return pl.pallas_call(
        functools.partial(_attn_prompt_kernel, lam_init),
        grid_spec=pltpu.PrefetchScalarGridSpec(
            num_scalar_prefetch=4,
            grid=(batch, N_HEADS, len(steps)),
            in_specs=[q_spec, kt_spec, v_spec, lam_spec, lam_spec, lam_spec, lam_spec, sub_spec],
            out_specs=q_spec,
            scratch_shapes=[pltpu.VMEM((2, t, HEAD), F32), pltpu.VMEM((2, t, 2 * HEAD), F32)]),
        out_shape=jax.ShapeDtypeStruct((batch * seq, GROUP), BF16),
        compiler_params=pltpu.CompilerParams(
            dimension_semantics=("parallel", "parallel", "arbitrary"), vmem_limit_bytes=VMEM_LIMIT),
        name="attn_prompt",
    )(qi_tab, kb_tab, kind_tab, first_tab, aq_bf, akt_bf, av_bf, lq1, lk1, lq2, lk2, subln)


def _page_copies(pt_ref, item, n_pages, ck_hbm, cv_hbm, kbuf, vbuf, sem, slot):
    page_rows = ck_hbm.shape[1]
    out = []
    for j in range(n_pages):
        page = pt_ref[item * n_pages + j]
        out.append(pltpu.make_async_copy(ck_hbm.at[page], kbuf.at[slot, j], sem.at[slot, 0]))
        out.append(pltpu.make_async_copy(cv_hbm.at[page], vbuf.at[slot, pl.ds(j * page_rows, page_rows)],
                                         sem.at[slot, 1]))
    return out


def _decode_item(lam, lam_init, q, kn, vn, sub4, kpages, vrows):
    n_pages = kpages.shape[0]
    past = n_pages * kpages.shape[2]
    rows = 2 * N_HEADS
    ri = lax.broadcasted_iota(jnp.int32, (rows, GROUP), 0)
    li = lax.broadcasted_iota(jnp.int32, (rows, GROUP), 1)
    mine = (li // A_DK) == (ri % N_HEADS) * 2 + ri // N_HEADS
    qbd = jnp.where(mine, jnp.broadcast_to(q.astype(F32), (rows, GROUP)), 0.0)
    qbd_bf = qbd.astype(BF16)
    s = jnp.concatenate(
        [jnp.dot(qbd_bf, kpages[j].astype(BF16), preferred_element_type=F32) for j in range(n_pages)],
        axis=-1)
    s_new = jnp.sum(qbd * kn.astype(F32), axis=-1, keepdims=True)
    m = jnp.maximum(jnp.max(s, axis=-1, keepdims=True), s_new)
    p = jnp.exp2(s - m)
    p_new = jnp.exp2(s_new - m)
    inv_l = 1.0 / (jnp.sum(p, axis=-1, keepdims=True) + p_new)
    pn = p * inv_l
    pn_new = p_new * inv_l
    attn = pn - lam * pltpu.roll(pn, N_HEADS, 0)
    attn_new = pn_new - lam * pltpu.roll(pn_new, N_HEADS, 0)
    attn_bf = attn.astype(BF16)
    o = jnp.concatenate(
        [jnp.dot(attn_bf, vrows[pl.ds(h, past, stride=N_HEADS), :].astype(BF16), preferred_element_type=F32)
         for h in range(N_HEADS)], axis=-1) + attn_new * vn
    keep = jnp.logical_and(ri < N_HEADS, li // HEAD == ri)
    o = jnp.where(keep, o, 0.0)
    ms = jnp.sum(o * o, axis=-1, keepdims=True) * (1.0 / HEAD)
    o = o * lax.rsqrt(ms + EPS) * sub4 * (1.0 - lam_init)
    return jnp.sum(o, axis=0, keepdims=True)


def _tail_kernel(decode, *refs):
    if decode is None:
        (x_ref, or_ref, gate_ref, oa_ref, gn_ref, wo_ref, nf_ref, wg_ref, wu_ref, wd_ref, nfin_ref,
         y_ref, act_scr) = refs
    else:
        (pt_ref, x_ref, or_ref, gate_ref, oa_ref, gn_ref, wo_ref, nf_ref, wg_ref, wu_ref, wd_ref, nfin_ref,
         q_ref, kn_ref, vn_ref, lq1_ref, lk1_ref, lq2_ref, lk2_ref, sub4_ref, ck_hbm, cv_hbm,
         y_ref, os_ref, act_scr, kbuf, vbuf, sem) = refs
        lam_init, n_pages, items = decode
        step = pl.program_id(0)
        copies = functools.partial(_page_copies, pt_ref, n_pages=n_pages, ck_hbm=ck_hbm, cv_hbm=cv_hbm,
                                   kbuf=kbuf, vbuf=vbuf, sem=sem)

        @pl.when(step == 0)
        def _():
            for cp in copies(item=0, slot=0):
                cp.start()

    def decode_item(jj):
        if decode is None:
            return
        item = step * items + jj
        slot = jj % 2
        if jj + 1 < items:
            for cp in copies(item=item + 1, slot=1 - slot):
                cp.start()
        else:
            @pl.when(step + 1 < pl.num_programs(0))
            def _():
                for cp in copies(item=item + 1, slot=1 - slot):
                    cp.start()
        for cp in copies(item=item, slot=slot):
            cp.wait()
        lam = _lambda(lq1_ref, lk1_ref, lq2_ref, lk2_ref, lam_init)
        os_ref[jj] = _decode_item(lam, lam_init, q_ref[jj], kn_ref[jj], vn_ref[jj], sub4_ref[...],
                                  kbuf.at[slot], vbuf.at[slot])

    tm = x_ref.shape[0]
    d_ff = wg_ref.shape[1]
    sub = min(tm, TAIL_ROWS)
    groups = tm // sub
    per_group = 0 if decode is None else items // (2 * groups)
    for g in range(groups):
        rows = slice(g * sub, (g + 1) * sub)
        for jj in range(per_group):
            decode_item(2 * g * per_group + jj)
        parts = []
        for h in range(N_HEADS):
            cs = slice(h * HEAD, (h + 1) * HEAD)
            parts.append((_rms(or_ref[rows, cs].astype(F32), gn_ref[...]) * gate_ref[rows, cs]).astype(BF16))
        parts.append(oa_ref[rows, :].astype(BF16))
        mix = jnp.concatenate(parts, axis=-1)
        x1 = x_ref[rows, :] + jnp.dot(mix, wo_ref[...], preferred_element_type=F32)
        h2 = _rms(x1, nf_ref[...]).astype(BF16)
        for j in range(d_ff // FF_CHUNK):
            fs = slice(j * FF_CHUNK, (j + 1) * FF_CHUNK)
            gt = jnp.dot(h2, wg_ref[:, fs], preferred_element_type=F32)
            up = jnp.dot(h2, wu_ref[:, fs], preferred_element_type=F32)
            act_scr[rows, fs] = (gt * _sigmoid(gt) * up).astype(BF16)
        for jj in range(per_group):
            decode_item((2 * g + 1) * per_group + jj)
        y = x1 + jnp.dot(act_scr[rows, :], wd_ref[...], preferred_element_type=F32)
        y_ref[rows, :] = _rms(y, nfin_ref[...])


def _tail(x, o_r, gate, o_a, r_gnorm, wo_bf, norm_ffn, wg_bf, wu_bf, wd_bf, norm_final, tm, decode=None):
    rows, d = x.shape
    d_ff = wg_bf.shape[1]
    steps = rows // tm

    def resident(shape):
        return pl.BlockSpec(shape, lambda i, *_: (0, 0), pipeline_mode=pl.Buffered(1))

    def const(shape):
        return pl.BlockSpec(shape, lambda i, *_: (0,) * len(shape))

    def row_block(width):
        return pl.BlockSpec((tm, width), lambda i, *_: (i, 0))

    in_specs = [row_block(d), row_block(GROUP), row_block(GROUP), row_block(GROUP), const((1, HEAD)),
                resident((d, d)), const((1, d)), resident((d, d_ff)), resident((d, d_ff)), resident((d_ff, d)),
                const((1, d))]
    operands = [x, o_r, gate, o_a, r_gnorm, wo_bf, norm_ffn, wg_bf, wu_bf, wd_bf, norm_final]
    out_specs = [row_block(d)]
    out_shape = [jax.ShapeDtypeStruct((rows, d), F32)]
    scratch = [pltpu.VMEM((tm, d_ff), BF16)]
    config = None
    prefetch = []
    if decode is not None:
        lam_init, page_table, aq_bf, ak_bf, av, lams, subln4, cache_k, cache_v = decode
        n, n_pages = page_table.shape
        n_phys, page_size = cache_k.shape[0], cache_k.shape[1]
        items = n // steps
        assert items * steps == n and items % (2 * (tm // min(tm, TAIL_ROWS))) == 0
        ck = jnp.transpose(cache_k, (0, 2, 3, 4, 1)).reshape(n_phys, GROUP, page_size)
        cv = cache_v.reshape(n_phys, page_size * N_HEADS, HEAD)
        item_block = pl.BlockSpec((items, 1, GROUP), lambda i, *_: (i, 0, 0))
        in_specs += [item_block, item_block, item_block] + [const((1, A_DK))] * 4 + [
            const((1, GROUP)), pl.BlockSpec(memory_space=pl.ANY), pl.BlockSpec(memory_space=pl.ANY)]
        operands += [a.reshape(n, 1, GROUP) for a in (aq_bf, ak_bf, av)] + list(lams) + [subln4, ck, cv]
        out_specs.append(item_block)
        out_shape.append(jax.ShapeDtypeStruct((n, 1, GROUP), F32))
        scratch += [pltpu.VMEM((2, n_pages, GROUP, page_size), F32),
                    pltpu.VMEM((2, n_pages * page_size * N_HEADS, HEAD), F32),
                    pltpu.SemaphoreType.DMA((2, 2))]
        config = (lam_init, n_pages, items)
        prefetch = [page_table.reshape(-1)]
    out = pl.pallas_call(
        functools.partial(_tail_kernel, config),
        grid_spec=pltpu.PrefetchScalarGridSpec(
            num_scalar_prefetch=len(prefetch), grid=(steps,),
            in_specs=in_specs, out_specs=out_specs, scratch_shapes=scratch),
        out_shape=out_shape,
        compiler_params=pltpu.CompilerParams(
            dimension_semantics=("arbitrary",), vmem_limit_bytes=TAIL_VMEM_LIMIT),
        name="tail",
    )(*prefetch, *operands)
    if decode is None:
        return out[0]
    return out[0], out[1].reshape(n, GROUP)


def kernel(x_prompt, x_sample, cache_k, cache_v, state_hgrn, page_table, w_in, w_out, lb_param,
           r_gnorm, lam_q1, lam_k1, lam_q2, lam_k2, a_subln, norm_mix, norm_ffn, w_gate, w_up,
           w_down, norm_final):
    batch, seq, d = x_prompt.shape
    n_dec = x_sample.shape[0]
    depth = w_in.shape[0]
    assert depth == 1 and x_sample.shape[1] == 1
    hp = x_prompt.reshape(batch * seq, d)
    hs = x_sample.reshape(n_dec, d)
    lb_param = lb_param.astype(F32)
    nfin = norm_final.reshape(1, d)
    kp, vp, sp, kss, vss, sss = [], [], [], [], [], []
    for l in range(depth):
        lam_init = 0.8 - 0.6 * math.exp(-0.3 * l)
        w_in_bf = w_in[l].astype(BF16)
        wo_bf = w_out[l].astype(BF16)
        wg_bf = w_gate[l].astype(BF16)
        wu_bf = w_up[l].astype(BF16)
        wd_bf = w_down[l].astype(BF16)
        nmix = norm_mix[l].reshape(1, d)
        nffn = norm_ffn[l].reshape(1, d)
        gn = r_gnorm[l].reshape(1, HEAD)
        sub = a_subln[l].reshape(1, HEAD)
        lams = [a[l].reshape(1, A_DK).astype(F32) for a in (lam_q1, lam_k1, lam_q2, lam_k2)]

        qr_s, logf_s, kr_s, vr_s, gate_s, aq_s, akt_s, av_s, ak_s = _inproj(
            hs, nmix, w_in_bf, lb_param, l, n_dec, n_dec, prompt=False)
        o_r_s, s_new_s = _hgrn_sample(qr_s, logf_s, kr_s, vr_s, state_hgrn[l].astype(F32))

        assert IN_ROWS == HGRN_BLOCK
        qr, cum_logf, kr, vr, gate, aq_bf, akt, av_heads, akt_bf, av_bf, worst = _inproj(
            hp, nmix, w_in_bf, lb_param, l, ROW_BLOCK, seq, prompt=True)
        o_r, s_new = _hgrn_prompt(qr, cum_logf, kr, vr, worst, batch, seq)
        o_a = _attn_prompt(aq_bf, akt_bf, av_bf, *lams, sub, lam_init, batch, seq)
        paged = (lam_init, page_table, aq_s, ak_s, av_s, lams, jnp.tile(sub, (1, N_HEADS)),
                 cache_k[l], cache_v[l])
        hp, o_a_s = _tail(hp, o_r, gate, o_a, gn, wo_bf, nffn, wg_bf, wu_bf, wd_bf, nfin, TAIL_BLOCK, paged)
        kp.append(jnp.transpose(akt.reshape(batch, N_HEADS, 2, A_DK, seq), (0, 4, 1, 2, 3)))
        vp.append(av_heads.reshape(batch, seq, N_HEADS, HEAD))
        sp.append(s_new)

        hs = _tail(hs, o_r_s, gate_s, o_a_s, gn, wo_bf, nffn, wg_bf, wu_bf, wd_bf, nfin, n_dec)
        kss.append(jnp.transpose(akt_s.reshape(N_HEADS, 2, A_DK, n_dec), (3, 0, 1, 2))[:, None])
        vss.append(av_s.reshape(n_dec, 1, N_HEADS, HEAD))
        sss.append(s_new_s)

    y_prompt = hp.reshape(batch, seq, d)
    y_sample = hs.reshape(n_dec, 1, d)
    return (y_prompt, y_sample, jnp.stack(kp), jnp.stack(vp), jnp.stack(sp),
            jnp.stack(kss), jnp.stack(vss), jnp.stack(sss))
```

```python
import functools
import math

import jax
import jax.numpy as jnp
from jax import lax
from jax.experimental import pallas as pl
from jax.experimental.pallas import tpu as pltpu

F32 = jnp.float32
BF16 = jnp.bfloat16

EPS = 1e-6
LOG2E = math.log2(math.e)
HEAD = 128
N_HEADS = 4
A_DK = 64
GROUP = N_HEADS * HEAD
N_PIECES = 7
SUBLANES = 8
HGRN_CHUNK = 128
HGRN_BLOCK = 256
HGRN_SAFE_EXPONENT = 60.0
ATT_BLOCK = 2048
ATT_KEY_BLOCK = 2048
ATT_ROWS = 128
ROW_BLOCK = 512
IN_ROWS = 256
TAIL_BLOCK = 256
TAIL_ROWS = 256
TAIL_VMEM_LIMIT = 56 * 1024 * 1024
FF_CHUNK = 256
VMEM_LIMIT = 48 * 1024 * 1024


def _sigmoid(x):
    return 1.0 / (1.0 + jnp.exp(-x))


def _rms(x, w):
    return x * lax.rsqrt(jnp.mean(x * x, axis=-1, keepdims=True) + EPS) * w


def _const_spec(shape):
    return pl.BlockSpec(shape, lambda *_: (0,) * len(shape))


def _inproj_kernel(layer, prompt, x_ref, nw_ref, w_ref, lb_ref,
                   qr_ref, logf_ref, kr_ref, vr_ref, gate_ref, aq_ref, akt_ref, *kv_refs):
    lbp = lb_ref[...]
    e = jnp.exp(lbp - jnp.max(lbp, axis=0, keepdims=True))
    lb = jnp.sum(e[:layer + 1], axis=0, keepdims=True) / jnp.sum(e, axis=0, keepdims=True)

    tm = x_ref.shape[0]
    sub = min(tm, IN_ROWS)
    for g in range(tm // sub):
        rows = slice(g * sub, (g + 1) * sub)
        h = _rms(x_ref[rows, :], nw_ref[...]).astype(BF16)

        def piece(j, h=h):
            return jnp.dot(h, w_ref[:, j * GROUP:(j + 1) * GROUP], preferred_element_type=F32)

        rq = piece(0)
        qr_ref[rows, :] = (rq * _sigmoid(rq)).astype(qr_ref.dtype)
        f = lb + (1.0 - lb) * _sigmoid(piece(1))
        logf = jnp.log(f)
        if prompt:
            c, half = HGRN_CHUNK, HGRN_CHUNK // 2
            worst = jnp.zeros((1, GROUP), F32)
            for ci in range(sub // c):
                b = _cumsum_rows(logf[ci * c:(ci + 1) * c])
                logf_ref[g * sub + ci * c:g * sub + (ci + 1) * c, :] = b
                b_mid = b[half - 1:half, :]
                worst = jnp.maximum(worst, jnp.maximum(-b_mid, b_mid - b[c - 1:c, :]))
            worst = functools.reduce(jnp.maximum,
                                     [worst[:, hd * HEAD:(hd + 1) * HEAD] for hd in range(N_HEADS)])
        else:
            logf_ref[rows, :] = logf
        kr_ref[rows, :] = (1.0 - f).astype(kr_ref.dtype)
        vr_ref[rows, :] = piece(2).astype(vr_ref.dtype)
        rg = piece(3)
        gate_ref[rows, :] = (rg * _sigmoid(rg)).astype(gate_ref.dtype)
        aq_ref[rows, :] = (piece(4) * (A_DK ** -0.5 * LOG2E)).astype(BF16)
        ak = piece(5)
        akt = ak.T
        akt_ref[0, :, rows] = akt
        av = piece(6)
        if prompt:
            avh_ref, aktb_ref, avb_ref, worst_ref = kv_refs
            worst_ref[g] = jnp.broadcast_to(worst, (SUBLANES, HEAD))
            for hd in range(N_HEADS):
                avh_ref[pl.ds(g * sub * N_HEADS + hd, sub, stride=N_HEADS), :] = av[:, hd * HEAD:(hd + 1) * HEAD]
            aktb_ref[0, :, rows] = akt.astype(BF16)
            avb_ref[rows, :] = av.astype(BF16)
        else:
            av_ref, akb_ref = kv_refs
            av_ref[rows, :] = av
            akb_ref[rows, :] = ak.astype(BF16)


def _inproj(x, norm_w, w_bf, lb_param, layer, tm, seq, prompt):
    rows, d = x.shape
    d_in = w_bf.shape[1]
    per_seq = seq // tm
    f32_out = jax.ShapeDtypeStruct((rows, GROUP), F32)
    bf_out = jax.ShapeDtypeStruct((rows, GROUP), BF16)
    row_spec = pl.BlockSpec((tm, GROUP), lambda i: (i, 0))
    kt_spec = pl.BlockSpec((1, GROUP, tm), lambda i: (i // per_seq, 0, i % per_seq))
    kt_shape = (rows // seq, GROUP, seq)
    if prompt:
        narrow = bf_out
        sub = min(tm, IN_ROWS)
        assert sub % HGRN_CHUNK == 0 and tm % sub == 0
        kv_specs = [pl.BlockSpec((tm * N_HEADS, HEAD), lambda i: (i, 0)), kt_spec, row_spec,
                    pl.BlockSpec((tm // sub, SUBLANES, HEAD), lambda i: (i, 0, 0))]
        kv_shapes = [jax.ShapeDtypeStruct((rows * N_HEADS, HEAD), F32),
                     jax.ShapeDtypeStruct(kt_shape, BF16), bf_out,
                     jax.ShapeDtypeStruct((rows // sub, SUBLANES, HEAD), F32)]
    else:
        narrow = f32_out
        kv_specs = [row_spec, row_spec]
        kv_shapes = [f32_out, bf_out]
    return pl.pallas_call(
        functools.partial(_inproj_kernel, layer, prompt),
        grid=(rows // tm,),
        in_specs=[pl.BlockSpec((tm, d), lambda i: (i, 0)),
                  _const_spec((1, d)),
                  pl.BlockSpec((d, d_in), lambda i: (0, 0), pipeline_mode=pl.Buffered(1)),
                  _const_spec(lb_param.shape)],
        out_specs=[row_spec] * 6 + [kt_spec] + kv_specs,
        out_shape=[narrow, f32_out, narrow, narrow, narrow, bf_out,
                   jax.ShapeDtypeStruct(kt_shape, F32)] + kv_shapes,
        compiler_params=pltpu.CompilerParams(
            dimension_semantics=("parallel",), vmem_limit_bytes=VMEM_LIMIT),
        name="inproj",
    )(x, norm_w, w_bf, lb_param)


def _cumsum_rows(x):
    n, lanes = x.shape
    groups = n // SUBLANES
    x3 = x.reshape(groups, SUBLANES, lanes)
    sub = lax.broadcasted_iota(jnp.int32, x3.shape, 1)
    shift = 1
    while shift < SUBLANES:
        x3 = x3 + jnp.where(sub >= shift, pltpu.roll(x3, shift, 1), 0.0)
        shift *= 2
    totals = x3[:, SUBLANES - 1:SUBLANES, :]
    pre = totals
    shift = 1
    while shift < groups:
        pre = pre + jnp.concatenate([jnp.zeros((shift, 1, lanes), F32), pre[:groups - shift]], axis=0)
        shift *= 2
    return (x3 + (pre - totals)).reshape(n, lanes)


def _hgrn_sample_step(q_ref, g_ref, k_ref, v_ref, s_in_ref, o_ref, s_out_ref):
    n = q_ref.shape[1]
    assert n <= SUBLANES

    def columns(x):
        return jnp.concatenate([x] * (SUBLANES // n), axis=0).T

    for h in range(N_HEADS):
        cs = slice(h * HEAD, (h + 1) * HEAD)
        qt = columns(q_ref[0, :, cs])
        ft = columns(jnp.exp(g_ref[0, :, cs]))
        kt = columns(k_ref[0, :, cs])
        for j in range(n):
            s_new = ft[:, j:j + 1] * s_in_ref[j, h] + kt[:, j:j + 1] * v_ref[0, j:j + 1, cs]
            s_out_ref[j, h] = s_new
            o_ref[0, j:j + 1, cs] = jnp.sum(qt[:, j:j + 1] * s_new, axis=0, keepdims=True)


def _hgrn_prompt_kernel(q_ref, b_ref, k_ref, v_ref, worst_ref, qs_ref, gs_ref, ks_ref, vs_ref, ss_in_ref,
                        o_ref, s_ref, os_ref, ss_out_ref, st_scr, oi_scr):
    i = pl.program_id(0)
    sample_step = functools.partial(_hgrn_sample_step, qs_ref, gs_ref, ks_ref, vs_ref, ss_in_ref,
                                    os_ref, ss_out_ref)

    @pl.when(i == 0)
    def _():
        st_scr[...] = jnp.zeros_like(st_scr)

    c = HGRN_CHUNK
    half = c // 2
    row = lax.broadcasted_iota(jnp.int32, (c, c), 0)
    col = lax.broadcasted_iota(jnp.int32, (c, c), 1)
    tri = col <= row
    nt = (((1,), (1,)), ((), ()))
    tn = (((0,), (0,)), ((), ()))
    chains = [(bi, ci, h) for bi in range(q_ref.shape[0]) for ci in range(HGRN_BLOCK // c)
              for h in range(N_HEADS)]

    def window(bi, ci, h):
        return bi, slice(ci * c, (ci + 1) * c), slice(h * HEAD, (h + 1) * HEAD)

    safe = jnp.max(worst_ref[...]) < HGRN_SAFE_EXPONENT

    def finish(chain, intra, q0, k2, v, b_last):
        bi, _, h = chain
        st = st_scr[bi, h]
        o = lax.dot_general(q0, st.astype(BF16), nt, preferred_element_type=F32) + intra
        o_ref[window(*chain)] = o.astype(o_ref.dtype)
        st_scr[bi, h] = st * jnp.exp(b_last) + lax.dot_general(v, k2, tn, preferred_element_type=F32)

    @pl.when(safe)
    def _():
        sample_step()
        for chain in chains:
            w = window(*chain)
            b = b_ref[w]
            b_mid = b[half - 1:half, :]
            b_last = b[c - 1:c, :]
            d = b - b_mid
            qm = q_ref[w] * jnp.exp(d)
            km = k_ref[w] * jnp.exp(-d)
            q0 = (qm * jnp.exp(b_mid)).astype(BF16)
            k2 = (km * jnp.exp(b_last - b_mid)).astype(BF16)
            v = v_ref[w].astype(BF16)
            a = lax.dot_general(qm.astype(BF16), km.astype(BF16), nt, preferred_element_type=F32)
            a = jnp.where(tri, a, 0.0).astype(BF16)
            finish(chain, jnp.dot(a, v, preferred_element_type=F32), q0, k2, v, b_last)

    @pl.when(jnp.logical_not(safe))
    def _():
        sample_step()
        srow = lax.broadcasted_iota(jnp.int32, (c, HEAD), 0)
        for chain in chains:
            bi, rs, cs = window(*chain)
            b = b_ref[bi, rs, cs]
            b_last = b[c - 1:c, :]
            kk = k_ref[bi, rs, cs]
            vv = v_ref[bi, rs, cs]

            def query_rows(tg, carry, bi=bi, rs=rs, cs=cs, b=b, kk=kk, vv=vv):
                base = pl.multiple_of(tg * SUBLANES, SUBLANES)
                at = pl.ds(rs.start + base, SUBLANES)
                b8 = b_ref[bi, at, cs]
                q8 = q_ref[bi, at, cs]
                out = []
                for j in range(SUBLANES):
                    decay = jnp.exp(jnp.where(srow <= base + j, b8[j:j + 1, :] - b, -jnp.inf))
                    a_col = jnp.sum(q8[j:j + 1, :] * kk * decay, axis=-1, keepdims=True)
                    out.append(jnp.sum(a_col * vv, axis=0, keepdims=True))
                oi_scr[pl.ds(base, SUBLANES), :] = jnp.concatenate(out, axis=0)
                return carry

            lax.fori_loop(0, c // SUBLANES, query_rows, 0)
            q0 = (q_ref[bi, rs, cs] * jnp.exp(b)).astype(BF16)
            k2 = (kk * jnp.exp(b_last - b)).astype(BF16)
            finish(chain, oi_scr[...], q0, k2, vv.astype(BF16), b_last)

    @pl.when(i == pl.num_programs(0) - 1)
    def _():
        for bi in range(q_ref.shape[0]):
            for h in range(N_HEADS):
                s_ref[bi, h] = st_scr[bi, h].T


def _hgrn(qr, cum_logf, kr, vr, worst, batch, seq, qr_s, logf_s, kr_s, vr_s, state_s):
    nb = seq // HGRN_BLOCK
    n_dec = qr_s.shape[0]
    per_step = n_dec // nb
    assert per_step * nb == n_dec
    spec = pl.BlockSpec((batch, HGRN_BLOCK, GROUP), lambda i: (0, i, 0))
    shape3 = (batch, seq, GROUP)
    rows_s = pl.BlockSpec((1, per_step, GROUP), lambda i: (i, 0, 0))
    state_spec = pl.BlockSpec((per_step, N_HEADS, HEAD, HEAD), lambda i: (i, 0, 0, 0))
    o_r, s_new, o_r_s, s_new_s = pl.pallas_call(
        _hgrn_prompt_kernel,
        grid=(nb,),
        in_specs=[spec] * 4 + [pl.BlockSpec((batch, 1, SUBLANES, HEAD), lambda i: (0, i, 0, 0))]
                 + [rows_s] * 4 + [state_spec],
        out_specs=[spec, pl.BlockSpec((batch, N_HEADS, HEAD, HEAD), lambda i: (0, 0, 0, 0)), rows_s, state_spec],
        out_shape=[jax.ShapeDtypeStruct(shape3, BF16),
                   jax.ShapeDtypeStruct((batch, N_HEADS, HEAD, HEAD), F32),
                   jax.ShapeDtypeStruct((nb, per_step, GROUP), F32),
                   jax.ShapeDtypeStruct(state_s.shape, F32)],
        scratch_shapes=[pltpu.VMEM((batch, N_HEADS, HEAD, HEAD), F32),
                        pltpu.VMEM((HGRN_CHUNK, HEAD), F32)],
        compiler_params=pltpu.CompilerParams(
            dimension_semantics=("arbitrary",), vmem_limit_bytes=VMEM_LIMIT),
        name="hgrn",
    )(*(a.reshape(shape3) for a in (qr, cum_logf, kr, vr)), worst.reshape(batch, nb, SUBLANES, HEAD),
      *(a.reshape(nb, per_step, GROUP) for a in (qr_s, logf_s, kr_s, vr_s)), state_s)
    return o_r.reshape(batch * seq, GROUP), s_new, o_r_s.reshape(n_dec, GROUP), s_new_s


def _lambda(lq1_ref, lk1_ref, lq2_ref, lk2_ref, lam_init):
    a = jnp.sum(lq1_ref[...] * lk1_ref[...], axis=-1, keepdims=True)
    b = jnp.sum(lq2_ref[...] * lk2_ref[...], axis=-1, keepdims=True)
    return jnp.exp(a) - jnp.exp(b) + lam_init


ATT_FULL, ATT_FULL_DIAG, ATT_DIAG = 0, 1, 2


def _attn_prompt_kernel(lam_init, qi_tab, kb_tab, kind_tab, first_tab,
                        q_ref, kt_ref, v_ref, lq1_ref, lk1_ref, lq2_ref, lk2_ref, sub_ref,
                        o_ref, m_scr, acc_scr):
    p = pl.program_id(2)
    kind = kind_tab[p]

    @pl.when(first_tab[p] == 1)
    def _():
        m_scr[...] = jnp.full_like(m_scr, -jnp.inf)
        acc_scr[...] = jnp.zeros_like(acc_scr)

    tq, tk = q_ref.shape[0], v_ref.shape[0]
    half_chunks = tq // HEAD

    def sweep(step_kind):
        masked = step_kind != ATT_FULL
        v_ext = jnp.concatenate([v_ref[...], jnp.ones((tk, HEAD), BF16)], axis=-1)
        lane = lax.broadcasted_iota(jnp.int32, (ATT_ROWS, HEAD), 1)
        row = lax.broadcasted_iota(jnp.int32, (ATT_ROWS, HEAD), 0)
        zero = jnp.zeros((ATT_ROWS, HEAD), BF16)
        group_chunks = ATT_ROWS // HEAD
        groups = range(tq // ATT_ROWS)
        for r in (reversed(groups) if masked else groups):
            rows = slice(r * ATT_ROWS, (r + 1) * ATT_ROWS)
            q = q_ref[rows, :]
            diag_chunks = (r + 1) * group_chunks
            n_chunks = {ATT_FULL: tk // HEAD, ATT_FULL_DIAG: half_chunks + diag_chunks,
                        ATT_DIAG: diag_chunks}[step_kind]
            kt = kt_ref[0, :, 0:n_chunks * HEAD]
            probs, alphas = [], []
            for mi, qm in enumerate((jnp.where(lane < A_DK, q, zero), jnp.where(lane >= A_DK, q, zero))):
                s = jnp.dot(qm, kt, preferred_element_type=F32)
                chunks = [s[:, c * HEAD:(c + 1) * HEAD] for c in range(n_chunks)]
                if masked:
                    for g in range(group_chunks):
                        c = n_chunks - group_chunks + g
                        chunks[c] = jnp.where(lane + g * HEAD <= row, chunks[c], -jnp.inf)
                mc = functools.reduce(jnp.maximum, chunks)
                m_prev = m_scr[mi, rows]
                m_new = jnp.maximum(m_prev, jnp.max(mc, axis=-1, keepdims=True))
                alphas.append(jnp.exp2(m_prev - m_new))
                probs.append(jnp.concatenate([jnp.exp2(sc - m_new).astype(BF16) for sc in chunks], axis=-1))
                m_scr[mi, rows] = m_new
            pv = jnp.dot(jnp.concatenate(probs, axis=0), v_ext[0:n_chunks * HEAD], preferred_element_type=F32)
            for mi, alpha in enumerate(alphas):
                acc_scr[mi, rows] = (jnp.concatenate([alpha, alpha], axis=-1) * acc_scr[mi, rows]
                                     + pv[mi * ATT_ROWS:(mi + 1) * ATT_ROWS])

    def finalize():
        lam = _lambda(lq1_ref, lk1_ref, lq2_ref, lk2_ref, lam_init)
        a0 = acc_scr[0]
        a1 = acc_scr[1]
        o = a0[:, :HEAD] / a0[:, HEAD:] - lam * (a1[:, :HEAD] / a1[:, HEAD:])
        o_ref[...] = (_rms(o, sub_ref[...]) * (1.0 - lam_init)).astype(o_ref.dtype)

    @pl.when(kind == ATT_FULL)
    def _():
        sweep(ATT_FULL)

    if tk > tq:
        @pl.when(kind == ATT_FULL_DIAG)
        def _():
            sweep(ATT_FULL_DIAG)
            finalize()

    @pl.when(kind == ATT_DIAG)
    def _():
        sweep(ATT_DIAG)
        finalize()


def _attn_prompt(aq_bf, akt_bf, av_bf, lq1, lk1, lq2, lk2, subln, lam_init, batch, seq):
    t = ATT_BLOCK
    ratio = ATT_KEY_BLOCK // t
    nq = seq // t
    assert ATT_ROWS % HEAD == 0 and ratio in (1, 2) and nq % ratio == 0
    steps = []
    for i in range(nq):
        for j in range(i // ratio):
            steps.append((i, j, ATT_FULL, int(j == 0)))
        steps.append((i, i // ratio, ATT_FULL_DIAG if i % ratio else ATT_DIAG, int(i < ratio)))
    qi_tab, kb_tab, kind_tab, first_tab = (jnp.asarray([s[c] for s in steps], jnp.int32) for c in range(4))
    q_spec = pl.BlockSpec((t, HEAD), lambda b, h, p, qi, kb, kind, first: (b * nq + qi[p], h))
    v_spec = pl.BlockSpec((ratio * t, HEAD),
                          lambda b, h, p, qi, kb, kind, first: (b * (nq // ratio) + kb[p], h))
    kt_spec = pl.BlockSpec((1, HEAD, ratio * t), lambda b, h, p, qi, kb, kind, first: (b, h, kb[p]))
    lam_spec = pl.BlockSpec((1, A_DK), lambda *_: (0, 0))
    sub_spec = pl.BlockSpec((1, HEAD), lambda *_: (0, 0))
    return pl.pallas_call(
        functools.partial(_attn_prompt_kernel, lam_init),
        grid_spec=pltpu.PrefetchScalarGridSpec(
            num_scalar_prefetch=4,
            grid=(batch, N_HEADS, len(steps)),
            in_specs=[q_spec, kt_spec, v_spec, lam_spec, lam_spec, lam_spec, lam_spec, sub_spec],
            out_specs=q_spec,
            scratch_shapes=[pltpu.VMEM((2, t, HEAD), F32), pltpu.VMEM((2, t, 2 * HEAD), F32)]),
        out_shape=jax.ShapeDtypeStruct((batch * seq, GROUP), BF16),
        compiler_params=pltpu.CompilerParams(
            dimension_semantics=("parallel", "parallel", "arbitrary"), vmem_limit_bytes=VMEM_LIMIT),
        name="attn_prompt",
    )(qi_tab, kb_tab, kind_tab, first_tab, aq_bf, akt_bf, av_bf, lq1, lk1, lq2, lk2, subln)


def _page_copies(pt_ref, item, n_pages, ck_hbm, cv_hbm, kbuf, vbuf, sem, slot):
    page_rows = ck_hbm.shape[1]
    out = []
    for j in range(n_pages):
        page = pt_ref[item * n_pages + j]
        out.append(pltpu.make_async_copy(ck_hbm.at[page], kbuf.at[slot, j], sem.at[slot, 0]))
        out.append(pltpu.make_async_copy(cv_hbm.at[page], vbuf.at[slot, pl.ds(j * page_rows, page_rows)],
                                         sem.at[slot, 1]))
    return out


def _decode_item(lam, lam_init, q, kn, vn, sub4, kpages, vrows):
    n_pages = kpages.shape[0]
    past = n_pages * kpages.shape[2]
    rows = 2 * N_HEADS
    ri = lax.broadcasted_iota(jnp.int32, (rows, GROUP), 0)
    li = lax.broadcasted_iota(jnp.int32, (rows, GROUP), 1)
    mine = (li // A_DK) == (ri % N_HEADS) * 2 + ri // N_HEADS
    qbd = jnp.where(mine, jnp.broadcast_to(q.astype(F32), (rows, GROUP)), 0.0)
    qbd_bf = qbd.astype(BF16)
    s = jnp.concatenate(
        [jnp.dot(qbd_bf, kpages[j].astype(BF16), preferred_element_type=F32) for j in range(n_pages)],
        axis=-1)
    s_new = jnp.sum(qbd * kn.astype(F32), axis=-1, keepdims=True)
    m = jnp.maximum(jnp.max(s, axis=-1, keepdims=True), s_new)
    p = jnp.exp2(s - m)
    p_new = jnp.exp2(s_new - m)
    inv_l = 1.0 / (jnp.sum(p, axis=-1, keepdims=True) + p_new)
    pn = p * inv_l
    pn_new = p_new * inv_l
    attn = pn - lam * pltpu.roll(pn, N_HEADS, 0)
    attn_new = pn_new - lam * pltpu.roll(pn_new, N_HEADS, 0)
    attn_bf = attn.astype(BF16)
    o = jnp.concatenate(
        [jnp.dot(attn_bf, vrows[pl.ds(h, past, stride=N_HEADS), :].astype(BF16), preferred_element_type=F32)
         for h in range(N_HEADS)], axis=-1) + attn_new * vn
    keep = jnp.logical_and(ri < N_HEADS, li // HEAD == ri)
    o = jnp.where(keep, o, 0.0)
    ms = jnp.sum(o * o, axis=-1, keepdims=True) * (1.0 / HEAD)
    o = o * lax.rsqrt(ms + EPS) * sub4 * (1.0 - lam_init)
    return jnp.sum(o, axis=0, keepdims=True)


def _tail_kernel(decode, *refs):
    if decode is None:
        (x_ref, or_ref, gate_ref, oa_ref, gn_ref, wo_ref, nf_ref, wg_ref, wu_ref, wd_ref, nfin_ref,
         y_ref, act_scr) = refs
    else:
        (pt_ref, x_ref, or_ref, gate_ref, oa_ref, gn_ref, wo_ref, nf_ref, wg_ref, wu_ref, wd_ref, nfin_ref,
         q_ref, kn_ref, vn_ref, lq1_ref, lk1_ref, lq2_ref, lk2_ref, sub4_ref, ck_hbm, cv_hbm,
         y_ref, os_ref, act_scr, kbuf, vbuf, sem) = refs
        lam_init, n_pages, items = decode
        step = pl.program_id(0)
        copies = functools.partial(_page_copies, pt_ref, n_pages=n_pages, ck_hbm=ck_hbm, cv_hbm=cv_hbm,
                                   kbuf=kbuf, vbuf=vbuf, sem=sem)

        @pl.when(step == 0)
        def _():
            for cp in copies(item=0, slot=0):
                cp.start()

    def decode_item(jj):
        if decode is None:
            return
        item = step * items + jj
        slot = jj % 2
        if jj + 1 < items:
            for cp in copies(item=item + 1, slot=1 - slot):
                cp.start()
        else:
            @pl.when(step + 1 < pl.num_programs(0))
            def _():
                for cp in copies(item=item + 1, slot=1 - slot):
                    cp.start()
        for cp in copies(item=item, slot=slot):
            cp.wait()
        lam = _lambda(lq1_ref, lk1_ref, lq2_ref, lk2_ref, lam_init)
        os_ref[jj] = _decode_item(lam, lam_init, q_ref[jj], kn_ref[jj], vn_ref[jj], sub4_ref[...],
                                  kbuf.at[slot], vbuf.at[slot])

    tm = x_ref.shape[0]
    d_ff = wg_ref.shape[1]
    sub = min(tm, TAIL_ROWS)
    groups = tm // sub
    per_group = 0 if decode is None else items // (2 * groups)
    for g in range(groups):
        rows = slice(g * sub, (g + 1) * sub)
        for jj in range(per_group):
            decode_item(2 * g * per_group + jj)
        parts = []
        for h in range(N_HEADS):
            cs = slice(h * HEAD, (h + 1) * HEAD)
            parts.append((_rms(or_ref[rows, cs].astype(F32), gn_ref[...]) * gate_ref[rows, cs]).astype(BF16))
        parts.append(oa_ref[rows, :].astype(BF16))
        mix = jnp.concatenate(parts, axis=-1)
        x1 = x_ref[rows, :] + jnp.dot(mix, wo_ref[...], preferred_element_type=F32)
        h2 = _rms(x1, nf_ref[...]).astype(BF16)
        for j in range(d_ff // FF_CHUNK):
            fs = slice(j * FF_CHUNK, (j + 1) * FF_CHUNK)
            gt = jnp.dot(h2, wg_ref[:, fs], preferred_element_type=F32)
            up = jnp.dot(h2, wu_ref[:, fs], preferred_element_type=F32)
            act_scr[rows, fs] = (gt * _sigmoid(gt) * up).astype(BF16)
        for jj in range(per_group):
            decode_item((2 * g + 1) * per_group + jj)
        y = x1 + jnp.dot(act_scr[rows, :], wd_ref[...], preferred_element_type=F32)
        y_ref[rows, :] = _rms(y, nfin_ref[...])


def _tail(x, o_r, gate, o_a, r_gnorm, wo_bf, norm_ffn, wg_bf, wu_bf, wd_bf, norm_final, tm, decode=None):
    rows, d = x.shape
    d_ff = wg_bf.shape[1]
    steps = rows // tm

    def resident(shape):
        return pl.BlockSpec(shape, lambda i, *_: (0, 0), pipeline_mode=pl.Buffered(1))

    def const(shape):
        return pl.BlockSpec(shape, lambda i, *_: (0,) * len(shape))

    def row_block(width):
        return pl.BlockSpec((tm, width), lambda i, *_: (i, 0))

    in_specs = [row_block(d), row_block(GROUP), row_block(GROUP), row_block(GROUP), const((1, HEAD)),
                resident((d, d)), const((1, d)), resident((d, d_ff)), resident((d, d_ff)), resident((d_ff, d)),
                const((1, d))]
    operands = [x, o_r, gate, o_a, r_gnorm, wo_bf, norm_ffn, wg_bf, wu_bf, wd_bf, norm_final]
    out_specs = [row_block(d)]
    out_shape = [jax.ShapeDtypeStruct((rows, d), F32)]
    scratch = [pltpu.VMEM((tm, d_ff), BF16)]
    config = None
    prefetch = []
    if decode is not None:
        lam_init, page_table, aq_bf, ak_bf, av, lams, subln4, cache_k, cache_v = decode
        n, n_pages = page_table.shape
        n_phys, page_size = cache_k.shape[0], cache_k.shape[1]
        items = n // steps
        assert items * steps == n and items % (2 * (tm // min(tm, TAIL_ROWS))) == 0
        ck = jnp.transpose(cache_k, (0, 2, 3, 4, 1)).reshape(n_phys, GROUP, page_size)
        cv = cache_v.reshape(n_phys, page_size * N_HEADS, HEAD)
        item_block = pl.BlockSpec((items, 1, GROUP), lambda i, *_: (i, 0, 0))
        in_specs += [item_block, item_block, item_block] + [const((1, A_DK))] * 4 + [
            const((1, GROUP)), pl.BlockSpec(memory_space=pl.ANY), pl.BlockSpec(memory_space=pl.ANY)]
        operands += [a.reshape(n, 1, GROUP) for a in (aq_bf, ak_bf, av)] + list(lams) + [subln4, ck, cv]
        out_specs.append(item_block)
        out_shape.append(jax.ShapeDtypeStruct((n, 1, GROUP), F32))
        scratch += [pltpu.VMEM((2, n_pages, GROUP, page_size), F32),
                    pltpu.VMEM((2, n_pages * page_size * N_HEADS, HEAD), F32),
                    pltpu.SemaphoreType.DMA((2, 2))]
        config = (lam_init, n_pages, items)
        prefetch = [page_table.reshape(-1)]
    out = pl.pallas_call(
        functools.partial(_tail_kernel, config),
        grid_spec=pltpu.PrefetchScalarGridSpec(
            num_scalar_prefetch=len(prefetch), grid=(steps,),
            in_specs=in_specs, out_specs=out_specs, scratch_shapes=scratch),
        out_shape=out_shape,
        compiler_params=pltpu.CompilerParams(
            dimension_semantics=("arbitrary",), vmem_limit_bytes=TAIL_VMEM_LIMIT),
        name="tail",
    )(*prefetch, *operands)
    if decode is None:
        return out[0]
    return out[0], out[1].reshape(n, GROUP)


def kernel(x_prompt, x_sample, cache_k, cache_v, state_hgrn, page_table, w_in, w_out, lb_param,
           r_gnorm, lam_q1, lam_k1, lam_q2, lam_k2, a_subln, norm_mix, norm_ffn, w_gate, w_up,
           w_down, norm_final):
    batch, seq, d = x_prompt.shape
    n_dec = x_sample.shape[0]
    depth = w_in.shape[0]
    assert depth == 1 and x_sample.shape[1] == 1
    hp = x_prompt.reshape(batch * seq, d)
    hs = x_sample.reshape(n_dec, d)
    lb_param = lb_param.astype(F32)
    nfin = norm_final.reshape(1, d)
    kp, vp, sp, kss, vss, sss = [], [], [], [], [], []
    for l in range(depth):
        lam_init = 0.8 - 0.6 * math.exp(-0.3 * l)
        w_in_bf = w_in[l].astype(BF16)
        wo_bf = w_out[l].astype(BF16)
        wg_bf = w_gate[l].astype(BF16)
        wu_bf = w_up[l].astype(BF16)
        wd_bf = w_down[l].astype(BF16)
        nmix = norm_mix[l].reshape(1, d)
        nffn = norm_ffn[l].reshape(1, d)
        gn = r_gnorm[l].reshape(1, HEAD)
        sub = a_subln[l].reshape(1, HEAD)
        lams = [a[l].reshape(1, A_DK).astype(F32) for a in (lam_q1, lam_k1, lam_q2, lam_k2)]

        qr_s, logf_s, kr_s, vr_s, gate_s, aq_s, akt_s, av_s, ak_s = _inproj(
            hs, nmix, w_in_bf, lb_param, l, n_dec, n_dec, prompt=False)

        assert IN_ROWS == HGRN_BLOCK
        qr, cum_logf, kr, vr, gate, aq_bf, akt, av_heads, akt_bf, av_bf, worst = _inproj(
            hp, nmix, w_in_bf, lb_param, l, ROW_BLOCK, seq, prompt=True)
        o_r, s_new, o_r_s, s_new_s = _hgrn(qr, cum_logf, kr, vr, worst, batch, seq,
                                           qr_s, logf_s, kr_s, vr_s, state_hgrn[l].astype(F32))
        o_a = _attn_prompt(aq_bf, akt_bf, av_bf, *lams, sub, lam_init, batch, seq)
        paged = (lam_init, page_table, aq_s, ak_s, av_s, lams, jnp.tile(sub, (1, N_HEADS)),
                 cache_k[l], cache_v[l])
        hp, o_a_s = _tail(hp, o_r, gate, o_a, gn, wo_bf, nffn, wg_bf, wu_bf, wd_bf, nfin, TAIL_BLOCK, paged)
        kp.append(jnp.transpose(akt.reshape(batch, N_HEADS, 2, A_DK, seq), (0, 4, 1, 2, 3)))
        vp.append(av_heads.reshape(batch, seq, N_HEADS, HEAD))
        sp.append(s_new)

        hs = _tail(hs, o_r_s, gate_s, o_a_s, gn, wo_bf, nffn, wg_bf, wu_bf, wd_bf, nfin, n_dec)
        kss.append(jnp.transpose(akt_s.reshape(N_HEADS, 2, A_DK, n_dec), (3, 0, 1, 2))[:, None])
        vss.append(av_s.reshape(n_dec, 1, N_HEADS, HEAD))
        sss.append(s_new_s)

    y_prompt = hp.reshape(batch, seq, d)
    y_sample = hs.reshape(n_dec, 1, d)
    return (y_prompt, y_sample, jnp.stack(kp), jnp.stack(vp), jnp.stack(sp),
            jnp.stack(kss), jnp.stack(vss), jnp.stack(sss))
```

```python
import functools
import math

import jax
import jax.numpy as jnp
from jax import lax
from jax.experimental import pallas as pl
from jax.experimental.pallas import tpu as pltpu

F32 = jnp.float32
BF16 = jnp.bfloat16

EPS = 1e-6
LOG2E = math.log2(math.e)
HEAD = 128
N_HEADS = 4
A_DK = 64
GROUP = N_HEADS * HEAD
SUBLANES = 8
HGRN_CHUNK = 128
HGRN_BLOCK = 256
HGRN_SAFE_EXPONENT = 60.0
ATT_BLOCK = 2048
ATT_KEY_BLOCK = 2048
ATT_ROWS = 128
ROW_BLOCK = 512
IN_ROWS = 256
TAIL_BLOCK = 256
TAIL_ROWS = 256
TAIL_VMEM_LIMIT = 56 * 1024 * 1024
FF_CHUNK = 256
VMEM_LIMIT = 48 * 1024 * 1024


def _sigmoid(x):
    return 1.0 / (1.0 + jnp.exp(-x))


def _rms(x, w):
    return x * lax.rsqrt(jnp.mean(x * x, axis=-1, keepdims=True) + EPS) * w


def _const_spec(shape):
    return pl.BlockSpec(shape, lambda *_: (0,) * len(shape))


def _inproj_kernel(layer, prompt, x_ref, nw_ref, w_ref, lb_ref,
                   qr_ref, logf_ref, kr_ref, vr_ref, gate_ref, aq_ref, akt_ref, *kv_refs):
    lbp = lb_ref[...]
    e = jnp.exp(lbp - jnp.max(lbp, axis=0, keepdims=True))
    lb = jnp.sum(e[:layer + 1], axis=0, keepdims=True) / jnp.sum(e, axis=0, keepdims=True)

    tm = x_ref.shape[0]
    sub = min(tm, IN_ROWS)
    for g in range(tm // sub):
        rows = slice(g * sub, (g + 1) * sub)
        h = _rms(x_ref[rows, :], nw_ref[...]).astype(BF16)

        def piece(j, h=h):
            return jnp.dot(h, w_ref[:, j * GROUP:(j + 1) * GROUP], preferred_element_type=F32)

        rq = piece(0)
        qr_ref[rows, :] = (rq * _sigmoid(rq)).astype(qr_ref.dtype)
        f = lb + (1.0 - lb) * _sigmoid(piece(1))
        logf = jnp.log(f)
        if prompt:
            c, half = HGRN_CHUNK, HGRN_CHUNK // 2
            worst = jnp.zeros((1, GROUP), F32)
            for ci in range(sub // c):
                b = _cumsum_rows(logf[ci * c:(ci + 1) * c])
                logf_ref[g * sub + ci * c:g * sub + (ci + 1) * c, :] = b
                b_mid = b[half - 1:half, :]
                worst = jnp.maximum(worst, jnp.maximum(-b_mid, b_mid - b[c - 1:c, :]))
            worst = functools.reduce(jnp.maximum,
                                     [worst[:, hd * HEAD:(hd + 1) * HEAD] for hd in range(N_HEADS)])
        else:
            logf_ref[rows, :] = logf
        kr_ref[rows, :] = (1.0 - f).astype(kr_ref.dtype)
        vr_ref[rows, :] = piece(2).astype(vr_ref.dtype)
        rg = piece(3)
        gate_ref[rows, :] = (rg * _sigmoid(rg)).astype(gate_ref.dtype)
        aq_ref[rows, :] = (piece(4) * (A_DK ** -0.5 * LOG2E)).astype(BF16)
        ak = piece(5)
        akt = ak.T
        akt_ref[0, :, rows] = akt
        av = piece(6)
        if prompt:
            avh_ref, aktb_ref, avb_ref, worst_ref = kv_refs
            worst_ref[g] = jnp.broadcast_to(worst, (SUBLANES, HEAD))
            for hd in range(N_HEADS):
                avh_ref[pl.ds(g * sub * N_HEADS + hd, sub, stride=N_HEADS), :] = av[:, hd * HEAD:(hd + 1) * HEAD]
            aktb_ref[0, :, rows] = akt.astype(BF16)
            avb_ref[rows, :] = av.astype(BF16)
        else:
            av_ref, akb_ref = kv_refs
            av_ref[rows, :] = av
            akb_ref[rows, :] = ak.astype(BF16)


def _inproj(x, norm_w, w_bf, lb_param, layer, tm, seq, prompt):
    rows, d = x.shape
    d_in = w_bf.shape[1]
    per_seq = seq // tm
    f32_out = jax.ShapeDtypeStruct((rows, GROUP), F32)
    bf_out = jax.ShapeDtypeStruct((rows, GROUP), BF16)
    row_spec = pl.BlockSpec((tm, GROUP), lambda i: (i, 0))
    kt_spec = pl.BlockSpec((1, GROUP, tm), lambda i: (i // per_seq, 0, i % per_seq))
    kt_shape = (rows // seq, GROUP, seq)
    if prompt:
        narrow = bf_out
        sub = min(tm, IN_ROWS)
        assert sub % HGRN_CHUNK == 0 and tm % sub == 0
        kv_specs = [pl.BlockSpec((tm * N_HEADS, HEAD), lambda i: (i, 0)), kt_spec, row_spec,
                    pl.BlockSpec((tm // sub, SUBLANES, HEAD), lambda i: (i, 0, 0))]
        kv_shapes = [jax.ShapeDtypeStruct((rows * N_HEADS, HEAD), F32),
                     jax.ShapeDtypeStruct(kt_shape, BF16), bf_out,
                     jax.ShapeDtypeStruct((rows // sub, SUBLANES, HEAD), F32)]
    else:
        narrow = f32_out
        kv_specs = [row_spec, row_spec]
        kv_shapes = [f32_out, bf_out]
    return pl.pallas_call(
        functools.partial(_inproj_kernel, layer, prompt),
        grid=(rows // tm,),
        in_specs=[pl.BlockSpec((tm, d), lambda i: (i, 0)),
                  _const_spec((1, d)),
                  pl.BlockSpec((d, d_in), lambda i: (0, 0), pipeline_mode=pl.Buffered(1)),
                  _const_spec(lb_param.shape)],
        out_specs=[row_spec] * 6 + [kt_spec] + kv_specs,
        out_shape=[narrow, f32_out, narrow, narrow, narrow, bf_out,
                   jax.ShapeDtypeStruct(kt_shape, F32)] + kv_shapes,
        compiler_params=pltpu.CompilerParams(
            dimension_semantics=("parallel",), vmem_limit_bytes=VMEM_LIMIT),
        name="inproj",
    )(x, norm_w, w_bf, lb_param)


def _cumsum_rows(x):
    n, lanes = x.shape
    groups = n // SUBLANES
    x3 = x.reshape(groups, SUBLANES, lanes)
    sub = lax.broadcasted_iota(jnp.int32, x3.shape, 1)
    shift = 1
    while shift < SUBLANES:
        x3 = x3 + jnp.where(sub >= shift, pltpu.roll(x3, shift, 1), 0.0)
        shift *= 2
    totals = x3[:, SUBLANES - 1:SUBLANES, :]
    pre = totals
    shift = 1
    while shift < groups:
        pre = pre + jnp.concatenate([jnp.zeros((shift, 1, lanes), F32), pre[:groups - shift]], axis=0)
        shift *= 2
    return (x3 + (pre - totals)).reshape(n, lanes)


def _hgrn_sample_step(q_ref, g_ref, k_ref, v_ref, s_in_ref, o_ref, s_out_ref):
    n = q_ref.shape[1]
    assert n <= SUBLANES

    def columns(x):
        return jnp.concatenate([x] * (SUBLANES // n), axis=0).T

    for h in range(N_HEADS):
        cs = slice(h * HEAD, (h + 1) * HEAD)
        qt = columns(q_ref[0, :, cs])
        ft = columns(jnp.exp(g_ref[0, :, cs]))
        kt = columns(k_ref[0, :, cs])
        for j in range(n):
            s_new = ft[:, j:j + 1] * s_in_ref[j, h] + kt[:, j:j + 1] * v_ref[0, j:j + 1, cs]
            s_out_ref[j, h] = s_new
            o_ref[0, j:j + 1, cs] = jnp.sum(qt[:, j:j + 1] * s_new, axis=0, keepdims=True)


def _hgrn_prompt_kernel(q_ref, b_ref, k_ref, v_ref, worst_ref, qs_ref, gs_ref, ks_ref, vs_ref, ss_in_ref,
                        o_ref, s_ref, os_ref, ss_out_ref, st_scr, oi_scr):
    i = pl.program_id(0)
    sample_step = functools.partial(_hgrn_sample_step, qs_ref, gs_ref, ks_ref, vs_ref, ss_in_ref,
                                    os_ref, ss_out_ref)

    @pl.when(i == 0)
    def _():
        st_scr[...] = jnp.zeros_like(st_scr)

    c = HGRN_CHUNK
    half = c // 2
    row = lax.broadcasted_iota(jnp.int32, (c, c), 0)
    col = lax.broadcasted_iota(jnp.int32, (c, c), 1)
    tri = col <= row
    nt = (((1,), (1,)), ((), ()))
    tn = (((0,), (0,)), ((), ()))
    chains = [(bi, ci, h) for bi in range(q_ref.shape[0]) for ci in range(HGRN_BLOCK // c)
              for h in range(N_HEADS)]

    def window(bi, ci, h):
        return bi, slice(ci * c, (ci + 1) * c), slice(h * HEAD, (h + 1) * HEAD)

    safe = jnp.max(worst_ref[...]) < HGRN_SAFE_EXPONENT

    def finish(chain, intra, q0, k2, v, b_last):
        bi, _, h = chain
        st = st_scr[bi, h]
        o = lax.dot_general(q0, st.astype(BF16), nt, preferred_element_type=F32) + intra
        o_ref[window(*chain)] = o.astype(o_ref.dtype)
        st_scr[bi, h] = st * jnp.exp(b_last) + lax.dot_general(v, k2, tn, preferred_element_type=F32)

    @pl.when(safe)
    def _():
        sample_step()
        for chain in chains:
            w = window(*chain)
            b = b_ref[w]
            b_mid = b[half - 1:half, :]
            b_last = b[c - 1:c, :]
            d = b - b_mid
            qm = q_ref[w] * jnp.exp(d)
            km = k_ref[w] * jnp.exp(-d)
            q0 = (qm * jnp.exp(b_mid)).astype(BF16)
            k2 = (km * jnp.exp(b_last - b_mid)).astype(BF16)
            v = v_ref[w].astype(BF16)
            a = lax.dot_general(qm.astype(BF16), km.astype(BF16), nt, preferred_element_type=F32)
            a = jnp.where(tri, a, 0.0).astype(BF16)
            finish(chain, jnp.dot(a, v, preferred_element_type=F32), q0, k2, v, b_last)

    @pl.when(jnp.logical_not(safe))
    def _():
        sample_step()
        srow = lax.broadcasted_iota(jnp.int32, (c, HEAD), 0)
        for chain in chains:
            bi, rs, cs = window(*chain)
            b = b_ref[bi, rs, cs]
            b_last = b[c - 1:c, :]
            kk = k_ref[bi, rs, cs]
            vv = v_ref[bi, rs, cs]

            def query_rows(tg, carry, bi=bi, rs=rs, cs=cs, b=b, kk=kk, vv=vv):
                base = pl.multiple_of(tg * SUBLANES, SUBLANES)
                at = pl.ds(rs.start + base, SUBLANES)
                b8 = b_ref[bi, at, cs]
                q8 = q_ref[bi, at, cs]
                out = []
                for j in range(SUBLANES):
                    decay = jnp.exp(jnp.where(srow <= base + j, b8[j:j + 1, :] - b, -jnp.inf))
                    a_col = jnp.sum(q8[j:j + 1, :] * kk * decay, axis=-1, keepdims=True)
                    out.append(jnp.sum(a_col * vv, axis=0, keepdims=True))
                oi_scr[pl.ds(base, SUBLANES), :] = jnp.concatenate(out, axis=0)
                return carry

            lax.fori_loop(0, c // SUBLANES, query_rows, 0)
            q0 = (q_ref[bi, rs, cs] * jnp.exp(b)).astype(BF16)
            k2 = (kk * jnp.exp(b_last - b)).astype(BF16)
            finish(chain, oi_scr[...], q0, k2, vv.astype(BF16), b_last)

    @pl.when(i == pl.num_programs(0) - 1)
    def _():
        for bi in range(q_ref.shape[0]):
            for h in range(N_HEADS):
                s_ref[bi, h] = st_scr[bi, h].T


def _hgrn(qr, cum_logf, kr, vr, worst, batch, seq, qr_s, logf_s, kr_s, vr_s, state_s):
    nb = seq // HGRN_BLOCK
    n_dec = qr_s.shape[0]
    per_step = n_dec // nb
    assert per_step * nb == n_dec
    spec = pl.BlockSpec((batch, HGRN_BLOCK, GROUP), lambda i: (0, i, 0))
    shape3 = (batch, seq, GROUP)
    rows_s = pl.BlockSpec((1, per_step, GROUP), lambda i: (i, 0, 0))
    state_spec = pl.BlockSpec((per_step, N_HEADS, HEAD, HEAD), lambda i: (i, 0, 0, 0))
    o_r, s_new, o_r_s, s_new_s = pl.pallas_call(
        _hgrn_prompt_kernel,
        grid=(nb,),
        in_specs=[spec] * 4 + [pl.BlockSpec((batch, 1, SUBLANES, HEAD), lambda i: (0, i, 0, 0))]
                 + [rows_s] * 4 + [state_spec],
        out_specs=[spec, pl.BlockSpec((batch, N_HEADS, HEAD, HEAD), lambda i: (0, 0, 0, 0)), rows_s, state_spec],
        out_shape=[jax.ShapeDtypeStruct(shape3, BF16),
                   jax.ShapeDtypeStruct((batch, N_HEADS, HEAD, HEAD), F32),
                   jax.ShapeDtypeStruct((nb, per_step, GROUP), F32),
                   jax.ShapeDtypeStruct(state_s.shape, F32)],
        scratch_shapes=[pltpu.VMEM((batch, N_HEADS, HEAD, HEAD), F32),
                        pltpu.VMEM((HGRN_CHUNK, HEAD), F32)],
        compiler_params=pltpu.CompilerParams(
            dimension_semantics=("arbitrary",), vmem_limit_bytes=VMEM_LIMIT),
        name="hgrn",
    )(*(a.reshape(shape3) for a in (qr, cum_logf, kr, vr)), worst.reshape(batch, nb, SUBLANES, HEAD),
      *(a.reshape(nb, per_step, GROUP) for a in (qr_s, logf_s, kr_s, vr_s)), state_s)
    return o_r.reshape(batch * seq, GROUP), s_new, o_r_s.reshape(n_dec, GROUP), s_new_s


def _lambda(lq1_ref, lk1_ref, lq2_ref, lk2_ref, lam_init):
    a = jnp.sum(lq1_ref[...] * lk1_ref[...], axis=-1, keepdims=True)
    b = jnp.sum(lq2_ref[...] * lk2_ref[...], axis=-1, keepdims=True)
    return jnp.exp(a) - jnp.exp(b) + lam_init


ATT_FULL, ATT_FULL_DIAG, ATT_DIAG = 0, 1, 2


def _attn_prompt_kernel(lam_init, qi_tab, kb_tab, kind_tab, first_tab,
                        q_ref, kt_ref, v_ref, lq1_ref, lk1_ref, lq2_ref, lk2_ref, sub_ref,
                        o_ref, m_scr, acc_scr):
    p = pl.program_id(2)
    kind = kind_tab[p]

    @pl.when(first_tab[p] == 1)
    def _():
        m_scr[...] = jnp.full_like(m_scr, -jnp.inf)
        acc_scr[...] = jnp.zeros_like(acc_scr)

    tq, tk = q_ref.shape[0], v_ref.shape[0]
    half_chunks = tq // HEAD

    def sweep(step_kind):
        masked = step_kind != ATT_FULL
        v_ext = jnp.concatenate([v_ref[...], jnp.ones((tk, HEAD), BF16)], axis=-1)
        lane = lax.broadcasted_iota(jnp.int32, (ATT_ROWS, HEAD), 1)
        row = lax.broadcasted_iota(jnp.int32, (ATT_ROWS, HEAD), 0)
        zero = jnp.zeros((ATT_ROWS, HEAD), BF16)
        group_chunks = ATT_ROWS // HEAD
        groups = range(tq // ATT_ROWS)
        for r in (reversed(groups) if masked else groups):
            rows = slice(r * ATT_ROWS, (r + 1) * ATT_ROWS)
            q = q_ref[rows, :]
            diag_chunks = (r + 1) * group_chunks
            n_chunks = {ATT_FULL: tk // HEAD, ATT_FULL_DIAG: half_chunks + diag_chunks,
                        ATT_DIAG: diag_chunks}[step_kind]
            kt = kt_ref[0, :, 0:n_chunks * HEAD]
            probs, alphas = [], []
            for mi, qm in enumerate((jnp.where(lane < A_DK, q, zero), jnp.where(lane >= A_DK, q, zero))):
                s = jnp.dot(qm, kt, preferred_element_type=F32)
                chunks = [s[:, c * HEAD:(c + 1) * HEAD] for c in range(n_chunks)]
                if masked:
                    for g in range(group_chunks):
                        c = n_chunks - group_chunks + g
                        chunks[c] = jnp.where(lane + g * HEAD <= row, chunks[c], -jnp.inf)
                mc = functools.reduce(jnp.maximum, chunks)
                m_prev = m_scr[mi, rows]
                m_new = jnp.maximum(m_prev, jnp.max(mc, axis=-1, keepdims=True))
                alphas.append(jnp.exp2(m_prev - m_new))
                probs.append(jnp.concatenate([jnp.exp2(sc - m_new).astype(BF16) for sc in chunks], axis=-1))
                m_scr[mi, rows] = m_new
            pv = jnp.dot(jnp.concatenate(probs, axis=0), v_ext[0:n_chunks * HEAD], preferred_element_type=F32)
            for mi, alpha in enumerate(alphas):
                acc_scr[mi, rows] = (jnp.concatenate([alpha, alpha], axis=-1) * acc_scr[mi, rows]
                                     + pv[mi * ATT_ROWS:(mi + 1) * ATT_ROWS])

    def finalize():
        lam = _lambda(lq1_ref, lk1_ref, lq2_ref, lk2_ref, lam_init)
        a0 = acc_scr[0]
        a1 = acc_scr[1]
        o = a0[:, :HEAD] / a0[:, HEAD:] - lam * (a1[:, :HEAD] / a1[:, HEAD:])
        o_ref[...] = (_rms(o, sub_ref[...]) * (1.0 - lam_init)).astype(o_ref.dtype)

    @pl.when(kind == ATT_FULL)
    def _():
        sweep(ATT_FULL)

    if tk > tq:
        @pl.when(kind == ATT_FULL_DIAG)
        def _():
            sweep(ATT_FULL_DIAG)
            finalize()

    @pl.when(kind == ATT_DIAG)
    def _():
        sweep(ATT_DIAG)
        finalize()


def _attn_prompt(aq_bf, akt_bf, av_bf, lq1, lk1, lq2, lk2, subln, lam_init, batch, seq):
    t = ATT_BLOCK
    ratio = ATT_KEY_BLOCK // t
    nq = seq // t
    assert ATT_ROWS % HEAD == 0 and ratio in (1, 2) and nq % ratio == 0
    steps = []
    for i in range(nq):
        for j in range(i // ratio):
            steps.append((i, j, ATT_FULL, int(j == 0)))
        steps.append((i, i // ratio, ATT_FULL_DIAG if i % ratio else ATT_DIAG, int(i < ratio)))
    qi_tab, kb_tab, kind_tab, first_tab = (jnp.asarray([s[c] for s in steps], jnp.int32) for c in range(4))
    q_spec = pl.BlockSpec((t, HEAD), lambda b, h, p, qi, kb, kind, first: (b * nq + qi[p], h))
    v_spec = pl.BlockSpec((ratio * t, HEAD),
                          lambda b, h, p, qi, kb, kind, first: (b * (nq // ratio) + kb[p], h))
    kt_spec = pl.BlockSpec((1, HEAD, ratio * t), lambda b, h, p, qi, kb, kind, first: (b, h, kb[p]))
    lam_spec = pl.BlockSpec((1, A_DK), lambda *_: (0, 0))
    sub_spec = pl.BlockSpec((1, HEAD), lambda *_: (0, 0))
    return pl.pallas_call(
        functools.partial(_attn_prompt_kernel, lam_init),
        grid_spec=pltpu.PrefetchScalarGridSpec(
            num_scalar_prefetch=4,
            grid=(batch, N_HEADS, len(steps)),
            in_specs=[q_spec, kt_spec, v_spec, lam_spec, lam_spec, lam_spec, lam_spec, sub_spec],
            out_specs=q_spec,
            scratch_shapes=[pltpu.VMEM((2, t, HEAD), F32), pltpu.VMEM((2, t, 2 * HEAD), F32)]),
        out_shape=jax.ShapeDtypeStruct((batch * seq, GROUP), BF16),
        compiler_params=pltpu.CompilerParams(
            dimension_semantics=("parallel", "parallel", "arbitrary"), vmem_limit_bytes=VMEM_LIMIT),
        name="attn_prompt",
    )(qi_tab, kb_tab, kind_tab, first_tab, aq_bf, akt_bf, av_bf, lq1, lk1, lq2, lk2, subln)


def _page_copies(pt_ref, item, n_pages, ck_hbm, cv_hbm, kbuf, vbuf, sem, slot):
    page_rows = ck_hbm.shape[1]
    out = []
    for j in range(n_pages):
        page = pt_ref[item * n_pages + j]
        out.append(pltpu.make_async_copy(ck_hbm.at[page], kbuf.at[slot, j], sem.at[slot, 0]))
        out.append(pltpu.make_async_copy(cv_hbm.at[page], vbuf.at[slot, pl.ds(j * page_rows, page_rows)],
                                         sem.at[slot, 1]))
    return out


def _decode_item(lam, lam_init, q, kn, vn, sub4, kpages, vrows):
    n_pages = kpages.shape[0]
    past = n_pages * kpages.shape[2]
    rows = 2 * N_HEADS
    ri = lax.broadcasted_iota(jnp.int32, (rows, GROUP), 0)
    li = lax.broadcasted_iota(jnp.int32, (rows, GROUP), 1)
    mine = (li // A_DK) == (ri % N_HEADS) * 2 + ri // N_HEADS
    qbd = jnp.where(mine, jnp.broadcast_to(q.astype(F32), (rows, GROUP)), 0.0)
    qbd_bf = qbd.astype(BF16)
    s = jnp.concatenate(
        [jnp.dot(qbd_bf, kpages[j].astype(BF16), preferred_element_type=F32) for j in range(n_pages)],
        axis=-1)
    s_new = jnp.sum(qbd * kn.astype(F32), axis=-1, keepdims=True)
    m = jnp.maximum(jnp.max(s, axis=-1, keepdims=True), s_new)
    p = jnp.exp2(s - m)
    p_new = jnp.exp2(s_new - m)
    inv_l = 1.0 / (jnp.sum(p, axis=-1, keepdims=True) + p_new)
    pn = p * inv_l
    pn_new = p_new * inv_l
    attn = pn - lam * pltpu.roll(pn, N_HEADS, 0)
    attn_new = pn_new - lam * pltpu.roll(pn_new, N_HEADS, 0)
    attn_bf = attn.astype(BF16)
    o = jnp.concatenate(
        [jnp.dot(attn_bf, vrows[pl.ds(h, past, stride=N_HEADS), :].astype(BF16), preferred_element_type=F32)
         for h in range(N_HEADS)], axis=-1) + attn_new * vn
    keep = jnp.logical_and(ri < N_HEADS, li // HEAD == ri)
    o = jnp.where(keep, o, 0.0)
    ms = jnp.sum(o * o, axis=-1, keepdims=True) * (1.0 / HEAD)
    o = o * lax.rsqrt(ms + EPS) * sub4 * (1.0 - lam_init)
    return jnp.sum(o, axis=0, keepdims=True)


def _tail_kernel(decode, *refs):
    if decode is None:
        (x_ref, or_ref, gate_ref, oa_ref, gn_ref, wo_ref, nf_ref, wg_ref, wu_ref, wd_ref, nfin_ref,
         y_ref, act_scr) = refs
    else:
        (pt_ref, x_ref, or_ref, gate_ref, oa_ref, gn_ref, wo_ref, nf_ref, wg_ref, wu_ref, wd_ref, nfin_ref,
         q_ref, kn_ref, vn_ref, lq1_ref, lk1_ref, lq2_ref, lk2_ref, sub4_ref, ck_hbm, cv_hbm,
         y_ref, os_ref, act_scr, kbuf, vbuf, sem) = refs
        lam_init, n_pages, items = decode
        step = pl.program_id(0)
        copies = functools.partial(_page_copies, pt_ref, n_pages=n_pages, ck_hbm=ck_hbm, cv_hbm=cv_hbm,
                                   kbuf=kbuf, vbuf=vbuf, sem=sem)

        @pl.when(step == 0)
        def _():
            for cp in copies(item=0, slot=0):
                cp.start()

    def decode_item(jj):
        if decode is None:
            return
        item = step * items + jj
        slot = jj % 2
        if jj + 1 < items:
            for cp in copies(item=item + 1, slot=1 - slot):
                cp.start()
        else:
            @pl.when(step + 1 < pl.num_programs(0))
            def _():
                for cp in copies(item=item + 1, slot=1 - slot):
                    cp.start()
        for cp in copies(item=item, slot=slot):
            cp.wait()
        lam = _lambda(lq1_ref, lk1_ref, lq2_ref, lk2_ref, lam_init)
        os_ref[jj] = _decode_item(lam, lam_init, q_ref[jj], kn_ref[jj], vn_ref[jj], sub4_ref[...],
                                  kbuf.at[slot], vbuf.at[slot])

    tm = x_ref.shape[0]
    d_ff = wg_ref.shape[1]
    sub = min(tm, TAIL_ROWS)
    groups = tm // sub
    per_group = 0 if decode is None else items // (2 * groups)
    for g in range(groups):
        rows = slice(g * sub, (g + 1) * sub)
        for jj in range(per_group):
            decode_item(2 * g * per_group + jj)
        parts = []
        for h in range(N_HEADS):
            cs = slice(h * HEAD, (h + 1) * HEAD)
            parts.append((_rms(or_ref[rows, cs].astype(F32), gn_ref[...]) * gate_ref[rows, cs]).astype(BF16))
        parts.append(oa_ref[rows, :].astype(BF16))
        mix = jnp.concatenate(parts, axis=-1)
        x1 = x_ref[rows, :] + jnp.dot(mix, wo_ref[...], preferred_element_type=F32)
        h2 = _rms(x1, nf_ref[...]).astype(BF16)
        for jj in range(per_group):
            decode_item((2 * g + 1) * per_group + jj)
        for j in range(d_ff // FF_CHUNK):
            fs = slice(j * FF_CHUNK, (j + 1) * FF_CHUNK)
            gt = jnp.dot(h2, wg_ref[:, fs], preferred_element_type=F32)
            up = jnp.dot(h2, wu_ref[:, fs], preferred_element_type=F32)
            act_scr[rows, fs] = (gt * _sigmoid(gt) * up).astype(BF16)
        y = x1 + jnp.dot(act_scr[rows, :], wd_ref[...], preferred_element_type=F32)
        y_ref[rows, :] = _rms(y, nfin_ref[...])


def _tail(x, o_r, gate, o_a, r_gnorm, wo_bf, norm_ffn, wg_bf, wu_bf, wd_bf, norm_final, tm, decode=None):
    rows, d = x.shape
    d_ff = wg_bf.shape[1]
    steps = rows // tm

    def resident(shape):
        return pl.BlockSpec(shape, lambda i, *_: (0, 0), pipeline_mode=pl.Buffered(1))

    def const(shape):
        return pl.BlockSpec(shape, lambda i, *_: (0,) * len(shape))

    def row_block(width):
        return pl.BlockSpec((tm, width), lambda i, *_: (i, 0))

    in_specs = [row_block(d), row_block(GROUP), row_block(GROUP), row_block(GROUP), const((1, HEAD)),
                resident((d, d)), const((1, d)), resident((d, d_ff)), resident((d, d_ff)), resident((d_ff, d)),
                const((1, d))]
    operands = [x, o_r, gate, o_a, r_gnorm, wo_bf, norm_ffn, wg_bf, wu_bf, wd_bf, norm_final]
    out_specs = [row_block(d)]
    out_shape = [jax.ShapeDtypeStruct((rows, d), F32)]
    scratch = [pltpu.VMEM((tm, d_ff), BF16)]
    config = None
    prefetch = []
    if decode is not None:
        lam_init, page_table, aq_bf, ak_bf, av, lams, subln4, cache_k, cache_v = decode
        n, n_pages = page_table.shape
        n_phys, page_size = cache_k.shape[0], cache_k.shape[1]
        items = n // steps
        assert items * steps == n and items % (2 * (tm // min(tm, TAIL_ROWS))) == 0
        ck = jnp.transpose(cache_k, (0, 2, 3, 4, 1)).reshape(n_phys, GROUP, page_size)
        cv = cache_v.reshape(n_phys, page_size * N_HEADS, HEAD)
        item_block = pl.BlockSpec((items, 1, GROUP), lambda i, *_: (i, 0, 0))
        in_specs += [item_block, item_block, item_block] + [const((1, A_DK))] * 4 + [
            const((1, GROUP)), pl.BlockSpec(memory_space=pl.ANY), pl.BlockSpec(memory_space=pl.ANY)]
        operands += [a.reshape(n, 1, GROUP) for a in (aq_bf, ak_bf, av)] + list(lams) + [subln4, ck, cv]
        out_specs.append(item_block)
        out_shape.append(jax.ShapeDtypeStruct((n, 1, GROUP), F32))
        scratch += [pltpu.VMEM((2, n_pages, GROUP, page_size), F32),
                    pltpu.VMEM((2, n_pages * page_size * N_HEADS, HEAD), F32),
                    pltpu.SemaphoreType.DMA((2, 2))]
        config = (lam_init, n_pages, items)
        prefetch = [page_table.reshape(-1)]
    out = pl.pallas_call(
        functools.partial(_tail_kernel, config),
        grid_spec=pltpu.PrefetchScalarGridSpec(
            num_scalar_prefetch=len(prefetch), grid=(steps,),
            in_specs=in_specs, out_specs=out_specs, scratch_shapes=scratch),
        out_shape=out_shape,
        compiler_params=pltpu.CompilerParams(
            dimension_semantics=("arbitrary",), vmem_limit_bytes=TAIL_VMEM_LIMIT),
        name="tail",
    )(*prefetch, *operands)
    if decode is None:
        return out[0]
    return out[0], out[1].reshape(n, GROUP)


def kernel(x_prompt, x_sample, cache_k, cache_v, state_hgrn, page_table, w_in, w_out, lb_param,
           r_gnorm, lam_q1, lam_k1, lam_q2, lam_k2, a_subln, norm_mix, norm_ffn, w_gate, w_up,
           w_down, norm_final):
    batch, seq, d = x_prompt.shape
    n_dec = x_sample.shape[0]
    depth = w_in.shape[0]
    assert depth == 1 and x_sample.shape[1] == 1
    hp = x_prompt.reshape(batch * seq, d)
    hs = x_sample.reshape(n_dec, d)
    lb_param = lb_param.astype(F32)
    nfin = norm_final.reshape(1, d)
    kp, vp, sp, kss, vss, sss = [], [], [], [], [], []
    for l in range(depth):
        lam_init = 0.8 - 0.6 * math.exp(-0.3 * l)
        w_in_bf = w_in[l].astype(BF16)
        wo_bf = w_out[l].astype(BF16)
        wg_bf = w_gate[l].astype(BF16)
        wu_bf = w_up[l].astype(BF16)
        wd_bf = w_down[l].astype(BF16)
        nmix = norm_mix[l].reshape(1, d)
        nffn = norm_ffn[l].reshape(1, d)
        gn = r_gnorm[l].reshape(1, HEAD)
        sub = a_subln[l].reshape(1, HEAD)
        lams = [a[l].reshape(1, A_DK).astype(F32) for a in (lam_q1, lam_k1, lam_q2, lam_k2)]

        qr_s, logf_s, kr_s, vr_s, gate_s, aq_s, akt_s, av_s, ak_s = _inproj(
            hs, nmix, w_in_bf, lb_param, l, n_dec, n_dec, prompt=False)

        assert IN_ROWS == HGRN_BLOCK
        qr, cum_logf, kr, vr, gate, aq_bf, akt, av_heads, akt_bf, av_bf, worst = _inproj(
            hp, nmix, w_in_bf, lb_param, l, ROW_BLOCK, seq, prompt=True)
        o_r, s_new, o_r_s, s_new_s = _hgrn(qr, cum_logf, kr, vr, worst, batch, seq,
                                           qr_s, logf_s, kr_s, vr_s, state_hgrn[l].astype(F32))
        o_a = _attn_prompt(aq_bf, akt_bf, av_bf, *lams, sub, lam_init, batch, seq)
        paged = (lam_init, page_table, aq_s, ak_s, av_s, lams, jnp.tile(sub, (1, N_HEADS)),
                 cache_k[l], cache_v[l])
        hp, o_a_s = _tail(hp, o_r, gate, o_a, gn, wo_bf, nffn, wg_bf, wu_bf, wd_bf, nfin, TAIL_BLOCK, paged)
        kp.append(jnp.transpose(akt.reshape(batch, N_HEADS, 2, A_DK, seq), (0, 4, 1, 2, 3)))
        vp.append(av_heads.reshape(batch, seq, N_HEADS, HEAD))
        sp.append(s_new)

        hs = _tail(hs, o_r_s, gate_s, o_a_s, gn, wo_bf, nffn, wg_bf, wu_bf, wd_bf, nfin, n_dec)
        kss.append(jnp.transpose(akt_s.reshape(N_HEADS, 2, A_DK, n_dec), (3, 0, 1, 2))[:, None])
        vss.append(av_s.reshape(n_dec, 1, N_HEADS, HEAD))
        sss.append(s_new_s)

    y_prompt = hp.reshape(batch, seq, d)
    y_sample = hs.reshape(n_dec, 1, d)
    return (y_prompt, y_sample, jnp.stack(kp), jnp.stack(vp), jnp.stack(sp),
            jnp.stack(kss), jnp.stack(vss), jnp.stack(sss))
```

```python
import functools
import math

import jax
import jax.numpy as jnp
from jax import lax
from jax.experimental import pallas as pl
from jax.experimental.pallas import tpu as pltpu

F32 = jnp.float32
BF16 = jnp.bfloat16

EPS = 1e-6
LOG2E = math.log2(math.e)
HEAD = 128
N_HEADS = 4
A_DK = 64
GROUP = N_HEADS * HEAD
SUBLANES = 8
HGRN_CHUNK = 128
HGRN_BLOCK = 256
HGRN_SAFE_EXPONENT = 60.0
ATT_BLOCK = 2048
ATT_KEY_BLOCK = 2048
ATT_ROWS = 128
ROW_BLOCK = 512
IN_ROWS = 256
TAIL_BLOCK = 256
TAIL_ROWS = 256
TAIL_VMEM_LIMIT = 56 * 1024 * 1024
FF_CHUNK = 256
VMEM_LIMIT = 48 * 1024 * 1024


def _sigmoid(x):
    return 1.0 / (1.0 + jnp.exp(-x))


def _rms(x, w):
    return x * lax.rsqrt(jnp.mean(x * x, axis=-1, keepdims=True) + EPS) * w


def _const_spec(shape):
    return pl.BlockSpec(shape, lambda *_: (0,) * len(shape))


def _inproj_kernel(layer, prompt, x_ref, nw_ref, w_ref, lb_ref,
                   qr_ref, logf_ref, kr_ref, vr_ref, gate_ref, aq_ref, akt_ref, *kv_refs):
    lbp = lb_ref[...]
    e = jnp.exp(lbp - jnp.max(lbp, axis=0, keepdims=True))
    lb = jnp.sum(e[:layer + 1], axis=0, keepdims=True) / jnp.sum(e, axis=0, keepdims=True)

    tm = x_ref.shape[0]
    sub = min(tm, IN_ROWS)
    for g in range(tm // sub):
        rows = slice(g * sub, (g + 1) * sub)
        h = _rms(x_ref[rows, :], nw_ref[...]).astype(BF16)

        def piece(j, h=h):
            return jnp.dot(h, w_ref[:, j * GROUP:(j + 1) * GROUP], preferred_element_type=F32)

        rq = piece(0)
        qr_ref[rows, :] = (rq * _sigmoid(rq)).astype(qr_ref.dtype)
        f = lb + (1.0 - lb) * _sigmoid(piece(1))
        logf = jnp.log(f)
        if prompt:
            c, half = HGRN_CHUNK, HGRN_CHUNK // 2
            worst = jnp.zeros((1, GROUP), F32)
            for ci in range(sub // c):
                b = _cumsum_rows(logf[ci * c:(ci + 1) * c])
                logf_ref[g * sub + ci * c:g * sub + (ci + 1) * c, :] = b
                b_mid = b[half - 1:half, :]
                worst = jnp.maximum(worst, jnp.maximum(-b_mid, b_mid - b[c - 1:c, :]))
            worst = functools.reduce(jnp.maximum,
                                     [worst[:, hd * HEAD:(hd + 1) * HEAD] for hd in range(N_HEADS)])
        else:
            logf_ref[rows, :] = logf
        kr_ref[rows, :] = (1.0 - f).astype(kr_ref.dtype)
        vr_ref[rows, :] = piece(2).astype(vr_ref.dtype)
        rg = piece(3)
        gate_ref[rows, :] = (rg * _sigmoid(rg)).astype(gate_ref.dtype)
        aq_ref[rows, :] = (piece(4) * (A_DK ** -0.5 * LOG2E)).astype(BF16)
        ak = piece(5)
        akt = ak.T
        akt_ref[0, :, rows] = akt
        av = piece(6)
        if prompt:
            avh_ref, aktb_ref, avb_ref, worst_ref = kv_refs
            worst_ref[g] = jnp.broadcast_to(worst, (SUBLANES, HEAD))
            for hd in range(N_HEADS):
                avh_ref[pl.ds(g * sub * N_HEADS + hd, sub, stride=N_HEADS), :] = av[:, hd * HEAD:(hd + 1) * HEAD]
            aktb_ref[0, :, rows] = akt.astype(BF16)
            avb_ref[rows, :] = av.astype(BF16)
        else:
            av_ref, akb_ref = kv_refs
            av_ref[rows, :] = av
            akb_ref[rows, :] = ak.astype(BF16)


def _inproj(x, norm_w, w_bf, lb_param, layer, tm, seq, prompt):
    rows, d = x.shape
    d_in = w_bf.shape[1]
    per_seq = seq // tm
    f32_out = jax.ShapeDtypeStruct((rows, GROUP), F32)
    bf_out = jax.ShapeDtypeStruct((rows, GROUP), BF16)
    row_spec = pl.BlockSpec((tm, GROUP), lambda i: (i, 0))
    kt_spec = pl.BlockSpec((1, GROUP, tm), lambda i: (i // per_seq, 0, i % per_seq))
    kt_shape = (rows // seq, GROUP, seq)
    if prompt:
        narrow = bf_out
        sub = min(tm, IN_ROWS)
        assert sub % HGRN_CHUNK == 0 and tm % sub == 0
        kv_specs = [pl.BlockSpec((tm * N_HEADS, HEAD), lambda i: (i, 0)), kt_spec, row_spec,
                    pl.BlockSpec((tm // sub, SUBLANES, HEAD), lambda i: (i, 0, 0))]
        kv_shapes = [jax.ShapeDtypeStruct((rows * N_HEADS, HEAD), F32),
                     jax.ShapeDtypeStruct(kt_shape, BF16), bf_out,
                     jax.ShapeDtypeStruct((rows // sub, SUBLANES, HEAD), F32)]
    else:
        narrow = f32_out
        kv_specs = [row_spec, row_spec]
        kv_shapes = [f32_out, bf_out]
    return pl.pallas_call(
        functools.partial(_inproj_kernel, layer, prompt),
        grid=(rows // tm,),
        in_specs=[pl.BlockSpec((tm, d), lambda i: (i, 0)),
                  _const_spec((1, d)),
                  pl.BlockSpec((d, d_in), lambda i: (0, 0), pipeline_mode=pl.Buffered(1)),
                  _const_spec(lb_param.shape)],
        out_specs=[row_spec] * 6 + [kt_spec] + kv_specs,
        out_shape=[narrow, f32_out, narrow, narrow, narrow, bf_out,
                   jax.ShapeDtypeStruct(kt_shape, F32)] + kv_shapes,
        compiler_params=pltpu.CompilerParams(
            dimension_semantics=("parallel",), vmem_limit_bytes=VMEM_LIMIT),
        name="inproj",
    )(x, norm_w, w_bf, lb_param)


def _cumsum_rows(x):
    n, lanes = x.shape
    groups = n // SUBLANES
    x3 = x.reshape(groups, SUBLANES, lanes)
    sub = lax.broadcasted_iota(jnp.int32, x3.shape, 1)
    shift = 1
    while shift < SUBLANES:
        x3 = x3 + jnp.where(sub >= shift, pltpu.roll(x3, shift, 1), 0.0)
        shift *= 2
    totals = x3[:, SUBLANES - 1:SUBLANES, :]
    pre = totals
    shift = 1
    while shift < groups:
        pre = pre + jnp.concatenate([jnp.zeros((shift, 1, lanes), F32), pre[:groups - shift]], axis=0)
        shift *= 2
    return (x3 + (pre - totals)).reshape(n, lanes)


def _hgrn_sample_step(q_ref, g_ref, k_ref, v_ref, s_in_ref, o_ref, s_out_ref):
    n = q_ref.shape[1]
    assert n <= SUBLANES

    def columns(x):
        return jnp.concatenate([x] * (SUBLANES // n), axis=0).T

    for h in range(N_HEADS):
        cs = slice(h * HEAD, (h + 1) * HEAD)
        qt = columns(q_ref[0, :, cs])
        ft = columns(jnp.exp(g_ref[0, :, cs]))
        kt = columns(k_ref[0, :, cs])
        for j in range(n):
            s_new = ft[:, j:j + 1] * s_in_ref[j, h] + kt[:, j:j + 1] * v_ref[0, j:j + 1, cs]
            s_out_ref[j, h] = s_new
            o_ref[0, j:j + 1, cs] = jnp.sum(qt[:, j:j + 1] * s_new, axis=0, keepdims=True)


def _hgrn_prompt_kernel(q_ref, b_ref, k_ref, v_ref, worst_ref, qs_ref, gs_ref, ks_ref, vs_ref, ss_in_ref,
                        o_ref, s_ref, os_ref, ss_out_ref, st_scr, oi_scr):
    i = pl.program_id(0)
    sample_step = functools.partial(_hgrn_sample_step, qs_ref, gs_ref, ks_ref, vs_ref, ss_in_ref,
                                    os_ref, ss_out_ref)

    @pl.when(i == 0)
    def _():
        st_scr[...] = jnp.zeros_like(st_scr)

    c = HGRN_CHUNK
    half = c // 2
    row = lax.broadcasted_iota(jnp.int32, (c, c), 0)
    col = lax.broadcasted_iota(jnp.int32, (c, c), 1)
    tri = col <= row
    nt = (((1,), (1,)), ((), ()))
    tn = (((0,), (0,)), ((), ()))
    chains = [(bi, ci, h) for bi in range(q_ref.shape[0]) for ci in range(HGRN_BLOCK // c)
              for h in range(N_HEADS)]

    def window(bi, ci, h):
        return bi, slice(ci * c, (ci + 1) * c), slice(h * HEAD, (h + 1) * HEAD)

    safe = jnp.max(worst_ref[...]) < HGRN_SAFE_EXPONENT

    def finish(chain, intra, q0, k2, v, b_last):
        bi, _, h = chain
        st = st_scr[bi, h]
        o = lax.dot_general(q0, st.astype(BF16), nt, preferred_element_type=F32) + intra
        o_ref[window(*chain)] = o.astype(o_ref.dtype)
        st_scr[bi, h] = st * jnp.exp(b_last) + lax.dot_general(v, k2, tn, preferred_element_type=F32)

    @pl.when(safe)
    def _():
        sample_step()
        for chain in chains:
            w = window(*chain)
            b = b_ref[w]
            b_mid = b[half - 1:half, :]
            b_last = b[c - 1:c, :]
            d = b - b_mid
            qm = q_ref[w] * jnp.exp(d)
            km = k_ref[w] * jnp.exp(-d)
            q0 = (qm * jnp.exp(b_mid)).astype(BF16)
            k2 = (km * jnp.exp(b_last - b_mid)).astype(BF16)
            v = v_ref[w].astype(BF16)
            a = lax.dot_general(qm.astype(BF16), km.astype(BF16), nt, preferred_element_type=F32)
            a = jnp.where(tri, a, 0.0).astype(BF16)
            finish(chain, jnp.dot(a, v, preferred_element_type=F32), q0, k2, v, b_last)

    @pl.when(jnp.logical_not(safe))
    def _():
        sample_step()
        srow = lax.broadcasted_iota(jnp.int32, (c, HEAD), 0)
        for chain in chains:
            bi, rs, cs = window(*chain)
            b = b_ref[bi, rs, cs]
            b_last = b[c - 1:c, :]
            kk = k_ref[bi, rs, cs]
            vv = v_ref[bi, rs, cs]

            def query_rows(tg, carry, bi=bi, rs=rs, cs=cs, b=b, kk=kk, vv=vv):
                base = pl.multiple_of(tg * SUBLANES, SUBLANES)
                at = pl.ds(rs.start + base, SUBLANES)
                b8 = b_ref[bi, at, cs]
                q8 = q_ref[bi, at, cs]
                out = []
                for j in range(SUBLANES):
                    decay = jnp.exp(jnp.where(srow <= base + j, b8[j:j + 1, :] - b, -jnp.inf))
                    a_col = jnp.sum(q8[j:j + 1, :] * kk * decay, axis=-1, keepdims=True)
                    out.append(jnp.sum(a_col * vv, axis=0, keepdims=True))
                oi_scr[pl.ds(base, SUBLANES), :] = jnp.concatenate(out, axis=0)
                return carry

            lax.fori_loop(0, c // SUBLANES, query_rows, 0)
            q0 = (q_ref[bi, rs, cs] * jnp.exp(b)).astype(BF16)
            k2 = (kk * jnp.exp(b_last - b)).astype(BF16)
            finish(chain, oi_scr[...], q0, k2, vv.astype(BF16), b_last)

    @pl.when(i == pl.num_programs(0) - 1)
    def _():
        for bi in range(q_ref.shape[0]):
            for h in range(N_HEADS):
                s_ref[bi, h] = st_scr[bi, h].T


def _hgrn(qr, cum_logf, kr, vr, worst, batch, seq, qr_s, logf_s, kr_s, vr_s, state_s):
    nb = seq // HGRN_BLOCK
    n_dec = qr_s.shape[0]
    per_step = n_dec // nb
    assert per_step * nb == n_dec
    spec = pl.BlockSpec((batch, HGRN_BLOCK, GROUP), lambda i: (0, i, 0))
    shape3 = (batch, seq, GROUP)
    rows_s = pl.BlockSpec((1, per_step, GROUP), lambda i: (i, 0, 0))
    state_spec = pl.BlockSpec((per_step, N_HEADS, HEAD, HEAD), lambda i: (i, 0, 0, 0))
    o_r, s_new, o_r_s, s_new_s = pl.pallas_call(
        _hgrn_prompt_kernel,
        grid=(nb,),
        in_specs=[spec] * 4 + [pl.BlockSpec((batch, 1, SUBLANES, HEAD), lambda i: (0, i, 0, 0))]
                 + [rows_s] * 4 + [state_spec],
        out_specs=[spec, pl.BlockSpec((batch, N_HEADS, HEAD, HEAD), lambda i: (0, 0, 0, 0)), rows_s, state_spec],
        out_shape=[jax.ShapeDtypeStruct(shape3, BF16),
                   jax.ShapeDtypeStruct((batch, N_HEADS, HEAD, HEAD), F32),
                   jax.ShapeDtypeStruct((nb, per_step, GROUP), F32),
                   jax.ShapeDtypeStruct(state_s.shape, F32)],
        scratch_shapes=[pltpu.VMEM((batch, N_HEADS, HEAD, HEAD), F32),
                        pltpu.VMEM((HGRN_CHUNK, HEAD), F32)],
        compiler_params=pltpu.CompilerParams(
            dimension_semantics=("arbitrary",), vmem_limit_bytes=VMEM_LIMIT),
        name="hgrn",
    )(*(a.reshape(shape3) for a in (qr, cum_logf, kr, vr)), worst.reshape(batch, nb, SUBLANES, HEAD),
      *(a.reshape(nb, per_step, GROUP) for a in (qr_s, logf_s, kr_s, vr_s)), state_s)
    return o_r.reshape(batch * seq, GROUP), s_new, o_r_s.reshape(n_dec, GROUP), s_new_s


def _lambda(lq1_ref, lk1_ref, lq2_ref, lk2_ref, lam_init):
    a = jnp.sum(lq1_ref[...] * lk1_ref[...], axis=-1, keepdims=True)
    b = jnp.sum(lq2_ref[...] * lk2_ref[...], axis=-1, keepdims=True)
    return jnp.exp(a) - jnp.exp(b) + lam_init


ATT_FULL, ATT_FULL_DIAG, ATT_DIAG = 0, 1, 2


def _attn_prompt_kernel(lam_init, qi_tab, kb_tab, kind_tab, first_tab,
                        q_ref, kt_ref, v_ref, lq1_ref, lk1_ref, lq2_ref, lk2_ref, sub_ref,
                        o_ref, m_scr, acc_scr):
    p = pl.program_id(2)
    kind = kind_tab[p]

    @pl.when(first_tab[p] == 1)
    def _():
        m_scr[...] = jnp.full_like(m_scr, -jnp.inf)
        acc_scr[...] = jnp.zeros_like(acc_scr)

    tq, tk = q_ref.shape[0], v_ref.shape[0]
    half_chunks = tq // HEAD

    def sweep(step_kind):
        masked = step_kind != ATT_FULL
        v_ext = jnp.concatenate([v_ref[...], jnp.ones((tk, HEAD), BF16)], axis=-1)
        lane = lax.broadcasted_iota(jnp.int32, (ATT_ROWS, HEAD), 1)
        row = lax.broadcasted_iota(jnp.int32, (ATT_ROWS, HEAD), 0)
        zero = jnp.zeros((ATT_ROWS, HEAD), BF16)
        group_chunks = ATT_ROWS // HEAD
        groups = range(tq // ATT_ROWS)
        for r in (reversed(groups) if masked else groups):
            rows = slice(r * ATT_ROWS, (r + 1) * ATT_ROWS)
            q = q_ref[rows, :]
            diag_chunks = (r + 1) * group_chunks
            n_chunks = {ATT_FULL: tk // HEAD, ATT_FULL_DIAG: half_chunks + diag_chunks,
                        ATT_DIAG: diag_chunks}[step_kind]
            kt = kt_ref[0, :, 0:n_chunks * HEAD]
            probs, alphas = [], []
            for mi, qm in enumerate((jnp.where(lane < A_DK, q, zero), jnp.where(lane >= A_DK, q, zero))):
                s = jnp.dot(qm, kt, preferred_element_type=F32)
                chunks = [s[:, c * HEAD:(c + 1) * HEAD] for c in range(n_chunks)]
                if masked:
                    for g in range(group_chunks):
                        c = n_chunks - group_chunks + g
                        chunks[c] = jnp.where(lane + g * HEAD <= row, chunks[c], -jnp.inf)
                mc = functools.reduce(jnp.maximum, chunks)
                m_prev = m_scr[mi, rows]
                m_new = jnp.maximum(m_prev, jnp.max(mc, axis=-1, keepdims=True))
                alphas.append(jnp.exp2(m_prev - m_new))
                probs.append(jnp.concatenate([jnp.exp2(sc - m_new).astype(BF16) for sc in chunks], axis=-1))
                m_scr[mi, rows] = m_new
            pv = jnp.dot(jnp.concatenate(probs, axis=0), v_ext[0:n_chunks * HEAD], preferred_element_type=F32)
            for mi, alpha in enumerate(alphas):
                acc_scr[mi, rows] = (jnp.concatenate([alpha, alpha], axis=-1) * acc_scr[mi, rows]
                                     + pv[mi * ATT_ROWS:(mi + 1) * ATT_ROWS])

    def finalize():
        lam = _lambda(lq1_ref, lk1_ref, lq2_ref, lk2_ref, lam_init)
        a0 = acc_scr[0]
        a1 = acc_scr[1]
        o = a0[:, :HEAD] / a0[:, HEAD:] - lam * (a1[:, :HEAD] / a1[:, HEAD:])
        o_ref[...] = (_rms(o, sub_ref[...]) * (1.0 - lam_init)).astype(o_ref.dtype)

    @pl.when(kind == ATT_FULL)
    def _():
        sweep(ATT_FULL)

    if tk > tq:
        @pl.when(kind == ATT_FULL_DIAG)
        def _():
            sweep(ATT_FULL_DIAG)
            finalize()

    @pl.when(kind == ATT_DIAG)
    def _():
        sweep(ATT_DIAG)
        finalize()


def _attn_prompt(aq_bf, akt_bf, av_bf, lq1, lk1, lq2, lk2, subln, lam_init, batch, seq):
    t = ATT_BLOCK
    ratio = ATT_KEY_BLOCK // t
    nq = seq // t
    assert ATT_ROWS % HEAD == 0 and ratio in (1, 2) and nq % ratio == 0
    steps = []
    for i in range(nq):
        for j in range(i // ratio):
            steps.append((i, j, ATT_FULL, int(j == 0)))
        steps.append((i, i // ratio, ATT_FULL_DIAG if i % ratio else ATT_DIAG, int(i < ratio)))
    qi_tab, kb_tab, kind_tab, first_tab = (jnp.asarray([s[c] for s in steps], jnp.int32) for c in range(4))
    q_spec = pl.BlockSpec((t, HEAD), lambda b, h, p, qi, kb, kind, first: (b * nq + qi[p], h))
    v_spec = pl.BlockSpec((ratio * t, HEAD),
                          lambda b, h, p, qi, kb, kind, first: (b * (nq // ratio) + kb[p], h))
    kt_spec = pl.BlockSpec((1, HEAD, ratio * t), lambda b, h, p, qi, kb, kind, first: (b, h, kb[p]))
    lam_spec = pl.BlockSpec((1, A_DK), lambda *_: (0, 0))
    sub_spec = pl.BlockSpec((1, HEAD), lambda *_: (0, 0))
    return pl.pallas_call(
        functools.partial(_attn_prompt_kernel, lam_init),
        grid_spec=pltpu.PrefetchScalarGridSpec(
            num_scalar_prefetch=4,
            grid=(batch, N_HEADS, len(steps)),
            in_specs=[q_spec, kt_spec, v_spec, lam_spec, lam_spec, lam_spec, lam_spec, sub_spec],
            out_specs=q_spec,
            scratch_shapes=[pltpu.VMEM((2, t, HEAD), F32), pltpu.VMEM((2, t, 2 * HEAD), F32)]),
        out_shape=jax.ShapeDtypeStruct((batch * seq, GROUP), BF16),
        compiler_params=pltpu.CompilerParams(
            dimension_semantics=("parallel", "parallel", "arbitrary"), vmem_limit_bytes=VMEM_LIMIT),
        name="attn_prompt",
    )(qi_tab, kb_tab, kind_tab, first_tab, aq_bf, akt_bf, av_bf, lq1, lk1, lq2, lk2, subln)


def _page_copies(pt_ref, item, n_pages, ck_hbm, cv_hbm, kbuf, vbuf, sem, slot):
    page_rows = ck_hbm.shape[1]
    out = []
    for j in range(n_pages):
        page = pt_ref[item * n_pages + j]
        out.append(pltpu.make_async_copy(ck_hbm.at[page], kbuf.at[slot, j], sem.at[slot, 0]))
        out.append(pltpu.make_async_copy(cv_hbm.at[page], vbuf.at[slot, pl.ds(j * page_rows, page_rows)],
                                         sem.at[slot, 1]))
    return out


def _decode_item(lam, lam_init, q, kn, vn, sub4, kpages, vrows):
    n_pages = kpages.shape[0]
    past = n_pages * kpages.shape[2]
    rows = 2 * N_HEADS
    ri = lax.broadcasted_iota(jnp.int32, (rows, GROUP), 0)
    li = lax.broadcasted_iota(jnp.int32, (rows, GROUP), 1)
    mine = (li // A_DK) == (ri % N_HEADS) * 2 + ri // N_HEADS
    qbd = jnp.where(mine, jnp.broadcast_to(q.astype(F32), (rows, GROUP)), 0.0)
    qbd_bf = qbd.astype(BF16)
    s = jnp.concatenate(
        [jnp.dot(qbd_bf, kpages[j].astype(BF16), preferred_element_type=F32) for j in range(n_pages)],
        axis=-1)
    s_new = jnp.sum(qbd * kn.astype(F32), axis=-1, keepdims=True)
    m = jnp.maximum(jnp.max(s, axis=-1, keepdims=True), s_new)
    p = jnp.exp2(s - m)
    p_new = jnp.exp2(s_new - m)
    inv_l = 1.0 / (jnp.sum(p, axis=-1, keepdims=True) + p_new)
    pn = p * inv_l
    pn_new = p_new * inv_l
    attn = pn - lam * pltpu.roll(pn, N_HEADS, 0)
    attn_new = pn_new - lam * pltpu.roll(pn_new, N_HEADS, 0)
    attn_bf = attn.astype(BF16)
    o = jnp.concatenate(
        [jnp.dot(attn_bf, vrows[pl.ds(h, past, stride=N_HEADS), :].astype(BF16), preferred_element_type=F32)
         for h in range(N_HEADS)], axis=-1) + attn_new * vn
    keep = jnp.logical_and(ri < N_HEADS, li // HEAD == ri)
    o = jnp.where(keep, o, 0.0)
    ms = jnp.sum(o * o, axis=-1, keepdims=True) * (1.0 / HEAD)
    o = o * lax.rsqrt(ms + EPS) * sub4 * (1.0 - lam_init)
    return jnp.sum(o, axis=0, keepdims=True)


def _tail_kernel(decode, *refs):
    if decode is None:
        (x_ref, or_ref, gate_ref, oa_ref, gn_ref, wo_ref, nf_ref, wg_ref, wu_ref, wd_ref, nfin_ref,
         y_ref, act_scr) = refs
    else:
        (pt_ref, x_ref, or_ref, gate_ref, oa_ref, gn_ref, wo_ref, nf_ref, wg_ref, wu_ref, wd_ref, nfin_ref,
         q_ref, kn_ref, vn_ref, lq1_ref, lk1_ref, lq2_ref, lk2_ref, sub4_ref, ck_hbm, cv_hbm,
         y_ref, os_ref, act_scr, kbuf, vbuf, sem) = refs
        lam_init, n_pages, items = decode
        step = pl.program_id(0)
        copies = functools.partial(_page_copies, pt_ref, n_pages=n_pages, ck_hbm=ck_hbm, cv_hbm=cv_hbm,
                                   kbuf=kbuf, vbuf=vbuf, sem=sem)

        @pl.when(step == 0)
        def _():
            for cp in copies(item=0, slot=0):
                cp.start()

    def decode_item(jj):
        if decode is None:
            return
        item = step * items + jj
        slot = jj % 2
        if jj + 1 < items:
            for cp in copies(item=item + 1, slot=1 - slot):
                cp.start()
        else:
            @pl.when(step + 1 < pl.num_programs(0))
            def _():
                for cp in copies(item=item + 1, slot=1 - slot):
                    cp.start()
        for cp in copies(item=item, slot=slot):
            cp.wait()
        lam = _lambda(lq1_ref, lk1_ref, lq2_ref, lk2_ref, lam_init)
        os_ref[jj] = _decode_item(lam, lam_init, q_ref[jj], kn_ref[jj], vn_ref[jj], sub4_ref[...],
                                  kbuf.at[slot], vbuf.at[slot])

    tm = x_ref.shape[0]
    d_ff = wg_ref.shape[1]
    sub = min(tm, TAIL_ROWS)
    groups = tm // sub
    per_group = 0 if decode is None else items // (2 * groups)
    for g in range(groups):
        rows = slice(g * sub, (g + 1) * sub)
        for jj in range(per_group):
            decode_item(2 * g * per_group + jj)
        parts = []
        for h in range(N_HEADS):
            cs = slice(h * HEAD, (h + 1) * HEAD)
            parts.append((_rms(or_ref[rows, cs].astype(F32), gn_ref[...]) * gate_ref[rows, cs]).astype(BF16))
        parts.append(oa_ref[rows, :].astype(BF16))
        mix = jnp.concatenate(parts, axis=-1)
        x1 = x_ref[rows, :] + jnp.dot(mix, wo_ref[...], preferred_element_type=F32)
        h2 = _rms(x1, nf_ref[...]).astype(BF16)
        n_ff = d_ff // FF_CHUNK
        for j in range(n_ff):
            if j == n_ff // 2:
                for jj in range(per_group):
                    decode_item((2 * g + 1) * per_group + jj)
            fs = slice(j * FF_CHUNK, (j + 1) * FF_CHUNK)
            gt = jnp.dot(h2, wg_ref[:, fs], preferred_element_type=F32)
            up = jnp.dot(h2, wu_ref[:, fs], preferred_element_type=F32)
            act_scr[rows, fs] = (gt * _sigmoid(gt) * up).astype(BF16)
        y = x1 + jnp.dot(act_scr[rows, :], wd_ref[...], preferred_element_type=F32)
        y_ref[rows, :] = _rms(y, nfin_ref[...])


def _tail(x, o_r, gate, o_a, r_gnorm, wo_bf, norm_ffn, wg_bf, wu_bf, wd_bf, norm_final, tm, decode=None):
    rows, d = x.shape
    d_ff = wg_bf.shape[1]
    steps = rows // tm

    def resident(shape):
        return pl.BlockSpec(shape, lambda i, *_: (0, 0), pipeline_mode=pl.Buffered(1))

    def const(shape):
        return pl.BlockSpec(shape, lambda i, *_: (0,) * len(shape))

    def row_block(width):
        return pl.BlockSpec((tm, width), lambda i, *_: (i, 0))

    in_specs = [row_block(d), row_block(GROUP), row_block(GROUP), row_block(GROUP), const((1, HEAD)),
                resident((d, d)), const((1, d)), resident((d, d_ff)), resident((d, d_ff)), resident((d_ff, d)),
                const((1, d))]
    operands = [x, o_r, gate, o_a, r_gnorm, wo_bf, norm_ffn, wg_bf, wu_bf, wd_bf, norm_final]
    out_specs = [row_block(d)]
    out_shape = [jax.ShapeDtypeStruct((rows, d), F32)]
    scratch = [pltpu.VMEM((tm, d_ff), BF16)]
    config = None
    prefetch = []
    if decode is not None:
        lam_init, page_table, aq_bf, ak_bf, av, lams, subln4, cache_k, cache_v = decode
        n, n_pages = page_table.shape
        n_phys, page_size = cache_k.shape[0], cache_k.shape[1]
        items = n // steps
        assert items * steps == n and items % (2 * (tm // min(tm, TAIL_ROWS))) == 0
        ck = jnp.transpose(cache_k, (0, 2, 3, 4, 1)).reshape(n_phys, GROUP, page_size)
        cv = cache_v.reshape(n_phys, page_size * N_HEADS, HEAD)
        item_block = pl.BlockSpec((items, 1, GROUP), lambda i, *_: (i, 0, 0))
        in_specs += [item_block, item_block, item_block] + [const((1, A_DK))] * 4 + [
            const((1, GROUP)), pl.BlockSpec(memory_space=pl.ANY), pl.BlockSpec(memory_space=pl.ANY)]
        operands += [a.reshape(n, 1, GROUP) for a in (aq_bf, ak_bf, av)] + list(lams) + [subln4, ck, cv]
        out_specs.append(item_block)
        out_shape.append(jax.ShapeDtypeStruct((n, 1, GROUP), F32))
        scratch += [pltpu.VMEM((2, n_pages, GROUP, page_size), F32),
                    pltpu.VMEM((2, n_pages * page_size * N_HEADS, HEAD), F32),
                    pltpu.SemaphoreType.DMA((2, 2))]
        config = (lam_init, n_pages, items)
        prefetch = [page_table.reshape(-1)]
    out = pl.pallas_call(
        functools.partial(_tail_kernel, config),
        grid_spec=pltpu.PrefetchScalarGridSpec(
            num_scalar_prefetch=len(prefetch), grid=(steps,),
            in_specs=in_specs, out_specs=out_specs, scratch_shapes=scratch),
        out_shape=out_shape,
        compiler_params=pltpu.CompilerParams(
            dimension_semantics=("arbitrary",), vmem_limit_bytes=TAIL_VMEM_LIMIT),
        name="tail",
    )(*prefetch, *operands)
    if decode is None:
        return out[0]
    return out[0], out[1].reshape(n, GROUP)


def kernel(x_prompt, x_sample, cache_k, cache_v, state_hgrn, page_table, w_in, w_out, lb_param,
           r_gnorm, lam_q1, lam_k1, lam_q2, lam_k2, a_subln, norm_mix, norm_ffn, w_gate, w_up,
           w_down, norm_final):
    batch, seq, d = x_prompt.shape
    n_dec = x_sample.shape[0]
    depth = w_in.shape[0]
    assert depth == 1 and x_sample.shape[1] == 1
    hp = x_prompt.reshape(batch * seq, d)
    hs = x_sample.reshape(n_dec, d)
    lb_param = lb_param.astype(F32)
    nfin = norm_final.reshape(1, d)
    kp, vp, sp, kss, vss, sss = [], [], [], [], [], []
    for l in range(depth):
        lam_init = 0.8 - 0.6 * math.exp(-0.3 * l)
        w_in_bf = w_in[l].astype(BF16)
        wo_bf = w_out[l].astype(BF16)
        wg_bf = w_gate[l].astype(BF16)
        wu_bf = w_up[l].astype(BF16)
        wd_bf = w_down[l].astype(BF16)
        nmix = norm_mix[l].reshape(1, d)
        nffn = norm_ffn[l].reshape(1, d)
        gn = r_gnorm[l].reshape(1, HEAD)
        sub = a_subln[l].reshape(1, HEAD)
        lams = [a[l].reshape(1, A_DK).astype(F32) for a in (lam_q1, lam_k1, lam_q2, lam_k2)]

        qr_s, logf_s, kr_s, vr_s, gate_s, aq_s, akt_s, av_s, ak_s = _inproj(
            hs, nmix, w_in_bf, lb_param, l, n_dec, n_dec, prompt=False)

        assert IN_ROWS == HGRN_BLOCK
        qr, cum_logf, kr, vr, gate, aq_bf, akt, av_heads, akt_bf, av_bf, worst = _inproj(
            hp, nmix, w_in_bf, lb_param, l, ROW_BLOCK, seq, prompt=True)
        o_r, s_new, o_r_s, s_new_s = _hgrn(qr, cum_logf, kr, vr, worst, batch, seq,
                                           qr_s, logf_s, kr_s, vr_s, state_hgrn[l].astype(F32))
        o_a = _attn_prompt(aq_bf, akt_bf, av_bf, *lams, sub, lam_init, batch, seq)
        paged = (lam_init, page_table, aq_s, ak_s, av_s, lams, jnp.tile(sub, (1, N_HEADS)),
                 cache_k[l], cache_v[l])
        hp, o_a_s = _tail(hp, o_r, gate, o_a, gn, wo_bf, nffn, wg_bf, wu_bf, wd_bf, nfin, TAIL_BLOCK, paged)
        kp.append(jnp.transpose(akt.reshape(batch, N_HEADS, 2, A_DK, seq), (0, 4, 1, 2, 3)))
        vp.append(av_heads.reshape(batch, seq, N_HEADS, HEAD))
        sp.append(s_new)

        hs = _tail(hs, o_r_s, gate_s, o_a_s, gn, wo_bf, nffn, wg_bf, wu_bf, wd_bf, nfin, n_dec)
        kss.append(jnp.transpose(akt_s.reshape(N_HEADS, 2, A_DK, n_dec), (3, 0, 1, 2))[:, None])
        vss.append(av_s.reshape(n_dec, 1, N_HEADS, HEAD))
        sss.append(s_new_s)

    y_prompt = hp.reshape(batch, seq, d)
    y_sample = hs.reshape(n_dec, 1, d)
    return (y_prompt, y_sample, jnp.stack(kp), jnp.stack(vp), jnp.stack(sp),
            jnp.stack(kss), jnp.stack(vss), jnp.stack(sss))
```

```python
import functools
import math

import jax
import jax.numpy as jnp
from jax import lax
from jax.experimental import pallas as pl
from jax.experimental.pallas import tpu as pltpu

F32 = jnp.float32
BF16 = jnp.bfloat16

EPS = 1e-6
LOG2E = math.log2(math.e)
HEAD = 128
N_HEADS = 4
A_DK = 64
GROUP = N_HEADS * HEAD
SUBLANES = 8
HGRN_CHUNK = 128
HGRN_BLOCK = 256
HGRN_SAFE_EXPONENT = 60.0
ATT_BLOCK = 2048
ATT_KEY_BLOCK = 2048
ATT_ROWS = 128
ROW_BLOCK = 512
IN_ROWS = 256
TAIL_BLOCK = 256
TAIL_ROWS = 256
TAIL_VMEM_LIMIT = 56 * 1024 * 1024
FF_CHUNK = 256
VMEM_LIMIT = 48 * 1024 * 1024


def _sigmoid(x):
    return 1.0 / (1.0 + jnp.exp(-x))


def _rms(x, w):
    return x * lax.rsqrt(jnp.mean(x * x, axis=-1, keepdims=True) + EPS) * w


def _const_spec(shape):
    return pl.BlockSpec(shape, lambda *_: (0,) * len(shape))


def _inproj_kernel(layer, prompt, x_ref, nw_ref, w_ref, lb_ref,
                   qr_ref, logf_ref, kr_ref, vr_ref, gate_ref, aq_ref, akt_ref, *kv_refs):
    lbp = lb_ref[...]
    e = jnp.exp(lbp - jnp.max(lbp, axis=0, keepdims=True))
    lb = jnp.sum(e[:layer + 1], axis=0, keepdims=True) / jnp.sum(e, axis=0, keepdims=True)

    tm = x_ref.shape[0]
    sub = min(tm, IN_ROWS)
    for g in range(tm // sub):
        rows = slice(g * sub, (g + 1) * sub)
        h = _rms(x_ref[rows, :], nw_ref[...]).astype(BF16)

        def piece(j, h=h):
            return jnp.dot(h, w_ref[:, j * GROUP:(j + 1) * GROUP], preferred_element_type=F32)

        rq = piece(0)
        qr_ref[rows, :] = (rq * _sigmoid(rq)).astype(qr_ref.dtype)
        f = lb + (1.0 - lb) * _sigmoid(piece(1))
        logf = jnp.log(f)
        if prompt:
            c, half = HGRN_CHUNK, HGRN_CHUNK // 2
            worst = jnp.zeros((1, GROUP), F32)
            for ci in range(sub // c):
                b = _cumsum_rows(logf[ci * c:(ci + 1) * c])
                logf_ref[g * sub + ci * c:g * sub + (ci + 1) * c, :] = b
                b_mid = b[half - 1:half, :]
                worst = jnp.maximum(worst, jnp.maximum(-b_mid, b_mid - b[c - 1:c, :]))
            worst = functools.reduce(jnp.maximum,
                                     [worst[:, hd * HEAD:(hd + 1) * HEAD] for hd in range(N_HEADS)])
        else:
            logf_ref[rows, :] = logf
        kr_ref[rows, :] = (1.0 - f).astype(kr_ref.dtype)
        vr_ref[rows, :] = piece(2).astype(vr_ref.dtype)
        rg = piece(3)
        gate_ref[rows, :] = (rg * _sigmoid(rg)).astype(gate_ref.dtype)
        aq_ref[rows, :] = (piece(4) * (A_DK ** -0.5 * LOG2E)).astype(BF16)
        ak = piece(5)
        akt = ak.T
        akt_ref[0, :, rows] = akt
        av = piece(6)
        if prompt:
            avh_ref, aktb_ref, avb_ref, worst_ref = kv_refs
            worst_ref[g] = jnp.broadcast_to(worst, (SUBLANES, HEAD))
            for hd in range(N_HEADS):
                avh_ref[pl.ds(g * sub * N_HEADS + hd, sub, stride=N_HEADS), :] = av[:, hd * HEAD:(hd + 1) * HEAD]
            aktb_ref[0, :, rows] = akt.astype(BF16)
            avb_ref[rows, :] = av.astype(BF16)
        else:
            av_ref, akb_ref = kv_refs
            av_ref[rows, :] = av
            akb_ref[rows, :] = ak.astype(BF16)


def _inproj(x, norm_w, w_bf, lb_param, layer, tm, seq, prompt):
    rows, d = x.shape
    d_in = w_bf.shape[1]
    per_seq = seq // tm
    f32_out = jax.ShapeDtypeStruct((rows, GROUP), F32)
    bf_out = jax.ShapeDtypeStruct((rows, GROUP), BF16)
    row_spec = pl.BlockSpec((tm, GROUP), lambda i: (i, 0))
    kt_spec = pl.BlockSpec((1, GROUP, tm), lambda i: (i // per_seq, 0, i % per_seq))
    kt_shape = (rows // seq, GROUP, seq)
    if prompt:
        narrow = bf_out
        sub = min(tm, IN_ROWS)
        assert sub % HGRN_CHUNK == 0 and tm % sub == 0
        kv_specs = [pl.BlockSpec((tm * N_HEADS, HEAD), lambda i: (i, 0)), kt_spec, row_spec,
                    pl.BlockSpec((tm // sub, SUBLANES, HEAD), lambda i: (i, 0, 0))]
        kv_shapes = [jax.ShapeDtypeStruct((rows * N_HEADS, HEAD), F32),
                     jax.ShapeDtypeStruct(kt_shape, BF16), bf_out,
                     jax.ShapeDtypeStruct((rows // sub, SUBLANES, HEAD), F32)]
    else:
        narrow = f32_out
        kv_specs = [row_spec, row_spec]
        kv_shapes = [f32_out, bf_out]
    return pl.pallas_call(
        functools.partial(_inproj_kernel, layer, prompt),
        grid=(rows // tm,),
        in_specs=[pl.BlockSpec((tm, d), lambda i: (i, 0)),
                  _const_spec((1, d)),
                  pl.BlockSpec((d, d_in), lambda i: (0, 0), pipeline_mode=pl.Buffered(1)),
                  _const_spec(lb_param.shape)],
        out_specs=[row_spec] * 6 + [kt_spec] + kv_specs,
        out_shape=[narrow, f32_out, narrow, narrow, narrow, bf_out,
                   jax.ShapeDtypeStruct(kt_shape, F32)] + kv_shapes,
        compiler_params=pltpu.CompilerParams(
            dimension_semantics=("parallel",), vmem_limit_bytes=VMEM_LIMIT),
        name="inproj",
    )(x, norm_w, w_bf, lb_param)


def _cumsum_rows(x):
    n, lanes = x.shape
    groups = n // SUBLANES
    x3 = x.reshape(groups, SUBLANES, lanes)
    sub = lax.broadcasted_iota(jnp.int32, x3.shape, 1)
    shift = 1
    while shift < SUBLANES:
        x3 = x3 + jnp.where(sub >= shift, pltpu.roll(x3, shift, 1), 0.0)
        shift *= 2
    totals = x3[:, SUBLANES - 1:SUBLANES, :]
    pre = totals
    shift = 1
    while shift < groups:
        pre = pre + jnp.concatenate([jnp.zeros((shift, 1, lanes), F32), pre[:groups - shift]], axis=0)
        shift *= 2
    return (x3 + (pre - totals)).reshape(n, lanes)


def _hgrn_sample_step(q_ref, g_ref, k_ref, v_ref, s_in_ref, o_ref, s_out_ref):
    n = q_ref.shape[1]
    assert n <= SUBLANES

    def columns(x):
        return jnp.concatenate([x] * (SUBLANES // n), axis=0).T

    for h in range(N_HEADS):
        cs = slice(h * HEAD, (h + 1) * HEAD)
        qt = columns(q_ref[0, :, cs])
        ft = columns(jnp.exp(g_ref[0, :, cs]))
        kt = columns(k_ref[0, :, cs])
        for j in range(n):
            s_new = ft[:, j:j + 1] * s_in_ref[j, h] + kt[:, j:j + 1] * v_ref[0, j:j + 1, cs]
            s_out_ref[j, h] = s_new
            o_ref[0, j:j + 1, cs] = jnp.sum(qt[:, j:j + 1] * s_new, axis=0, keepdims=True)


def _hgrn_prompt_kernel(q_ref, b_ref, k_ref, v_ref, worst_ref, qs_ref, gs_ref, ks_ref, vs_ref, ss_in_ref,
                        o_ref, s_ref, os_ref, ss_out_ref, st_scr, oi_scr):
    i = pl.program_id(0)
    sample_step = functools.partial(_hgrn_sample_step, qs_ref, gs_ref, ks_ref, vs_ref, ss_in_ref,
                                    os_ref, ss_out_ref)

    @pl.when(i == 0)
    def _():
        st_scr[...] = jnp.zeros_like(st_scr)

    c = HGRN_CHUNK
    half = c // 2
    row = lax.broadcasted_iota(jnp.int32, (c, c), 0)
    col = lax.broadcasted_iota(jnp.int32, (c, c), 1)
    tri = col <= row
    nt = (((1,), (1,)), ((), ()))
    tn = (((0,), (0,)), ((), ()))
    chains = [(bi, ci, h) for bi in range(q_ref.shape[0]) for ci in range(HGRN_BLOCK // c)
              for h in range(N_HEADS)]

    def window(bi, ci, h):
        return bi, slice(ci * c, (ci + 1) * c), slice(h * HEAD, (h + 1) * HEAD)

    safe = jnp.max(worst_ref[...]) < HGRN_SAFE_EXPONENT

    def finish(chain, intra, q0, k2, v, b_last):
        bi, _, h = chain
        st = st_scr[bi, h]
        o = lax.dot_general(q0, st.astype(BF16), nt, preferred_element_type=F32) + intra
        o_ref[window(*chain)] = o.astype(o_ref.dtype)
        st_scr[bi, h] = st * jnp.exp(b_last) + lax.dot_general(v, k2, tn, preferred_element_type=F32)

    @pl.when(safe)
    def _():
        sample_step()
        for chain in chains:
            w = window(*chain)
            b = b_ref[w]
            b_mid = b[half - 1:half, :]
            b_last = b[c - 1:c, :]
            d = b - b_mid
            qm = q_ref[w] * jnp.exp(d)
            km = k_ref[w] * jnp.exp(-d)
            q0 = (qm * jnp.exp(b_mid)).astype(BF16)
            k2 = (km * jnp.exp(b_last - b_mid)).astype(BF16)
            v = v_ref[w].astype(BF16)
            a = lax.dot_general(qm.astype(BF16), km.astype(BF16), nt, preferred_element_type=F32)
            a = jnp.where(tri, a, 0.0).astype(BF16)
            finish(chain, jnp.dot(a, v, preferred_element_type=F32), q0, k2, v, b_last)

    @pl.when(jnp.logical_not(safe))
    def _():
        sample_step()
        srow = lax.broadcasted_iota(jnp.int32, (c, HEAD), 0)
        for chain in chains:
            bi, rs, cs = window(*chain)
            b = b_ref[bi, rs, cs]
            b_last = b[c - 1:c, :]
            kk = k_ref[bi, rs, cs]
            vv = v_ref[bi, rs, cs]

            def query_rows(tg, carry, bi=bi, rs=rs, cs=cs, b=b, kk=kk, vv=vv):
                base = pl.multiple_of(tg * SUBLANES, SUBLANES)
                at = pl.ds(rs.start + base, SUBLANES)
                b8 = b_ref[bi, at, cs]
                q8 = q_ref[bi, at, cs]
                out = []
                for j in range(SUBLANES):
                    decay = jnp.exp(jnp.where(srow <= base + j, b8[j:j + 1, :] - b, -jnp.inf))
                    a_col = jnp.sum(q8[j:j + 1, :] * kk * decay, axis=-1, keepdims=True)
                    out.append(jnp.sum(a_col * vv, axis=0, keepdims=True))
                oi_scr[pl.ds(base, SUBLANES), :] = jnp.concatenate(out, axis=0)
                return carry

            lax.fori_loop(0, c // SUBLANES, query_rows, 0)
            q0 = (q_ref[bi, rs, cs] * jnp.exp(b)).astype(BF16)
            k2 = (kk * jnp.exp(b_last - b)).astype(BF16)
            finish(chain, oi_scr[...], q0, k2, vv.astype(BF16), b_last)

    @pl.when(i == pl.num_programs(0) - 1)
    def _():
        for bi in range(q_ref.shape[0]):
            for h in range(N_HEADS):
                s_ref[bi, h] = st_scr[bi, h].T


def _hgrn(qr, cum_logf, kr, vr, worst, batch, seq, qr_s, logf_s, kr_s, vr_s, state_s):
    nb = seq // HGRN_BLOCK
    n_dec = qr_s.shape[0]
    per_step = n_dec // nb
    assert per_step * nb == n_dec
    spec = pl.BlockSpec((batch, HGRN_BLOCK, GROUP), lambda i: (0, i, 0))
    shape3 = (batch, seq, GROUP)
    rows_s = pl.BlockSpec((1, per_step, GROUP), lambda i: (i, 0, 0))
    state_spec = pl.BlockSpec((per_step, N_HEADS, HEAD, HEAD), lambda i: (i, 0, 0, 0))
    o_r, s_new, o_r_s, s_new_s = pl.pallas_call(
        _hgrn_prompt_kernel,
        grid=(nb,),
        in_specs=[spec] * 4 + [pl.BlockSpec((batch, 1, SUBLANES, HEAD), lambda i: (0, i, 0, 0))]
                 + [rows_s] * 4 + [state_spec],
        out_specs=[spec, pl.BlockSpec((batch, N_HEADS, HEAD, HEAD), lambda i: (0, 0, 0, 0)), rows_s, state_spec],
        out_shape=[jax.ShapeDtypeStruct(shape3, BF16),
                   jax.ShapeDtypeStruct((batch, N_HEADS, HEAD, HEAD), F32),
                   jax.ShapeDtypeStruct((nb, per_step, GROUP), F32),
                   jax.ShapeDtypeStruct(state_s.shape, F32)],
        scratch_shapes=[pltpu.VMEM((batch, N_HEADS, HEAD, HEAD), F32),
                        pltpu.VMEM((HGRN_CHUNK, HEAD), F32)],
        compiler_params=pltpu.CompilerParams(
            dimension_semantics=("arbitrary",), vmem_limit_bytes=VMEM_LIMIT),
        name="hgrn",
    )(*(a.reshape(shape3) for a in (qr, cum_logf, kr, vr)), worst.reshape(batch, nb, SUBLANES, HEAD),
      *(a.reshape(nb, per_step, GROUP) for a in (qr_s, logf_s, kr_s, vr_s)), state_s)
    return o_r.reshape(batch * seq, GROUP), s_new, o_r_s.reshape(n_dec, GROUP), s_new_s


def _lambda(lq1_ref, lk1_ref, lq2_ref, lk2_ref, lam_init):
    a = jnp.sum(lq1_ref[...] * lk1_ref[...], axis=-1, keepdims=True)
    b = jnp.sum(lq2_ref[...] * lk2_ref[...], axis=-1, keepdims=True)
    return jnp.exp(a) - jnp.exp(b) + lam_init


ATT_FULL, ATT_FULL_DIAG, ATT_DIAG = 0, 1, 2


def _attn_prompt_kernel(lam_init, qi_tab, kb_tab, kind_tab, first_tab,
                        q_ref, kt_ref, v_ref, lq1_ref, lk1_ref, lq2_ref, lk2_ref, sub_ref,
                        o_ref, m_scr, acc_scr):
    p = pl.program_id(2)
    kind = kind_tab[p]

    @pl.when(first_tab[p] == 1)
    def _():
        m_scr[...] = jnp.full_like(m_scr, -jnp.inf)
        acc_scr[...] = jnp.zeros_like(acc_scr)

    tq, tk = q_ref.shape[0], v_ref.shape[0]
    half_chunks = tq // HEAD

    def sweep(step_kind):
        masked = step_kind != ATT_FULL
        v_ext = jnp.concatenate([v_ref[...], jnp.ones((tk, HEAD), BF16)], axis=-1)
        lane = lax.broadcasted_iota(jnp.int32, (ATT_ROWS, HEAD), 1)
        row = lax.broadcasted_iota(jnp.int32, (ATT_ROWS, HEAD), 0)
        zero = jnp.zeros((ATT_ROWS, HEAD), BF16)
        group_chunks = ATT_ROWS // HEAD
        groups = range(tq // ATT_ROWS)
        for r in (reversed(groups) if masked else groups):
            rows = slice(r * ATT_ROWS, (r + 1) * ATT_ROWS)
            q = q_ref[rows, :]
            diag_chunks = (r + 1) * group_chunks
            n_chunks = {ATT_FULL: tk // HEAD, ATT_FULL_DIAG: half_chunks + diag_chunks,
                        ATT_DIAG: diag_chunks}[step_kind]
            kt = kt_ref[0, :, 0:n_chunks * HEAD]
            probs, alphas = [], []
            for mi, qm in enumerate((jnp.where(lane < A_DK, q, zero), jnp.where(lane >= A_DK, q, zero))):
                s = jnp.dot(qm, kt, preferred_element_type=F32)
                chunks = [s[:, c * HEAD:(c + 1) * HEAD] for c in range(n_chunks)]
                if masked:
                    for g in range(group_chunks):
                        c = n_chunks - group_chunks + g
                        chunks[c] = jnp.where(lane + g * HEAD <= row, chunks[c], -jnp.inf)
                mc = functools.reduce(jnp.maximum, chunks)
                m_prev = m_scr[mi, rows]
                m_new = jnp.maximum(m_prev, jnp.max(mc, axis=-1, keepdims=True))
                alphas.append(jnp.exp2(m_prev - m_new))
                probs.append(jnp.concatenate([jnp.exp2(sc - m_new).astype(BF16) for sc in chunks], axis=-1))
                m_scr[mi, rows] = m_new
            pv = jnp.dot(jnp.concatenate(probs, axis=0), v_ext[0:n_chunks * HEAD], preferred_element_type=F32)
            for mi, alpha in enumerate(alphas):
                acc_scr[mi, rows] = (jnp.concatenate([alpha, alpha], axis=-1) * acc_scr[mi, rows]
                                     + pv[mi * ATT_ROWS:(mi + 1) * ATT_ROWS])

    def finalize():
        lam = _lambda(lq1_ref, lk1_ref, lq2_ref, lk2_ref, lam_init)
        a0 = acc_scr[0]
        a1 = acc_scr[1]
        o = a0[:, :HEAD] / a0[:, HEAD:] - lam * (a1[:, :HEAD] / a1[:, HEAD:])
        o_ref[...] = (_rms(o, sub_ref[...]) * (1.0 - lam_init)).astype(o_ref.dtype)

    @pl.when(kind == ATT_FULL)
    def _():
        sweep(ATT_FULL)

    if tk > tq:
        @pl.when(kind == ATT_FULL_DIAG)
        def _():
            sweep(ATT_FULL_DIAG)
            finalize()

    @pl.when(kind == ATT_DIAG)
    def _():
        sweep(ATT_DIAG)
        finalize()


def _attn_prompt(aq_bf, akt_bf, av_bf, lq1, lk1, lq2, lk2, subln, lam_init, batch, seq):
    t = ATT_BLOCK
    ratio = ATT_KEY_BLOCK // t
    nq = seq // t
    assert ATT_ROWS % HEAD == 0 and ratio in (1, 2) and nq % ratio == 0
    steps = []
    for i in range(nq):
        for j in range(i // ratio):
            steps.append((i, j, ATT_FULL, int(j == 0)))
        steps.append((i, i // ratio, ATT_FULL_DIAG if i % ratio else ATT_DIAG, int(i < ratio)))
    qi_tab, kb_tab, kind_tab, first_tab = (jnp.asarray([s[c] for s in steps], jnp.int32) for c in range(4))
    q_spec = pl.BlockSpec((t, HEAD), lambda b, h, p, qi, kb, kind, first: (b * nq + qi[p], h))
    v_spec = pl.BlockSpec((ratio * t, HEAD),
                          lambda b, h, p, qi, kb, kind, first: (b * (nq // ratio) + kb[p], h))
    kt_spec = pl.BlockSpec((1, HEAD, ratio * t), lambda b, h, p, qi, kb, kind, first: (b, h, kb[p]))
    lam_spec = pl.BlockSpec((1, A_DK), lambda *_: (0, 0))
    sub_spec = pl.BlockSpec((1, HEAD), lambda *_: (0, 0))
    return pl.pallas_call(
        functools.partial(_attn_prompt_kernel, lam_init),
        grid_spec=pltpu.PrefetchScalarGridSpec(
            num_scalar_prefetch=4,
            grid=(batch, N_HEADS, len(steps)),
            in_specs=[q_spec, kt_spec, v_spec, lam_spec, lam_spec, lam_spec, lam_spec, sub_spec],
            out_specs=q_spec,
            scratch_shapes=[pltpu.VMEM((2, t, HEAD), F32), pltpu.VMEM((2, t, 2 * HEAD), F32)]),
        out_shape=jax.ShapeDtypeStruct((batch * seq, GROUP), BF16),
        compiler_params=pltpu.CompilerParams(
            dimension_semantics=("parallel", "parallel", "arbitrary"), vmem_limit_bytes=VMEM_LIMIT),
        name="attn_prompt",
    )(qi_tab, kb_tab, kind_tab, first_tab, aq_bf, akt_bf, av_bf, lq1, lk1, lq2, lk2, subln)


def _page_copies(pt_ref, item, n_pages, ck_hbm, cv_hbm, kbuf, vbuf, sem, slot):
    page_rows = ck_hbm.shape[1]
    out = []
    for j in range(n_pages):
        page = pt_ref[item * n_pages + j]
        out.append(pltpu.make_async_copy(ck_hbm.at[page], kbuf.at[slot, j], sem.at[slot, 0]))
        out.append(pltpu.make_async_copy(cv_hbm.at[page], vbuf.at[slot, pl.ds(j * page_rows, page_rows)],
                                         sem.at[slot, 1]))
    return out


def _decode_item(lam, lam_init, q, kn, vn, sub4, kpages, vrows):
    n_pages = kpages.shape[0]
    past = n_pages * kpages.shape[2]
    rows = 2 * N_HEADS
    ri = lax.broadcasted_iota(jnp.int32, (rows, GROUP), 0)
    li = lax.broadcasted_iota(jnp.int32, (rows, GROUP), 1)
    mine = (li // A_DK) == (ri % N_HEADS) * 2 + ri // N_HEADS
    qbd = jnp.where(mine, jnp.broadcast_to(q.astype(F32), (rows, GROUP)), 0.0)
    qbd_bf = qbd.astype(BF16)
    s = jnp.concatenate(
        [jnp.dot(qbd_bf, kpages[j].astype(BF16), preferred_element_type=F32) for j in range(n_pages)],
        axis=-1)
    s_new = jnp.sum(qbd * kn.astype(F32), axis=-1, keepdims=True)
    m = jnp.maximum(jnp.max(s, axis=-1, keepdims=True), s_new)
    p = jnp.exp2(s - m)
    p_new = jnp.exp2(s_new - m)
    inv_l = 1.0 / (jnp.sum(p, axis=-1, keepdims=True) + p_new)
    pn = p * inv_l
    pn_new = p_new * inv_l
    attn = pn - lam * pltpu.roll(pn, N_HEADS, 0)
    attn_new = pn_new - lam * pltpu.roll(pn_new, N_HEADS, 0)
    attn_bf = attn.astype(BF16)
    o = jnp.concatenate(
        [jnp.dot(attn_bf, vrows[pl.ds(h, past, stride=N_HEADS), :].astype(BF16), preferred_element_type=F32)
         for h in range(N_HEADS)], axis=-1) + attn_new * vn
    keep = jnp.logical_and(ri < N_HEADS, li // HEAD == ri)
    o = jnp.where(keep, o, 0.0)
    ms = jnp.sum(o * o, axis=-1, keepdims=True) * (1.0 / HEAD)
    o = o * lax.rsqrt(ms + EPS) * sub4 * (1.0 - lam_init)
    return jnp.sum(o, axis=0, keepdims=True)


def _tail_kernel(decode, *refs):
    if decode is None:
        (x_ref, or_ref, gate_ref, oa_ref, gn_ref, wo_ref, nf_ref, wg_ref, wu_ref, wd_ref, nfin_ref,
         y_ref, act_scr) = refs
    else:
        (pt_ref, x_ref, or_ref, gate_ref, oa_ref, gn_ref, wo_ref, nf_ref, wg_ref, wu_ref, wd_ref, nfin_ref,
         q_ref, kn_ref, vn_ref, lq1_ref, lk1_ref, lq2_ref, lk2_ref, sub4_ref, ck_hbm, cv_hbm,
         y_ref, os_ref, act_scr, kbuf, vbuf, sem) = refs
        lam_init, n_pages, items = decode
        step = pl.program_id(0)
        copies = functools.partial(_page_copies, pt_ref, n_pages=n_pages, ck_hbm=ck_hbm, cv_hbm=cv_hbm,
                                   kbuf=kbuf, vbuf=vbuf, sem=sem)

        @pl.when(step == 0)
        def _():
            for jj in range(items):
                for cp in copies(item=jj, slot=jj):
                    cp.start()

    def decode_items():
        for jj in range(items):
            for cp in copies(item=step * items + jj, slot=jj):
                cp.wait()
        lam = _lambda(lq1_ref, lk1_ref, lq2_ref, lk2_ref, lam_init)
        for jj in range(items):
            os_ref[jj] = _decode_item(lam, lam_init, q_ref[jj], kn_ref[jj], vn_ref[jj], sub4_ref[...],
                                      kbuf.at[jj], vbuf.at[jj])

    def prefetch_items():
        @pl.when(step + 1 < pl.num_programs(0))
        def _():
            for jj in range(items):
                for cp in copies(item=(step + 1) * items + jj, slot=jj):
                    cp.start()

    tm = x_ref.shape[0]
    d_ff = wg_ref.shape[1]
    sub = min(tm, TAIL_ROWS)
    groups = tm // sub
    for g in range(groups):
        rows = slice(g * sub, (g + 1) * sub)
        if decode is not None and g == 0:
            decode_items()
        parts = []
        for h in range(N_HEADS):
            cs = slice(h * HEAD, (h + 1) * HEAD)
            parts.append((_rms(or_ref[rows, cs].astype(F32), gn_ref[...]) * gate_ref[rows, cs]).astype(BF16))
        parts.append(oa_ref[rows, :].astype(BF16))
        mix = jnp.concatenate(parts, axis=-1)
        x1 = x_ref[rows, :] + jnp.dot(mix, wo_ref[...], preferred_element_type=F32)
        h2 = _rms(x1, nf_ref[...]).astype(BF16)
        if decode is not None and g == 0:
            prefetch_items()
        for j in range(d_ff // FF_CHUNK):
            fs = slice(j * FF_CHUNK, (j + 1) * FF_CHUNK)
            gt = jnp.dot(h2, wg_ref[:, fs], preferred_element_type=F32)
            up = jnp.dot(h2, wu_ref[:, fs], preferred_element_type=F32)
            act_scr[rows, fs] = (gt * _sigmoid(gt) * up).astype(BF16)
        y = x1 + jnp.dot(act_scr[rows, :], wd_ref[...], preferred_element_type=F32)
        y_ref[rows, :] = _rms(y, nfin_ref[...])


def _tail(x, o_r, gate, o_a, r_gnorm, wo_bf, norm_ffn, wg_bf, wu_bf, wd_bf, norm_final, tm, decode=None):
    rows, d = x.shape
    d_ff = wg_bf.shape[1]
    steps = rows // tm

    def resident(shape):
        return pl.BlockSpec(shape, lambda i, *_: (0, 0), pipeline_mode=pl.Buffered(1))

    def const(shape):
        return pl.BlockSpec(shape, lambda i, *_: (0,) * len(shape))

    def row_block(width):
        return pl.BlockSpec((tm, width), lambda i, *_: (i, 0))

    in_specs = [row_block(d), row_block(GROUP), row_block(GROUP), row_block(GROUP), const((1, HEAD)),
                resident((d, d)), const((1, d)), resident((d, d_ff)), resident((d, d_ff)), resident((d_ff, d)),
                const((1, d))]
    operands = [x, o_r, gate, o_a, r_gnorm, wo_bf, norm_ffn, wg_bf, wu_bf, wd_bf, norm_final]
    out_specs = [row_block(d)]
    out_shape = [jax.ShapeDtypeStruct((rows, d), F32)]
    scratch = [pltpu.VMEM((tm, d_ff), BF16)]
    config = None
    prefetch = []
    if decode is not None:
        lam_init, page_table, aq_bf, ak_bf, av, lams, subln4, cache_k, cache_v = decode
        n, n_pages = page_table.shape
        n_phys, page_size = cache_k.shape[0], cache_k.shape[1]
        items = n // steps
        assert items * steps == n
        ck = jnp.transpose(cache_k, (0, 2, 3, 4, 1)).reshape(n_phys, GROUP, page_size)
        cv = cache_v.reshape(n_phys, page_size * N_HEADS, HEAD)
        item_block = pl.BlockSpec((items, 1, GROUP), lambda i, *_: (i, 0, 0))
        in_specs += [item_block, item_block, item_block] + [const((1, A_DK))] * 4 + [
            const((1, GROUP)), pl.BlockSpec(memory_space=pl.ANY), pl.BlockSpec(memory_space=pl.ANY)]
        operands += [a.reshape(n, 1, GROUP) for a in (aq_bf, ak_bf, av)] + list(lams) + [subln4, ck, cv]
        out_specs.append(item_block)
        out_shape.append(jax.ShapeDtypeStruct((n, 1, GROUP), F32))
        scratch += [pltpu.VMEM((items, n_pages, GROUP, page_size), F32),
                    pltpu.VMEM((items, n_pages * page_size * N_HEADS, HEAD), F32),
                    pltpu.SemaphoreType.DMA((items, 2))]
        config = (lam_init, n_pages, items)
        prefetch = [page_table.reshape(-1)]
    out = pl.pallas_call(
        functools.partial(_tail_kernel, config),
        grid_spec=pltpu.PrefetchScalarGridSpec(
            num_scalar_prefetch=len(prefetch), grid=(steps,),
            in_specs=in_specs, out_specs=out_specs, scratch_shapes=scratch),
        out_shape=out_shape,
        compiler_params=pltpu.CompilerParams(
            dimension_semantics=("arbitrary",), vmem_limit_bytes=TAIL_VMEM_LIMIT),
        name="tail",
    )(*prefetch, *operands)
    if decode is None:
        return out[0]
    return out[0], out[1].reshape(n, GROUP)


def kernel(x_prompt, x_sample, cache_k, cache_v, state_hgrn, page_table, w_in, w_out, lb_param,
           r_gnorm, lam_q1, lam_k1, lam_q2, lam_k2, a_subln, norm_mix, norm_ffn, w_gate, w_up,
           w_down, norm_final):
    batch, seq, d = x_prompt.shape
    n_dec = x_sample.shape[0]
    depth = w_in.shape[0]
    assert depth == 1 and x_sample.shape[1] == 1
    hp = x_prompt.reshape(batch * seq, d)
    hs = x_sample.reshape(n_dec, d)
    lb_param = lb_param.astype(F32)
    nfin = norm_final.reshape(1, d)
    kp, vp, sp, kss, vss, sss = [], [], [], [], [], []
    for l in range(depth):
        lam_init = 0.8 - 0.6 * math.exp(-0.3 * l)
        w_in_bf = w_in[l].astype(BF16)
        wo_bf = w_out[l].astype(BF16)
        wg_bf = w_gate[l].astype(BF16)
        wu_bf = w_up[l].astype(BF16)
        wd_bf = w_down[l].astype(BF16)
        nmix = norm_mix[l].reshape(1, d)
        nffn = norm_ffn[l].reshape(1, d)
        gn = r_gnorm[l].reshape(1, HEAD)
        sub = a_subln[l].reshape(1, HEAD)
        lams = [a[l].reshape(1, A_DK).astype(F32) for a in (lam_q1, lam_k1, lam_q2, lam_k2)]

        qr_s, logf_s, kr_s, vr_s, gate_s, aq_s, akt_s, av_s, ak_s = _inproj(
            hs, nmix, w_in_bf, lb_param, l, n_dec, n_dec, prompt=False)

        assert IN_ROWS == HGRN_BLOCK
        qr, cum_logf, kr, vr, gate, aq_bf, akt, av_heads, akt_bf, av_bf, worst = _inproj(
            hp, nmix, w_in_bf, lb_param, l, ROW_BLOCK, seq, prompt=True)
        o_r, s_new, o_r_s, s_new_s = _hgrn(qr, cum_logf, kr, vr, worst, batch, seq,
                                           qr_s, logf_s, kr_s, vr_s, state_hgrn[l].astype(F32))
        o_a = _attn_prompt(aq_bf, akt_bf, av_bf, *lams, sub, lam_init, batch, seq)
        paged = (lam_init, page_table, aq_s, ak_s, av_s, lams, jnp.tile(sub, (1, N_HEADS)),
                 cache_k[l], cache_v[l])
        hp, o_a_s = _tail(hp, o_r, gate, o_a, gn, wo_bf, nffn, wg_bf, wu_bf, wd_bf, nfin, TAIL_BLOCK, paged)
        kp.append(jnp.transpose(akt.reshape(batch, N_HEADS, 2, A_DK, seq), (0, 4, 1, 2, 3)))
        vp.append(av_heads.reshape(batch, seq, N_HEADS, HEAD))
        sp.append(s_new)

        hs = _tail(hs, o_r_s, gate_s, o_a_s, gn, wo_bf, nffn, wg_bf, wu_bf, wd_bf, nfin, n_dec)
        kss.append(jnp.transpose(akt_s.reshape(N_HEADS, 2, A_DK, n_dec), (3, 0, 1, 2))[:, None])
        vss.append(av_s.reshape(n_dec, 1, N_HEADS, HEAD))
        sss.append(s_new_s)

    y_prompt = hp.reshape(batch, seq, d)
    y_sample = hs.reshape(n_dec, 1, d)
    return (y_prompt, y_sample, jnp.stack(kp), jnp.stack(vp), jnp.stack(sp),
            jnp.stack(kss), jnp.stack(vss), jnp.stack(sss))
```

```python
import functools
import math

import jax
import jax.numpy as jnp
from jax import lax
from jax.experimental import pallas as pl
from jax.experimental.pallas import tpu as pltpu

F32 = jnp.float32
BF16 = jnp.bfloat16

EPS = 1e-6
LOG2E = math.log2(math.e)
HEAD = 128
N_HEADS = 4
A_DK = 64
GROUP = N_HEADS * HEAD
N_PIECES = 7
SUBLANES = 8
HGRN_CHUNK = 128
HGRN_BLOCK = 256
HGRN_SAFE_EXPONENT = 60.0
ATT_BLOCK = 2048
ATT_KEY_BLOCK = 2048
ATT_ROWS = 128
ROW_BLOCK = 512
IN_ROWS = 256
TAIL_BLOCK = 256
TAIL_ROWS = 256
TAIL_VMEM_LIMIT = 56 * 1024 * 1024
FF_CHUNK = 256
VMEM_LIMIT = 48 * 1024 * 1024


def _sigmoid(x):
    return 1.0 / (1.0 + jnp.exp(-x))


def _rms(x, w):
    return x * lax.rsqrt(jnp.mean(x * x, axis=-1, keepdims=True) + EPS) * w


def _const_spec(shape):
    return pl.BlockSpec(shape, lambda *_: (0,) * len(shape))


def _inproj_kernel(layer, prompt, x_ref, nw_ref, w_ref, lb_ref,
                   qr_ref, logf_ref, kr_ref, vr_ref, gate_ref, aq_ref, akt_ref, *kv_refs):
    lbp = lb_ref[...]
    e = jnp.exp(lbp - jnp.max(lbp, axis=0, keepdims=True))
    lb = jnp.sum(e[:layer + 1], axis=0, keepdims=True) / jnp.sum(e, axis=0, keepdims=True)

    tm = x_ref.shape[0]
    sub = min(tm, IN_ROWS)
    for g in range(tm // sub):
        rows = slice(g * sub, (g + 1) * sub)
        h = _rms(x_ref[rows, :], nw_ref[...]).astype(BF16)

        def piece(j, h=h):
            return jnp.dot(h, w_ref[:, j * GROUP:(j + 1) * GROUP], preferred_element_type=F32)

        rq = piece(0)
        qr_ref[rows, :] = (rq * _sigmoid(rq)).astype(qr_ref.dtype)
        f = lb + (1.0 - lb) * _sigmoid(piece(1))
        logf = jnp.log(f)
        if prompt:
            c, half = HGRN_CHUNK, HGRN_CHUNK // 2
            worst = jnp.zeros((1, GROUP), F32)
            for ci in range(sub // c):
                b = _cumsum_rows(logf[ci * c:(ci + 1) * c])
                logf_ref[g * sub + ci * c:g * sub + (ci + 1) * c, :] = b
                b_mid = b[half - 1:half, :]
                worst = jnp.maximum(worst, jnp.maximum(-b_mid, b_mid - b[c - 1:c, :]))
            worst = functools.reduce(jnp.maximum,
                                     [worst[:, hd * HEAD:(hd + 1) * HEAD] for hd in range(N_HEADS)])
        else:
            logf_ref[rows, :] = logf
        kr_ref[rows, :] = (1.0 - f).astype(kr_ref.dtype)
        vr_ref[rows, :] = piece(2).astype(vr_ref.dtype)
        rg = piece(3)
        gate_ref[rows, :] = (rg * _sigmoid(rg)).astype(gate_ref.dtype)
        aq_ref[rows, :] = (piece(4) * (A_DK ** -0.5 * LOG2E)).astype(BF16)
        ak = piece(5)
        akt = ak.T
        akt_ref[0, :, rows] = akt
        av = piece(6)
        if prompt:
            avh_ref, aktb_ref, avb_ref, worst_ref = kv_refs
            worst_ref[g] = jnp.broadcast_to(worst, (SUBLANES, HEAD))
            for hd in range(N_HEADS):
                avh_ref[pl.ds(g * sub * N_HEADS + hd, sub, stride=N_HEADS), :] = av[:, hd * HEAD:(hd + 1) * HEAD]
            aktb_ref[0, :, rows] = akt.astype(BF16)
            avb_ref[rows, :] = av.astype(BF16)
        else:
            av_ref, akb_ref = kv_refs
            av_ref[rows, :] = av
            akb_ref[rows, :] = ak.astype(BF16)


def _inproj(x, norm_w, w_bf, lb_param, layer, tm, seq, prompt):
    rows, d = x.shape
    d_in = w_bf.shape[1]
    per_seq = seq // tm
    f32_out = jax.ShapeDtypeStruct((rows, GROUP), F32)
    bf_out = jax.ShapeDtypeStruct((rows, GROUP), BF16)
    row_spec = pl.BlockSpec((tm, GROUP), lambda i: (i, 0))
    kt_spec = pl.BlockSpec((1, GROUP, tm), lambda i: (i // per_seq, 0, i % per_seq))
    kt_shape = (rows // seq, GROUP, seq)
    if prompt:
        narrow = bf_out
        sub = min(tm, IN_ROWS)
        assert sub % HGRN_CHUNK == 0 and tm % sub == 0
        kv_specs = [pl.BlockSpec((tm * N_HEADS, HEAD), lambda i: (i, 0)), kt_spec, row_spec,
                    pl.BlockSpec((tm // sub, SUBLANES, HEAD), lambda i: (i, 0, 0))]
        kv_shapes = [jax.ShapeDtypeStruct((rows * N_HEADS, HEAD), F32),
                     jax.ShapeDtypeStruct(kt_shape, BF16), bf_out,
                     jax.ShapeDtypeStruct((rows // sub, SUBLANES, HEAD), F32)]
    else:
        narrow = f32_out
        kv_specs = [row_spec, row_spec]
        kv_shapes = [f32_out, bf_out]
    return pl.pallas_call(
        functools.partial(_inproj_kernel, layer, prompt),
        grid=(rows // tm,),
        in_specs=[pl.BlockSpec((tm, d), lambda i: (i, 0)),
                  _const_spec((1, d)),
                  pl.BlockSpec((d, d_in), lambda i: (0, 0), pipeline_mode=pl.Buffered(1)),
                  _const_spec(lb_param.shape)],
        out_specs=[row_spec] * 6 + [kt_spec] + kv_specs,
        out_shape=[narrow, f32_out, narrow, narrow, narrow, bf_out,
                   jax.ShapeDtypeStruct(kt_shape, F32)] + kv_shapes,
        compiler_params=pltpu.CompilerParams(
            dimension_semantics=("parallel",), vmem_limit_bytes=VMEM_LIMIT),
        name="inproj",
    )(x, norm_w, w_bf, lb_param)


def _cumsum_rows(x):
    n, lanes = x.shape
    groups = n // SUBLANES
    x3 = x.reshape(groups, SUBLANES, lanes)
    sub = lax.broadcasted_iota(jnp.int32, x3.shape, 1)
    shift = 1
    while shift < SUBLANES:
        x3 = x3 + jnp.where(sub >= shift, pltpu.roll(x3, shift, 1), 0.0)
        shift *= 2
    totals = x3[:, SUBLANES - 1:SUBLANES, :]
    pre = totals
    shift = 1
    while shift < groups:
        pre = pre + jnp.concatenate([jnp.zeros((shift, 1, lanes), F32), pre[:groups - shift]], axis=0)
        shift *= 2
    return (x3 + (pre - totals)).reshape(n, lanes)


def _hgrn_sample_step(q_ref, g_ref, k_ref, v_ref, s_in_ref, o_ref, s_out_ref):
    n = q_ref.shape[1]
    assert n <= SUBLANES

    def columns(x):
        return jnp.concatenate([x] * (SUBLANES // n), axis=0).T

    for h in range(N_HEADS):
        cs = slice(h * HEAD, (h + 1) * HEAD)
        qt = columns(q_ref[0, :, cs])
        ft = columns(jnp.exp(g_ref[0, :, cs]))
        kt = columns(k_ref[0, :, cs])
        for j in range(n):
            s_new = ft[:, j:j + 1] * s_in_ref[j, h] + kt[:, j:j + 1] * v_ref[0, j:j + 1, cs]
            s_out_ref[j, h] = s_new
            o_ref[0, j:j + 1, cs] = jnp.sum(qt[:, j:j + 1] * s_new, axis=0, keepdims=True)


def _hgrn_prompt_kernel(q_ref, b_ref, k_ref, v_ref, worst_ref, qs_ref, gs_ref, ks_ref, vs_ref, ss_in_ref,
                        o_ref, s_ref, os_ref, ss_out_ref, st_scr, oi_scr):
    i = pl.program_id(0)
    sample_step = functools.partial(_hgrn_sample_step, qs_ref, gs_ref, ks_ref, vs_ref, ss_in_ref,
                                    os_ref, ss_out_ref)

    @pl.when(i == 0)
    def _():
        st_scr[...] = jnp.zeros_like(st_scr)

    c = HGRN_CHUNK
    half = c // 2
    row = lax.broadcasted_iota(jnp.int32, (c, c), 0)
    col = lax.broadcasted_iota(jnp.int32, (c, c), 1)
    tri = col <= row
    nt = (((1,), (1,)), ((), ()))
    tn = (((0,), (0,)), ((), ()))
    chains = [(bi, ci, h) for bi in range(q_ref.shape[0]) for ci in range(HGRN_BLOCK // c)
              for h in range(N_HEADS)]

    def window(bi, ci, h):
        return bi, slice(ci * c, (ci + 1) * c), slice(h * HEAD, (h + 1) * HEAD)

    safe = jnp.max(worst_ref[...]) < HGRN_SAFE_EXPONENT

    def finish(chain, intra, q0, k2, v, b_last):
        bi, _, h = chain
        st = st_scr[bi, h]
        o = lax.dot_general(q0, st.astype(BF16), nt, preferred_element_type=F32) + intra
        o_ref[window(*chain)] = o.astype(o_ref.dtype)
        st_scr[bi, h] = st * jnp.exp(b_last) + lax.dot_general(v, k2, tn, preferred_element_type=F32)

    @pl.when(safe)
    def _():
        sample_step()
        for chain in chains:
            w = window(*chain)
            b = b_ref[w]
            b_mid = b[half - 1:half, :]
            b_last = b[c - 1:c, :]
            d = b - b_mid
            qm = q_ref[w] * jnp.exp(d)
            km = k_ref[w] * jnp.exp(-d)
            q0 = (qm * jnp.exp(b_mid)).astype(BF16)
            k2 = (km * jnp.exp(b_last - b_mid)).astype(BF16)
            v = v_ref[w].astype(BF16)
            a = lax.dot_general(qm.astype(BF16), km.astype(BF16), nt, preferred_element_type=F32)
            a = jnp.where(tri, a, 0.0).astype(BF16)
            finish(chain, jnp.dot(a, v, preferred_element_type=F32), q0, k2, v, b_last)

    @pl.when(jnp.logical_not(safe))
    def _():
        sample_step()
        srow = lax.broadcasted_iota(jnp.int32, (c, HEAD), 0)
        for chain in chains:
            bi, rs, cs = window(*chain)
            b = b_ref[bi, rs, cs]
            b_last = b[c - 1:c, :]
            kk = k_ref[bi, rs, cs]
            vv = v_ref[bi, rs, cs]

            def query_rows(tg, carry, bi=bi, rs=rs, cs=cs, b=b, kk=kk, vv=vv):
                base = pl.multiple_of(tg * SUBLANES, SUBLANES)
                at = pl.ds(rs.start + base, SUBLANES)
                b8 = b_ref[bi, at, cs]
                q8 = q_ref[bi, at, cs]
                out = []
                for j in range(SUBLANES):
                    decay = jnp.exp(jnp.where(srow <= base + j, b8[j:j + 1, :] - b, -jnp.inf))
                    a_col = jnp.sum(q8[j:j + 1, :] * kk * decay, axis=-1, keepdims=True)
                    out.append(jnp.sum(a_col * vv, axis=0, keepdims=True))
                oi_scr[pl.ds(base, SUBLANES), :] = jnp.concatenate(out, axis=0)
                return carry

            lax.fori_loop(0, c // SUBLANES, query_rows, 0)
            q0 = (q_ref[bi, rs, cs] * jnp.exp(b)).astype(BF16)
            k2 = (kk * jnp.exp(b_last - b)).astype(BF16)
            finish(chain, oi_scr[...], q0, k2, vv.astype(BF16), b_last)

    @pl.when(i == pl.num_programs(0) - 1)
    def _():
        for bi in range(q_ref.shape[0]):
            for h in range(N_HEADS):
                s_ref[bi, h] = st_scr[bi, h].T


def _hgrn(qr, cum_logf, kr, vr, worst, batch, seq, qr_s, logf_s, kr_s, vr_s, state_s):
    nb = seq // HGRN_BLOCK
    n_dec = qr_s.shape[0]
    per_step = n_dec // nb
    assert per_step * nb == n_dec
    spec = pl.BlockSpec((batch, HGRN_BLOCK, GROUP), lambda i: (0, i, 0))
    shape3 = (batch, seq, GROUP)
    rows_s = pl.BlockSpec((1, per_step, GROUP), lambda i: (i, 0, 0))
    state_spec = pl.BlockSpec((per_step, N_HEADS, HEAD, HEAD), lambda i: (i, 0, 0, 0))
    o_r, s_new, o_r_s, s_new_s = pl.pallas_call(
        _hgrn_prompt_kernel,
        grid=(nb,),
        in_specs=[spec] * 4 + [pl.BlockSpec((batch, 1, SUBLANES, HEAD), lambda i: (0, i, 0, 0))]
                 + [rows_s] * 4 + [state_spec],
        out_specs=[spec, pl.BlockSpec((batch, N_HEADS, HEAD, HEAD), lambda i: (0, 0, 0, 0)), rows_s, state_spec],
        out_shape=[jax.ShapeDtypeStruct(shape3, BF16),
                   jax.ShapeDtypeStruct((batch, N_HEADS, HEAD, HEAD), F32),
                   jax.ShapeDtypeStruct((nb, per_step, GROUP), F32),
                   jax.ShapeDtypeStruct(state_s.shape, F32)],
        scratch_shapes=[pltpu.VMEM((batch, N_HEADS, HEAD, HEAD), F32),
                        pltpu.VMEM((HGRN_CHUNK, HEAD), F32)],
        compiler_params=pltpu.CompilerParams(
            dimension_semantics=("arbitrary",), vmem_limit_bytes=VMEM_LIMIT),
        name="hgrn",
    )(*(a.reshape(shape3) for a in (qr, cum_logf, kr, vr)), worst.reshape(batch, nb, SUBLANES, HEAD),
      *(a.reshape(nb, per_step, GROUP) for a in (qr_s, logf_s, kr_s, vr_s)), state_s)
    return o_r.reshape(batch * seq, GROUP), s_new, o_r_s.reshape(n_dec, GROUP), s_new_s


def _lambda(lq1_ref, lk1_ref, lq2_ref, lk2_ref, lam_init):
    a = jnp.sum(lq1_ref[...] * lk1_ref[...], axis=-1, keepdims=True)
    b = jnp.sum(lq2_ref[...] * lk2_ref[...], axis=-1, keepdims=True)
    return jnp.exp(a) - jnp.exp(b) + lam_init


ATT_FULL, ATT_FULL_DIAG, ATT_DIAG = 0, 1, 2


def _attn_prompt_kernel(lam_init, qi_tab, kb_tab, kind_tab, first_tab,
                        q_ref, kt_ref, v_ref, lq1_ref, lk1_ref, lq2_ref, lk2_ref, sub_ref,
                        o_ref, m_scr, acc_scr):
    p = pl.program_id(2)
    kind = kind_tab[p]

    @pl.when(first_tab[p] == 1)
    def _():
        m_scr[...] = jnp.full_like(m_scr, -jnp.inf)
        acc_scr[...] = jnp.zeros_like(acc_scr)

    tq, tk = q_ref.shape[0], v_ref.shape[0]
    half_chunks = tq // HEAD

    def sweep(step_kind):
        masked = step_kind != ATT_FULL
        v_ext = jnp.concatenate([v_ref[...], jnp.ones((tk, HEAD), BF16)], axis=-1)
        lane = lax.broadcasted_iota(jnp.int32, (ATT_ROWS, HEAD), 1)
        row = lax.broadcasted_iota(jnp.int32, (ATT_ROWS, HEAD), 0)
        zero = jnp.zeros((ATT_ROWS, HEAD), BF16)
        group_chunks = ATT_ROWS // HEAD
        groups = range(tq // ATT_ROWS)
        for r in (reversed(groups) if masked else groups):
            rows = slice(r * ATT_ROWS, (r + 1) * ATT_ROWS)
            q = q_ref[rows, :]
            diag_chunks = (r + 1) * group_chunks
            n_chunks = {ATT_FULL: tk // HEAD, ATT_FULL_DIAG: half_chunks + diag_chunks,
                        ATT_DIAG: diag_chunks}[step_kind]
            kt = kt_ref[0, :, 0:n_chunks * HEAD]
            probs, alphas = [], []
            for mi, qm in enumerate((jnp.where(lane < A_DK, q, zero), jnp.where(lane >= A_DK, q, zero))):
                s = jnp.dot(qm, kt, preferred_element_type=F32)
                chunks = [s[:, c * HEAD:(c + 1) * HEAD] for c in range(n_chunks)]
                if masked:
                    for g in range(group_chunks):
                        c = n_chunks - group_chunks + g
                        chunks[c] = jnp.where(lane + g * HEAD <= row, chunks[c], -jnp.inf)
                mc = functools.reduce(jnp.maximum, chunks)
                m_prev = m_scr[mi, rows]
                m_new = jnp.maximum(m_prev, jnp.max(mc, axis=-1, keepdims=True))
                alphas.append(jnp.exp2(m_prev - m_new))
                probs.append(jnp.concatenate([jnp.exp2(sc - m_new).astype(BF16) for sc in chunks], axis=-1))
                m_scr[mi, rows] = m_new
            pv = jnp.dot(jnp.concatenate(probs, axis=0), v_ext[0:n_chunks * HEAD], preferred_element_type=F32)
            for mi, alpha in enumerate(alphas):
                acc_scr[mi, rows] = (jnp.concatenate([alpha, alpha], axis=-1) * acc_scr[mi, rows]
                                     + pv[mi * ATT_ROWS:(mi + 1) * ATT_ROWS])

    def finalize():
        lam = _lambda(lq1_ref, lk1_ref, lq2_ref, lk2_ref, lam_init)
        a0 = acc_scr[0]
        a1 = acc_scr[1]
        o = a0[:, :HEAD] / a0[:, HEAD:] - lam * (a1[:, :HEAD] / a1[:, HEAD:])
        o_ref[...] = (_rms(o, sub_ref[...]) * (1.0 - lam_init)).astype(o_ref.dtype)

    @pl.when(kind == ATT_FULL)
    def _():
        sweep(ATT_FULL)

    if tk > tq:
        @pl.when(kind == ATT_FULL_DIAG)
        def _():
            sweep(ATT_FULL_DIAG)
            finalize()

    @pl.when(kind == ATT_DIAG)
    def _():
        sweep(ATT_DIAG)
        finalize()


def _attn_prompt(aq_bf, akt_bf, av_bf, lq1, lk1, lq2, lk2, subln, lam_init, batch, seq):
    t = ATT_BLOCK
    ratio = ATT_KEY_BLOCK // t
    nq = seq // t
    assert ATT_ROWS % HEAD == 0 and ratio in (1, 2) and nq % ratio == 0
    steps = []
    for i in range(nq):
        for j in range(i // ratio):
            steps.append((i, j, ATT_FULL, int(j == 0)))
        steps.append((i, i // ratio, ATT_FULL_DIAG if i % ratio else ATT_DIAG, int(i < ratio)))
    qi_tab, kb_tab, kind_tab, first_tab = (jnp.asarray([s[c] for s in steps], jnp.int32) for c in range(4))
    q_spec = pl.BlockSpec((t, HEAD), lambda b, h, p, qi, kb, kind, first: (b * nq + qi[p], h))
    v_spec = pl.BlockSpec((ratio * t, HEAD),
                          lambda b, h, p, qi, kb, kind, first: (b * (nq // ratio) + kb[p], h))
    kt_spec = pl.BlockSpec((1, HEAD, ratio * t), lambda b, h, p, qi, kb, kind, first: (b, h, kb[p]))
    lam_spec = pl.BlockSpec((1, A_DK), lambda *_: (0, 0))
    sub_spec = pl.BlockSpec((1, HEAD), lambda *_: (0, 0))
    return pl.pallas_call(
        functools.partial(_attn_prompt_kernel, lam_init),
        grid_spec=pltpu.PrefetchScalarGridSpec(
            num_scalar_prefetch=4,
            grid=(batch, N_HEADS, len(steps)),
            in_specs=[q_spec, kt_spec, v_spec, lam_spec, lam_spec, lam_spec, lam_spec, sub_spec],
            out_specs=q_spec,
            scratch_shapes=[pltpu.VMEM((2, t, HEAD), F32), pltpu.VMEM((2, t, 2 * HEAD), F32)]),
        out_shape=jax.ShapeDtypeStruct((batch * seq, GROUP), BF16),
        compiler_params=pltpu.CompilerParams(
            dimension_semantics=("parallel", "parallel", "arbitrary"), vmem_limit_bytes=VMEM_LIMIT),
        name="attn_prompt",
    )(qi_tab, kb_tab, kind_tab, first_tab, aq_bf, akt_bf, av_bf, lq1, lk1, lq2, lk2, subln)


def _page_copies(pt_ref, item, n_pages, ck_hbm, cv_hbm, kbuf, vbuf, sem, slot):
    page_rows = ck_hbm.shape[1]
    out = []
    for j in range(n_pages):
        page = pt_ref[item * n_pages + j]
        out.append(pltpu.make_async_copy(ck_hbm.at[page], kbuf.at[slot, j], sem.at[slot, 0]))
        out.append(pltpu.make_async_copy(cv_hbm.at[page], vbuf.at[slot, pl.ds(j * page_rows, page_rows)],
                                         sem.at[slot, 1]))
    return out


def _decode_item(lam, lam_init, q, kn, vn, sub4, kpages, vrows):
    n_pages = kpages.shape[0]
    past = n_pages * kpages.shape[2]
    rows = 2 * N_HEADS
    ri = lax.broadcasted_iota(jnp.int32, (rows, GROUP), 0)
    li = lax.broadcasted_iota(jnp.int32, (rows, GROUP), 1)
    mine = (li // A_DK) == (ri % N_HEADS) * 2 + ri // N_HEADS
    qbd = jnp.where(mine, jnp.broadcast_to(q.astype(F32), (rows, GROUP)), 0.0)
    qbd_bf = qbd.astype(BF16)
    s = jnp.concatenate(
        [jnp.dot(qbd_bf, kpages[j].astype(BF16), preferred_element_type=F32) for j in range(n_pages)],
        axis=-1)
    s_new = jnp.sum(qbd * kn.astype(F32), axis=-1, keepdims=True)
    m = jnp.maximum(jnp.max(s, axis=-1, keepdims=True), s_new)
    p = jnp.exp2(s - m)
    p_new = jnp.exp2(s_new - m)
    inv_l = 1.0 / (jnp.sum(p, axis=-1, keepdims=True) + p_new)
    pn = p * inv_l
    pn_new = p_new * inv_l
    attn = pn - lam * pltpu.roll(pn, N_HEADS, 0)
    attn_new = pn_new - lam * pltpu.roll(pn_new, N_HEADS, 0)
    attn_bf = attn.astype(BF16)
    o = jnp.concatenate(
        [jnp.dot(attn_bf, vrows[pl.ds(h, past, stride=N_HEADS), :].astype(BF16), preferred_element_type=F32)
         for h in range(N_HEADS)], axis=-1) + attn_new * vn
    keep = jnp.logical_and(ri < N_HEADS, li // HEAD == ri)
    o = jnp.where(keep, o, 0.0)
    ms = jnp.sum(o * o, axis=-1, keepdims=True) * (1.0 / HEAD)
    o = o * lax.rsqrt(ms + EPS) * sub4 * (1.0 - lam_init)
    return jnp.sum(o, axis=0, keepdims=True)


def _tail_kernel(decode, *refs):
    if decode is None:
        (x_ref, or_ref, gate_ref, oa_ref, gn_ref, wo_ref, nf_ref, wg_ref, wu_ref, wd_ref, nfin_ref,
         y_ref, act_scr) = refs
    else:
        (pt_ref, x_ref, or_ref, gate_ref, oa_ref, gn_ref, wo_ref, nf_ref, wg_ref, wu_ref, wd_ref, nfin_ref,
         q_ref, kn_ref, vn_ref, lq1_ref, lk1_ref, lq2_ref, lk2_ref, sub4_ref, ck_hbm, cv_hbm,
         y_ref, os_ref, act_scr, kbuf, vbuf, sem) = refs
        lam_init, n_pages, items = decode
        step = pl.program_id(0)
        copies = functools.partial(_page_copies, pt_ref, n_pages=n_pages, ck_hbm=ck_hbm, cv_hbm=cv_hbm,
                                   kbuf=kbuf, vbuf=vbuf, sem=sem)

        @pl.when(step == 0)
        def _():
            for cp in copies(item=0, slot=0):
                cp.start()

    def decode_item(jj):
        if decode is None:
            return
        item = step * items + jj
        slot = jj % 2
        if jj + 1 < items:
            for cp in copies(item=item + 1, slot=1 - slot):
                cp.start()
        else:
            @pl.when(step + 1 < pl.num_programs(0))
            def _():
                for cp in copies(item=item + 1, slot=1 - slot):
                    cp.start()
        for cp in copies(item=item, slot=slot):
            cp.wait()
        lam = _lambda(lq1_ref, lk1_ref, lq2_ref, lk2_ref, lam_init)
        os_ref[jj] = _decode_item(lam, lam_init, q_ref[jj], kn_ref[jj], vn_ref[jj], sub4_ref[...],
                                  kbuf.at[slot], vbuf.at[slot])

    tm = x_ref.shape[0]
    d_ff = wg_ref.shape[1]
    sub = min(tm, TAIL_ROWS)
    groups = tm // sub
    per_group = 0 if decode is None else items // (2 * groups)
    for g in range(groups):
        rows = slice(g * sub, (g + 1) * sub)
        parts = []
        for h in range(N_HEADS):
            cs = slice(h * HEAD, (h + 1) * HEAD)
            parts.append((_rms(or_ref[rows, cs].astype(F32), gn_ref[...]) * gate_ref[rows, cs]).astype(BF16))
        parts.append(oa_ref[rows, :].astype(BF16))
        mix = jnp.concatenate(parts, axis=-1)
        x1 = x_ref[rows, :] + jnp.dot(mix, wo_ref[...], preferred_element_type=F32)
        h2 = _rms(x1, nf_ref[...]).astype(BF16)
        for jj in range(per_group):
            decode_item(2 * g * per_group + jj)
        for j in range(d_ff // FF_CHUNK):
            fs = slice(j * FF_CHUNK, (j + 1) * FF_CHUNK)
            gt = jnp.dot(h2, wg_ref[:, fs], preferred_element_type=F32)
            up = jnp.dot(h2, wu_ref[:, fs], preferred_element_type=F32)
            act_scr[rows, fs] = (gt * _sigmoid(gt) * up).astype(BF16)
        for jj in range(per_group):
            decode_item((2 * g + 1) * per_group + jj)
        y = x1 + jnp.dot(act_scr[rows, :], wd_ref[...], preferred_element_type=F32)
        y_ref[rows, :] = _rms(y, nfin_ref[...])


def _tail(x, o_r, gate, o_a, r_gnorm, wo_bf, norm_ffn, wg_bf, wu_bf, wd_bf, norm_final, tm, decode=None):
    rows, d = x.shape
    d_ff = wg_bf.shape[1]
    steps = rows // tm

    def resident(shape):
        return pl.BlockSpec(shape, lambda i, *_: (0, 0), pipeline_mode=pl.Buffered(1))

    def const(shape):
        return pl.BlockSpec(shape, lambda i, *_: (0,) * len(shape))

    def row_block(width):
        return pl.BlockSpec((tm, width), lambda i, *_: (i, 0))

    in_specs = [row_block(d), row_block(GROUP), row_block(GROUP), row_block(GROUP), const((1, HEAD)),
                resident((d, d)), const((1, d)), resident((d, d_ff)), resident((d, d_ff)), resident((d_ff, d)),
                const((1, d))]
    operands = [x, o_r, gate, o_a, r_gnorm, wo_bf, norm_ffn, wg_bf, wu_bf, wd_bf, norm_final]
    out_specs = [row_block(d)]
    out_shape = [jax.ShapeDtypeStruct((rows, d), F32)]
    scratch = [pltpu.VMEM((tm, d_ff), BF16)]
    config = None
    prefetch = []
    if decode is not None:
        lam_init, page_table, aq_bf, ak_bf, av, lams, subln4, cache_k, cache_v = decode
        n, n_pages = page_table.shape
        n_phys, page_size = cache_k.shape[0], cache_k.shape[1]
        items = n // steps
        assert items * steps == n and items % (2 * (tm // min(tm, TAIL_ROWS))) == 0
        ck = jnp.transpose(cache_k, (0, 2, 3, 4, 1)).reshape(n_phys, GROUP, page_size)
        cv = cache_v.reshape(n_phys, page_size * N_HEADS, HEAD)
        item_block = pl.BlockSpec((items, 1, GROUP), lambda i, *_: (i, 0, 0))
        in_specs += [item_block, item_block, item_block] + [const((1, A_DK))] * 4 + [
            const((1, GROUP)), pl.BlockSpec(memory_space=pl.ANY), pl.BlockSpec(memory_space=pl.ANY)]
        operands += [a.reshape(n, 1, GROUP) for a in (aq_bf, ak_bf, av)] + list(lams) + [subln4, ck, cv]
        out_specs.append(item_block)
        out_shape.append(jax.ShapeDtypeStruct((n, 1, GROUP), F32))
        scratch += [pltpu.VMEM((2, n_pages, GROUP, page_size), F32),
                    pltpu.VMEM((2, n_pages * page_size * N_HEADS, HEAD), F32),
                    pltpu.SemaphoreType.DMA((2, 2))]
        config = (lam_init, n_pages, items)
        prefetch = [page_table.reshape(-1)]
    out = pl.pallas_call(
        functools.partial(_tail_kernel, config),
        grid_spec=pltpu.PrefetchScalarGridSpec(
            num_scalar_prefetch=len(prefetch), grid=(steps,),
            in_specs=in_specs, out_specs=out_specs, scratch_shapes=scratch),
        out_shape=out_shape,
        compiler_params=pltpu.CompilerParams(
            dimension_semantics=("arbitrary",), vmem_limit_bytes=TAIL_VMEM_LIMIT),
        name="tail",
    )(*prefetch, *operands)
    if decode is None:
        return out[0]
    return out[0], out[1].reshape(n, GROUP)


def kernel(x_prompt, x_sample, cache_k, cache_v, state_hgrn, page_table, w_in, w_out, lb_param,
           r_gnorm, lam_q1, lam_k1, lam_q2, lam_k2, a_subln, norm_mix, norm_ffn, w_gate, w_up,
           w_down, norm_final):
    batch, seq, d = x_prompt.shape
    n_dec = x_sample.shape[0]
    depth = w_in.shape[0]
    assert depth == 1 and x_sample.shape[1] == 1
    hp = x_prompt.reshape(batch * seq, d)
    hs = x_sample.reshape(n_dec, d)
    lb_param = lb_param.astype(F32)
    nfin = norm_final.reshape(1, d)
    kp, vp, sp, kss, vss, sss = [], [], [], [], [], []
    for l in range(depth):
        lam_init = 0.8 - 0.6 * math.exp(-0.3 * l)
        w_in_bf = w_in[l].astype(BF16)
        wo_bf = w_out[l].astype(BF16)
        wg_bf = w_gate[l].astype(BF16)
        wu_bf = w_up[l].astype(BF16)
        wd_bf = w_down[l].astype(BF16)
        nmix = norm_mix[l].reshape(1, d)
        nffn = norm_ffn[l].reshape(1, d)
        gn = r_gnorm[l].reshape(1, HEAD)
        sub = a_subln[l].reshape(1, HEAD)
        lams = [a[l].reshape(1, A_DK).astype(F32) for a in (lam_q1, lam_k1, lam_q2, lam_k2)]

        qr_s, logf_s, kr_s, vr_s, gate_s, aq_s, akt_s, av_s, ak_s = _inproj(
            hs, nmix, w_in_bf, lb_param, l, n_dec, n_dec, prompt=False)

        assert IN_ROWS == HGRN_BLOCK
        qr, cum_logf, kr, vr, gate, aq_bf, akt, av_heads, akt_bf, av_bf, worst = _inproj(
            hp, nmix, w_in_bf, lb_param, l, ROW_BLOCK, seq, prompt=True)
        o_r, s_new, o_r_s, s_new_s = _hgrn(qr, cum_logf, kr, vr, worst, batch, seq,
                                           qr_s, logf_s, kr_s, vr_s, state_hgrn[l].astype(F32))
        o_a = _attn_prompt(aq_bf, akt_bf, av_bf, *lams, sub, lam_init, batch, seq)
        paged = (lam_init, page_table, aq_s, ak_s, av_s, lams, jnp.tile(sub, (1, N_HEADS)),
                 cache_k[l], cache_v[l])
        hp, o_a_s = _tail(hp, o_r, gate, o_a, gn, wo_bf, nffn, wg_bf, wu_bf, wd_bf, nfin, TAIL_BLOCK, paged)
        kp.append(jnp.transpose(akt.reshape(batch, N_HEADS, 2, A_DK, seq), (0, 4, 1, 2, 3)))
        vp.append(av_heads.reshape(batch, seq, N_HEADS, HEAD))
        sp.append(s_new)

        hs = _tail(hs, o_r_s, gate_s, o_a_s, gn, wo_bf, nffn, wg_bf, wu_bf, wd_bf, nfin, n_dec)
        kss.append(jnp.transpose(akt_s.reshape(N_HEADS, 2, A_DK, n_dec), (3, 0, 1, 2))[:, None])
        vss.append(av_s.reshape(n_dec, 1, N_HEADS, HEAD))
        sss.append(s_new_s)

    y_prompt = hp.reshape(batch, seq, d)
    y_sample = hs.reshape(n_dec, 1, d)
    return (y_prompt, y_sample, jnp.stack(kp), jnp.stack(vp), jnp.stack(sp),
            jnp.stack(kss), jnp.stack(vss), jnp.stack(sss))
```
